```python
import math
import jax
import jax.numpy as jnp
from jax import lax
import numpy as np

D_MODEL = 1024
BATCH = 4
SEQ = 8192
DEPTH = 1

CTX_LEN = 256
GRID_W = 64
CHUNK = 128
ROWS_PER_CHUNK = CHUNK // GRID_W
MIX_W = D_MODEL
S5_W = D_MODEL // 4
GM_W = MIX_W - S5_W
GM_HEAD_DIM = 128
GM_HEADS = GM_W // GM_HEAD_DIM
S5_GROUP = 16
S5_GROUPS = S5_W // S5_GROUP
S5_STATE = 64
IN_COLS = S5_W + 2 * GM_W
N_EXPERTS = 16
CAPACITY_FACTOR = 2
D_FF = 2816
ALPHA = (2.0 * DEPTH) ** 0.25
BETA = (8.0 * DEPTH) ** -0.25
LN_EPS = 1e-6

kernel_name = 'hybrid_gmlp_s5_ecmoe_prefix_dit_block'


def layer_norm(x, g=None, b=None):
    xf = x.astype(jnp.float32)
    mu = jnp.mean(xf, axis=-1, keepdims=True)
    var = jnp.mean(jnp.square(xf - mu), axis=-1, keepdims=True)
    y = (xf - mu) * lax.rsqrt(var + LN_EPS)
    if g is not None:
        y = y * g.astype(jnp.float32) + b.astype(jnp.float32)
    return y.astype(x.dtype)


def modulate(x, shift, scale):
    return layer_norm(x) * (1 + scale) + shift


def chunk_mlp(u, v, w_s, b_s, n_chunks):
    bn, length, _ = u.shape
    u = u.reshape(bn, n_chunks, CHUNK, GM_HEADS, GM_HEAD_DIM)
    v = layer_norm(v.reshape(bn, n_chunks, CHUNK, GM_HEADS, GM_HEAD_DIM))
    mixed = jnp.einsum('gqk,bnkgc->bnqgc', w_s, v) + jnp.swapaxes(b_s, 0, 1)[:, :, None]
    return (u * mixed).reshape(bn, length, GM_W)


def diag_scan(lam_bar, drive, s0=None):
    length = drive.shape[1]
    a = jnp.broadcast_to(lam_bar, (1, length) + lam_bar.shape)

    def combine(e1, e2):
        a1, b1 = e1
        a2, b2 = e2
        return a1 * a2, a2 * b1 + b2

    a_cum, s = lax.associative_scan(combine, (a, drive), axis=1)
    if s0 is not None:
        s = s + a_cum * s0[:, None]
    return s


def s5_direction(u_lat, u_ctx, a_re, a_im, log_step, b_re, b_im, c_re, c_im, need_ctx):
    f32 = jnp.float32
    lam = lax.complex(a_re.astype(f32), a_im.astype(f32))
    lam_bar = jnp.exp(lam * jnp.exp(log_step.astype(f32))[:, None])
    b_bar = ((lam_bar - 1.0) / lam)[:, :, None] * lax.complex(b_re.astype(f32), b_im.astype(f32))
    c_mat = lax.complex(c_re.astype(f32), c_im.astype(f32))

    def drive(u):
        return jnp.einsum('blgh,gph->blgp', u.astype(f32).astype(jnp.complex64), b_bar)

    def readout(s):
        return jnp.real(jnp.einsum('blgp,ghp->blgh', s, c_mat))

    s_ctx = diag_scan(lam_bar, drive(u_ctx))
    s_lat = diag_scan(lam_bar, drive(u_lat), s_ctx[:, -1])
    y_ctx = readout(s_ctx) if need_ctx else None
    return readout(s_lat), y_ctx


def s5_block(u_lat, u_ctx, a_re, a_im, log_step, b_re, b_im, c_re, c_im, d, w_glu, b_glu, need_ctx):
    dtype = u_lat.dtype

    def groups(u):
        return u.reshape(u.shape[0], u.shape[1], S5_GROUPS, S5_GROUP)

    def flip(t, rev):
        return jnp.flip(t, axis=1) if rev else t

    def glu(y):
        g = jax.nn.gelu(y.reshape(y.shape[0], y.shape[1], S5_W).astype(dtype))
        return g * jax.nn.sigmoid(g @ w_glu + b_glu)

    ug, ucg = groups(u_lat), groups(u_ctx)
    d32 = d.astype(jnp.float32)
    y = d32 * ug.astype(jnp.float32)
    yc = d32 * ucg.astype(jnp.float32) if need_ctx else None
    for direction in range(2):
        rev = direction == 1
        yd, ycd = s5_direction(flip(ug, rev), flip(ucg, rev), a_re[direction], a_im[direction],
                               log_step[direction], b_re[direction], b_im[direction],
                               c_re[direction], c_im[direction], need_ctx)
        y = y + flip(yd, rev)
        if need_ctx:
            yc = yc + flip(ycd, rev)
    out_ctx = glu(yc) if need_ctx else None
    return glu(y), out_ctx


def ec_moe(h, w_router, w_gate, w_up, w_down):
    bn, length, _ = h.shape
    cap = CAPACITY_FACTOR * length // N_EXPERTS
    aff = jax.nn.softmax((h @ w_router).astype(jnp.float32), axis=-1)
    gate, idx = lax.top_k(jnp.swapaxes(aff, 1, 2), cap)
    bidx = jnp.arange(bn)[:, None, None]
    xg = h[bidx, idx]
    hid = jax.nn.silu(jnp.einsum('becd,edf->becf', xg, w_gate)) * jnp.einsum('becd,edf->becf', xg, w_up)
    out = jnp.einsum('becf,efd->becd', hid, w_down) * gate[..., None].astype(h.dtype)
    return jnp.zeros_like(h).at[bidx, idx].add(out)


def hybrid_layer(x, xc, mod_lat, mod_ctx, n_chunks_lat, w_in, gm_ws, gm_bs, s5_a_re, s5_a_im, s5_log_step,
                 s5_b_re, s5_b_im, s5_c_re, s5_c_im, s5_d, s5_w_glu, s5_b_glu, w_out, ln1_g, ln1_b,
                 w_router, moe_w_gate, moe_w_up, moe_w_down, ln2_g, ln2_b, update_ctx):
    sh1, sc1, g1, sh2, sc2, g2 = jnp.split(mod_lat, 6, axis=-1)
    csh1, csc1, cg1, csh2, csc2, cg2 = jnp.split(mod_ctx, 6, axis=-1)
    z = modulate(x, sh1, sc1) @ w_in
    zc = modulate(xc, csh1, csc1) @ (w_in if update_ctx else w_in[:, :S5_W])
    s5_lat, s5_ctx = s5_block(z[..., :S5_W], zc[..., :S5_W], s5_a_re, s5_a_im, s5_log_step, s5_b_re, s5_b_im,
                              s5_c_re, s5_c_im, s5_d, s5_w_glu, s5_b_glu, update_ctx)
    gm_lat = chunk_mlp(jax.nn.gelu(z[..., S5_W:S5_W + GM_W]), jax.nn.gelu(z[..., S5_W + GM_W:]),
                       gm_ws, gm_bs, n_chunks_lat)
    mix = jnp.concatenate([s5_lat, gm_lat], axis=-1) @ w_out
    x_new = layer_norm(ALPHA * x + g1 * mix, ln1_g, ln1_b)
    moe = ec_moe(modulate(x_new, sh2, sc2), w_router, moe_w_gate, moe_w_up, moe_w_down)
    x_new = layer_norm(ALPHA * x_new + g2 * moe, ln2_g, ln2_b)
    if update_ctx:
        gm_ctx = chunk_mlp(jax.nn.gelu(zc[..., S5_W:S5_W + GM_W]), jax.nn.gelu(zc[..., S5_W + GM_W:]),
                           gm_ws, gm_bs, xc.shape[1] // CHUNK)
        mix_c = jnp.concatenate([s5_ctx, gm_ctx], axis=-1) @ w_out
        xc = layer_norm(ALPHA * xc + cg1 * mix_c, ln1_g, ln1_b)
        moe_c = ec_moe(modulate(xc, csh2, csc2), w_router, moe_w_gate, moe_w_up, moe_w_down)
        xc = layer_norm(ALPHA * xc + cg2 * moe_c, ln2_g, ln2_b)
    return x_new, xc


def setup_inputs(seed: int = 0) -> dict:
    key = jax.random.key(seed)
    ks = jax.random.split(key, 32)
    f32 = jnp.float32

    def nrm(k, shape, scale=1.0):
        return jax.random.normal(k, shape, f32) * scale

    L2 = (DEPTH, 2, S5_GROUPS)
    return {
        'x': nrm(ks[0], (BATCH, SEQ, D_MODEL)),
        'c': nrm(ks[1], (BATCH, D_MODEL)),
        'ctx': nrm(ks[2], (BATCH, CTX_LEN, D_MODEL)),
        'c_ctx': nrm(ks[3], (D_MODEL,)),
        'w_ada': nrm(ks[4], (DEPTH, D_MODEL, 6 * D_MODEL), D_MODEL ** -0.5),
        'b_ada': nrm(ks[5], (DEPTH, 6 * D_MODEL), 0.02),
        'w_in': nrm(ks[6], (DEPTH, D_MODEL, IN_COLS), D_MODEL ** -0.5),
        'gm_ws': nrm(ks[7], (DEPTH, GM_HEADS, CHUNK, CHUNK), CHUNK ** -0.5),
        'gm_bs': 1.0 + nrm(ks[8], (DEPTH, GM_HEADS, CHUNK), 0.02),
        's5_a_re': -0.5 + nrm(ks[9], L2 + (S5_STATE,), 0.01),
        's5_a_im': math.pi * jnp.arange(S5_STATE, dtype=f32) + nrm(ks[10], L2 + (S5_STATE,), 0.01),
        's5_log_step': jax.random.uniform(ks[11], L2, f32, math.log(1e-3), math.log(1e-1)),
        's5_b_re': nrm(ks[12], L2 + (S5_STATE, S5_GROUP), (2.0 * S5_GROUP) ** -0.5),
        's5_b_im': nrm(ks[13], L2 + (S5_STATE, S5_GROUP), (2.0 * S5_GROUP) ** -0.5),
        's5_c_re': nrm(ks[14], L2 + (S5_GROUP, S5_STATE), (2.0 * S5_STATE) ** -0.5),
        's5_c_im': nrm(ks[15], L2 + (S5_GROUP, S5_STATE), (2.0 * S5_STATE) ** -0.5),
        's5_d': nrm(ks[16], (DEPTH, S5_GROUPS, S5_GROUP)),
        's5_w_glu': nrm(ks[17], (DEPTH, S5_W, S5_W), S5_W ** -0.5),
        's5_b_glu': nrm(ks[18], (DEPTH, S5_W), 0.02),
        'w_out': nrm(ks[19], (DEPTH, MIX_W, D_MODEL), BETA * MIX_W ** -0.5),
        'ln1_g': 1.0 + nrm(ks[20], (DEPTH, D_MODEL), 0.02),
        'ln1_b': nrm(ks[21], (DEPTH, D_MODEL), 0.02),
        'w_router': nrm(ks[22], (DEPTH, D_MODEL, N_EXPERTS), D_MODEL ** -0.5),
        'moe_w_gate': nrm(ks[23], (DEPTH, N_EXPERTS, D_MODEL, D_FF), D_MODEL ** -0.5),
        'moe_w_up': nrm(ks[24], (DEPTH, N_EXPERTS, D_MODEL, D_FF), D_MODEL ** -0.5),
        'moe_w_down': nrm(ks[25], (DEPTH, N_EXPERTS, D_FF, D_MODEL), BETA * D_FF ** -0.5),
        'ln2_g': 1.0 + nrm(ks[26], (DEPTH, D_MODEL), 0.02),
        'ln2_b': nrm(ks[27], (DEPTH, D_MODEL), 0.02),
    }


def reference(x, c, ctx, c_ctx, w_ada, b_ada, w_in, gm_ws, gm_bs, s5_a_re, s5_a_im, s5_log_step,
              s5_b_re, s5_b_im, s5_c_re, s5_c_im, s5_d, s5_w_glu, s5_b_glu, w_out, ln1_g, ln1_b,
              w_router, moe_w_gate, moe_w_up, moe_w_down, ln2_g, ln2_b):
    rows = x.shape[1] // GRID_W
    n_chunks_lat = rows // ROWS_PER_CHUNK
    xc = ctx
    for l in range(DEPTH):
        mod_lat = (jax.nn.silu(c) @ w_ada[l] + b_ada[l])[:, None, :]
        mod_ctx = (jax.nn.silu(c_ctx) @ w_ada[l] + b_ada[l])[None, None, :]
        x, xc = hybrid_layer(x, xc, mod_lat, mod_ctx, n_chunks_lat, w_in[l], gm_ws[l], gm_bs[l],
                             s5_a_re[l], s5_a_im[l], s5_log_step[l], s5_b_re[l], s5_b_im[l],
                             s5_c_re[l], s5_c_im[l], s5_d[l], s5_w_glu[l], s5_b_glu[l], w_out[l],
                             ln1_g[l], ln1_b[l], w_router[l], moe_w_gate[l], moe_w_up[l],
                             moe_w_down[l], ln2_g[l], ln2_b[l], l < DEPTH - 1)
    return x
```

```python
import functools
import math

import jax
import jax.numpy as jnp
from jax import lax
from jax.experimental import pallas as pl
from jax.experimental.pallas import tpu as pltpu

D_MODEL = 1024
CHUNK = 128
S5_W = D_MODEL // 4
GM_W = D_MODEL - S5_W
GM_HEAD_DIM = 128
GM_HEADS = GM_W // GM_HEAD_DIM
S5_GROUP = 16
S5_GROUPS = S5_W // S5_GROUP
S5_STATE = 64
S5_STATES = S5_GROUPS * S5_STATE
IN_COLS = S5_W + 2 * GM_W
N_EXPERTS = 16
CAPACITY_FACTOR = 2
D_FF = 2816
DEPTH = 1
ALPHA = (2.0 * DEPTH) ** 0.25
LN_EPS = 1e-6

F32 = jnp.float32
BF16 = jnp.bfloat16
HIGHEST = lax.Precision.HIGHEST

LANES = 128
SUBLANES = 8
VMEM_LIMIT_BYTES = 58 * 1024 * 1024

TOK_TILE = 512
SCAN_T = 256
SCAN_LANES = 256
SCAN_HALF = SUBLANES // 2
FF_TILE = D_FF // 2
ROW_TILE = 256
COMB_TOK = 2048
COMPACT_WIN = CHUNK + SUBLANES
F32_TINY = 2.0 ** -126
BRACKET_GEO_STEPS = 8
BRACKET_LIN_STEPS = 26


def _layer_norm(x):
    mu = jnp.mean(x, axis=-1, keepdims=True)
    xc = x - mu
    var = jnp.mean(xc * xc, axis=-1, keepdims=True)
    return xc * lax.rsqrt(var + LN_EPS)


def _params(semantics, vmem=None):
    return pltpu.CompilerParams(dimension_semantics=semantics, vmem_limit_bytes=vmem)


def _adaln_kernel(c_ref, w_ref, b_ref, o_ref):
    c = c_ref[...]
    a = c * jax.nn.sigmoid(c)
    o_ref[...] = jnp.dot(a, w_ref[...], precision=HIGHEST, preferred_element_type=F32) + b_ref[...]


def _adaln(cc, w_ada, b_ada):
    n = w_ada.shape[1] // D_MODEL
    return pl.pallas_call(
        _adaln_kernel,
        grid=(n,),
        in_specs=[pl.BlockSpec((SUBLANES, D_MODEL), lambda j: (0, 0)),
                  pl.BlockSpec((D_MODEL, D_MODEL), lambda j: (0, j)),
                  pl.BlockSpec((1, D_MODEL), lambda j: (0, j))],
        out_specs=pl.BlockSpec((SUBLANES, D_MODEL), lambda j: (0, j)),
        out_shape=jax.ShapeDtypeStruct((SUBLANES, w_ada.shape[1]), F32),
        compiler_params=_params(("arbitrary",)),
        name="adaln",
    )(cc, w_ada, b_ada)


def _inproj_kernel(x_ref, mod_ref, w_ref, ws_ref, bs_ref, zs_ref, gm_ref):
    h = _layer_norm(x_ref[...]) * (1.0 + mod_ref[1:2, :]) + mod_ref[0:1, :]
    z = jnp.dot(h.astype(BF16), w_ref[...], preferred_element_type=F32)
    zs_ref[...] = z[:, :S5_W]
    for g in range(GM_HEADS):
        lo = S5_W + g * GM_HEAD_DIM
        u = jax.nn.gelu(z[:, lo:lo + GM_HEAD_DIM])
        v = jax.nn.gelu(z[:, lo + GM_W:lo + GM_W + GM_HEAD_DIM])
        vn = _layer_norm(v).astype(BF16)
        wsg = ws_ref[g]
        bsg = bs_ref[g]
        for c in range(TOK_TILE // CHUNK):
            rows = slice(c * CHUNK, (c + 1) * CHUNK)
            mixed = jnp.dot(wsg, vn[rows, :], preferred_element_type=F32) + bsg
            gm_ref[rows, g * GM_HEAD_DIM:(g + 1) * GM_HEAD_DIM] = (u[rows, :] * mixed).astype(BF16)


def _inproj(x, mod, w_in, gm_ws, gm_bs):
    bn, length, _ = x.shape
    return pl.pallas_call(
        _inproj_kernel,
        grid=(bn, length // TOK_TILE),
        in_specs=[pl.BlockSpec((None, TOK_TILE, D_MODEL), lambda b, i: (b, i, 0)),
                  pl.BlockSpec((None, 6, D_MODEL), lambda b, i: (b, 0, 0)),
                  pl.BlockSpec((D_MODEL, IN_COLS), lambda b, i: (0, 0)),
                  pl.BlockSpec((GM_HEADS, CHUNK, CHUNK), lambda b, i: (0, 0, 0)),
                  pl.BlockSpec((GM_HEADS, CHUNK, 1), lambda b, i: (0, 0, 0))],
        out_specs=[pl.BlockSpec((TOK_TILE, S5_W), lambda b, i: (i, b)),
                   pl.BlockSpec((None, TOK_TILE, GM_W), lambda b, i: (b, i, 0))],
        out_shape=[jax.ShapeDtypeStruct((length, bn * S5_W), F32),
                   jax.ShapeDtypeStruct((bn, length, GM_W), BF16)],
        compiler_params=_params(("arbitrary", "arbitrary"), VMEM_LIMIT_BYTES),
        name="inproj",
    )(x, mod, w_in, gm_ws, gm_bs)


def _ctxproj_kernel(x_ref, mod_ref, w_ref, o_ref):
    h = _layer_norm(x_ref[...]) * (1.0 + mod_ref[1:2, :]) + mod_ref[0:1, :]
    o_ref[...] = jnp.dot(h.astype(BF16), w_ref[...], preferred_element_type=F32)


def _ctxproj(ctx, mod_ctx, w_in):
    bn, clen, _ = ctx.shape
    return pl.pallas_call(
        _ctxproj_kernel,
        grid=(bn,),
        in_specs=[pl.BlockSpec((None, clen, D_MODEL), lambda b: (b, 0, 0)),
                  pl.BlockSpec((None, 6, D_MODEL), lambda b: (0, 0, 0)),
                  pl.BlockSpec((D_MODEL, S5_W), lambda b: (0, 0))],
        out_specs=pl.BlockSpec((clen, S5_W), lambda b: (0, b)),
        out_shape=jax.ShapeDtypeStruct((clen, bn * S5_W), F32),
        compiler_params=_params(("arbitrary",)),
        name="ctxproj",
    )(ctx, mod_ctx, w_in)


def _s5_tables(a_re, a_im, log_step, b_re, b_im, c_re, c_im):
    step = jnp.exp(log_step)[..., None]
    mag = jnp.exp(a_re * step)
    lb_re = mag * jnp.cos(a_im * step)
    lb_im = mag * jnp.sin(a_im * step)
    den = a_re * a_re + a_im * a_im
    q_re = ((lb_re - 1.0) * a_re + lb_im * a_im) / den
    q_im = (lb_im * a_re - (lb_re - 1.0) * a_im) / den
    bb_re = q_re[..., None] * b_re - q_im[..., None] * b_im
    bb_im = q_re[..., None] * b_im + q_im[..., None] * b_re
    eye = jnp.eye(S5_GROUPS, dtype=F32)

    def in_block(m):
        return jnp.einsum('dgph,gk->dghkp', m, eye).reshape(2, S5_W, S5_STATES)

    def out_block(m):
        return jnp.einsum('dghp,gk->dgpkh', m, eye).reshape(2, S5_STATES, S5_W)

    bmat = jnp.concatenate([in_block(bb_re), in_block(bb_im)], axis=2).astype(BF16)
    cmat = jnp.concatenate([out_block(c_re), out_block(-c_im)], axis=1).astype(BF16)
    l1 = jnp.stack([lb_re.reshape(2, S5_STATES), lb_im.reshape(2, S5_STATES)], axis=1)
    l2 = jnp.stack([l1[:, 0] * l1[:, 0] - l1[:, 1] * l1[:, 1], 2.0 * l1[:, 0] * l1[:, 1]], axis=1)
    zero = jnp.zeros_like(l1)

    def halves(top, bottom):
        t = jnp.concatenate([top[:, 0], top[:, 1]], axis=-1)[:, None, :]
        b = jnp.concatenate([bottom[:, 0], bottom[:, 1]], axis=-1)[:, None, :]
        return jnp.concatenate([jnp.repeat(t, SCAN_HALF, axis=1), jnp.repeat(b, SCAN_HALF, axis=1)], axis=1)

    first = jnp.stack([halves(zero, l1)[0], halves(l1, zero)[1]])
    carry = jnp.stack([halves(l1, l2)[0], halves(l2, l1)[1]])
    return bmat, cmat, jnp.stack([first, carry], axis=1)


def _scan_tiles(df_ref, db_ref, s_ref, tab_ref):
    n_tiles = df_ref.shape[0] // SUBLANES
    top = lax.broadcasted_iota(jnp.int32, (SUBLANES, SCAN_LANES), 0) < SCAN_HALF

    def cmul_add(a_re, a_im, x_re, x_im, y_re, y_im):
        return y_re + a_re * x_re - a_im * x_im, y_im + a_re * x_im + a_im * x_re

    for lb in range(S5_STATES // SCAN_LANES):
        re_cols = pl.ds(lb * SCAN_LANES, SCAN_LANES)
        im_cols = pl.ds(S5_STATES + lb * SCAN_LANES, SCAN_LANES)
        tabs = [[(tab_ref[d, k, :, re_cols], tab_ref[d, k, :, im_cols]) for k in range(2)] for d in range(2)]

        def step(d_ref, rows, s, tab, keep_top):
            (f_re, f_im), (c_re, c_im) = tab
            x_re, x_im = d_ref[rows, re_cols], d_ref[rows, im_cols]
            v_re, v_im = cmul_add(f_re, f_im, pltpu.roll(x_re, SCAN_HALF, 0), pltpu.roll(x_im, SCAN_HALF, 0),
                                  x_re, x_im)
            y_re, y_im = cmul_add(c_re, c_im, s[0], s[1], v_re, v_im)
            d_ref[rows, re_cols] = y_re
            d_ref[rows, im_cols] = y_im
            sel = top if keep_top else ~top
            return (jnp.where(sel, y_re, pltpu.roll(y_re, SCAN_HALF, 0)),
                    jnp.where(sel, y_im, pltpu.roll(y_im, SCAN_HALF, 0)))

        def body(i, carry, re_cols=re_cols, im_cols=im_cols, tabs=tabs, step=step):
            sf, sb = carry
            rows_f = pl.ds(pl.multiple_of(i * SUBLANES, SUBLANES), SUBLANES)
            rows_b = pl.ds(pl.multiple_of((n_tiles - 1 - i) * SUBLANES, SUBLANES), SUBLANES)
            return step(df_ref, rows_f, sf, tabs[0], False), step(db_ref, rows_b, sb, tabs[1], True)

        init = ((s_ref[0, :, re_cols], s_ref[0, :, im_cols]), (s_ref[1, :, re_cols], s_ref[1, :, im_cols]))
        sf, sb = lax.fori_loop(0, n_tiles, body, init)
        s_ref[0, :, re_cols], s_ref[0, :, im_cols] = sf
        s_ref[1, :, re_cols], s_ref[1, :, im_cols] = sb


def _s5scan_kernel(uf_ref, ub_ref, uc_ref, bmat_ref, cmat_ref, tab_ref, yf_ref, yb_ref, df_ref, db_ref, s_ref):
    j = pl.program_id(0)

    def drive(u_ref, d_ref, direction):
        d_ref[...] = jnp.dot(u_ref[...].astype(BF16), bmat_ref[direction], preferred_element_type=F32)

    @pl.when(j == 0)
    def _context():
        s_ref[...] = jnp.zeros_like(s_ref)
        drive(uc_ref, df_ref, 0)
        drive(uc_ref, db_ref, 1)
        _scan_tiles(df_ref, db_ref, s_ref, tab_ref)

    drive(uf_ref, df_ref, 0)
    drive(ub_ref, db_ref, 1)
    _scan_tiles(df_ref, db_ref, s_ref, tab_ref)
    yf_ref[...] = jnp.dot(df_ref[...].astype(BF16), cmat_ref[0], preferred_element_type=F32)
    yb_ref[...] = jnp.dot(db_ref[...].astype(BF16), cmat_ref[1], preferred_element_type=F32)


def _s5scan(u_lat, u_ctx, bmat, cmat, tab):
    rows = u_lat.shape[0]
    blk = SCAN_T * SCAN_HALF
    n = rows // blk
    assert u_ctx.shape[0] == blk
    return pl.pallas_call(
        _s5scan_kernel,
        grid=(n,),
        in_specs=[pl.BlockSpec((blk, S5_W), lambda j: (j, 0)),
                  pl.BlockSpec((blk, S5_W), lambda j: (n - 1 - j, 0)),
                  pl.BlockSpec((blk, S5_W), lambda j: (0, 0)),
                  pl.BlockSpec((2, S5_W, 2 * S5_STATES), lambda j: (0, 0, 0)),
                  pl.BlockSpec((2, 2 * S5_STATES, S5_W), lambda j: (0, 0, 0)),
                  pl.BlockSpec((2, 2, SUBLANES, 2 * S5_STATES), lambda j: (0, 0, 0, 0))],
        out_specs=[pl.BlockSpec((blk, S5_W), lambda j: (j, 0)),
                   pl.BlockSpec((blk, S5_W), lambda j: (n - 1 - j, 0))],
        out_shape=[jax.ShapeDtypeStruct((rows, S5_W), F32)] * 2,
        scratch_shapes=[pltpu.VMEM((blk, 2 * S5_STATES), F32),
                        pltpu.VMEM((blk, 2 * S5_STATES), F32),
                        pltpu.VMEM((2, SUBLANES, 2 * S5_STATES), F32)],
        compiler_params=_params(("arbitrary",), VMEM_LIMIT_BYTES),
        name="s5scan",
    )(u_lat, u_lat, u_ctx, bmat, cmat, tab)


def _outproj_kernel(x_ref, zs_ref, yf_ref, yb_ref, gm_ref, mod_ref, d_ref, wglu_ref, bglu_ref, wout_ref,
                    ln_ref, wr_ref, x1_ref, hp_ref, aff_ref):
    y = d_ref[...] * zs_ref[...] + yf_ref[...] + yb_ref[...]
    g = jax.nn.gelu(y)
    s5 = g * jax.nn.sigmoid(jnp.dot(g.astype(BF16), wglu_ref[...], preferred_element_type=F32) + bglu_ref[...])
    mix = (jnp.dot(s5.astype(BF16), wout_ref[0:S5_W, :], preferred_element_type=F32)
           + jnp.dot(gm_ref[...], wout_ref[S5_W:, :], preferred_element_type=F32))
    x1 = _layer_norm(ALPHA * x_ref[...] + mod_ref[2:3, :] * mix) * ln_ref[0:1, :] + ln_ref[1:2, :]
    x1_ref[...] = x1
    h = _layer_norm(x1) * (1.0 + mod_ref[4:5, :]) + mod_ref[3:4, :]
    logits = lax.dot_general(wr_ref[...], h, (((1,), (1,)), ((), ())), precision=HIGHEST,
                             preferred_element_type=F32)
    ex = jnp.exp(logits - jnp.max(logits, axis=0, keepdims=True))
    aff_ref[...] = ex / jnp.sum(ex, axis=0, keepdims=True)
    hp_ref[...] = h.reshape(TOK_TILE, 1, D_MODEL)


def _outproj(x, zs, yf, yb, gm, mod, d, w_glu, b_glu, w_out, ln1, w_rt):
    bn, length, _ = x.shape
    tok = lambda b, i: (b, i, 0)
    s5 = lambda b, i: (i, b)
    const2 = lambda b, i: (0, 0)
    return pl.pallas_call(
        _outproj_kernel,
        grid=(bn, length // TOK_TILE),
        in_specs=[pl.BlockSpec((None, TOK_TILE, D_MODEL), tok),
                  pl.BlockSpec((TOK_TILE, S5_W), s5),
                  pl.BlockSpec((TOK_TILE, S5_W), s5),
                  pl.BlockSpec((TOK_TILE, S5_W), s5),
                  pl.BlockSpec((None, TOK_TILE, GM_W), tok),
                  pl.BlockSpec((None, 6, D_MODEL), lambda b, i: (b, 0, 0)),
                  pl.BlockSpec((1, S5_W), const2),
                  pl.BlockSpec((S5_W, S5_W), const2),
                  pl.BlockSpec((1, S5_W), const2),
                  pl.BlockSpec((D_MODEL, D_MODEL), const2),
                  pl.BlockSpec((2, D_MODEL), const2),
                  pl.BlockSpec((N_EXPERTS, D_MODEL), const2)],
        out_specs=[pl.BlockSpec((None, TOK_TILE, D_MODEL), tok),
                   pl.BlockSpec((None, TOK_TILE, 1, D_MODEL), lambda b, i: (b, i, 0, 0)),
                   pl.BlockSpec((None, N_EXPERTS, TOK_TILE), lambda b, i: (b, 0, i))],
        out_shape=[jax.ShapeDtypeStruct((bn, length, D_MODEL), F32),
                   jax.ShapeDtypeStruct((bn, length, 1, D_MODEL), F32),
                   jax.ShapeDtypeStruct((bn, N_EXPERTS, length), F32)],
        compiler_params=_params(("arbitrary", "arbitrary"), VMEM_LIMIT_BYTES),
        name="outproj",
    )(x, zs, yf, yb, gm, mod, d, w_glu, b_glu, w_out, ln1, w_rt)


def _route_kernel(aff_ref, key_ref, offs_ref, *, cap):
    aff = aff_ref[...]
    ne, nr, _ = aff.shape

    def count(mask):
        return jnp.sum(jnp.sum(mask.astype(F32), axis=2, keepdims=True), axis=1, keepdims=True)

    normal = count(aff >= F32_TINY) >= cap
    lo = jnp.where(normal, F32_TINY, 0.0)
    hi = jnp.where(normal, 2.0, F32_TINY)
    for it in range(BRACKET_GEO_STEPS + BRACKET_LIN_STEPS):
        mid = 0.5 * (lo + hi)
        if it < BRACKET_GEO_STEPS:
            mid = jnp.where(normal, jnp.sqrt(lo * hi), mid)
        ok = count(aff >= mid) >= cap
        lo = jnp.where(ok, mid, lo)
        hi = jnp.where(ok, hi, mid)
    thr = jnp.min(jnp.min(jnp.where(aff >= lo, aff, 2.0), axis=2, keepdims=True), axis=1, keepdims=True)
    above = aff > thr
    tied = aff == thr
    need = cap - count(above)

    kk = lax.broadcasted_iota(jnp.int32, (LANES, LANES), 0)
    ll = lax.broadcasted_iota(jnp.int32, (LANES, LANES), 1)
    upper = (kk <= ll).astype(BF16)
    ones = jnp.ones((LANES, LANES), BF16)
    ri = lax.broadcasted_iota(jnp.int32, (nr, nr), 0)
    rj = lax.broadcasted_iota(jnp.int32, (nr, nr), 1)
    lower = (rj < ri).astype(BF16)

    def exclusive_prefix(mask):
        m = mask.astype(BF16).reshape(ne * nr, LANES)
        in_row = jnp.dot(m, upper, preferred_element_type=F32).reshape(ne, nr, LANES)
        row_tot = jnp.dot(m, ones, preferred_element_type=F32).reshape(ne, nr, LANES)
        row_off = jnp.stack([jnp.dot(lower, row_tot[e].astype(BF16), preferred_element_type=F32)
                             for e in range(ne)])
        return in_row - mask.astype(F32) + row_off, row_off

    tie_rank, _ = exclusive_prefix(tied)
    sel = above | (tied & (tie_rank < need))
    pos, row_off = exclusive_prefix(sel)
    key_ref[...] = jnp.where(sel, pos, -1.0)
    offs_ref[...] = row_off


def _route(aff4, cap):
    bn, ne, nr, _ = aff4.shape
    spec = pl.BlockSpec((None, ne, nr, LANES), lambda b: (b, 0, 0, 0))
    return pl.pallas_call(
        functools.partial(_route_kernel, cap=cap),
        grid=(bn,),
        in_specs=[spec],
        out_specs=[spec, spec],
        out_shape=[jax.ShapeDtypeStruct(aff4.shape, F32)] * 2,
        compiler_params=_params(("arbitrary",), VMEM_LIMIT_BYTES),
        name="route",
    )(aff4)


def _compact_kernel(offs_sref, key_ref, aff_ref, idx_ref, gate_ref, acc_i, acc_g, *, cap):
    be = pl.program_id(0) * pl.num_programs(1) + pl.program_id(1)
    nr = key_ref.shape[0]
    acc_i[...] = jnp.zeros_like(acc_i)
    acc_g[...] = jnp.zeros_like(acc_g)
    slot = lax.broadcasted_iota(jnp.int32, (COMPACT_WIN, LANES), 0)
    lane = lax.broadcasted_iota(jnp.int32, (1, LANES), 1)

    def body(r, carry):
        off = offs_sref[be * nr + r]
        base = pl.multiple_of((off // SUBLANES) * SUBLANES, SUBLANES)
        win = pl.ds(base, COMPACT_WIN)
        hit = (slot + base) == key_ref[r].astype(jnp.int32)
        tok = (lane + r * LANES).astype(F32)
        acc_i[win, :] += jnp.where(hit, tok, 0.0)
        acc_g[win, :] += jnp.where(hit, aff_ref[r], 0.0)
        return carry

    lax.fori_loop(0, nr, body, 0)
    idx_ref[...] = jnp.sum(acc_i[0:cap, :], axis=1, keepdims=True).astype(jnp.int32)
    gate_ref[...] = jnp.sum(acc_g[0:cap, :], axis=1, keepdims=True)


def _compact(row_offs, key5, aff5, cap):
    bn, ne, nr, _, _ = key5.shape
    spec_in = pl.BlockSpec((None, None, nr, 1, LANES), lambda b, e, offs: (b, e, 0, 0, 0))
    spec_out = pl.BlockSpec((None, None, cap, 1), lambda b, e, offs: (b, e, 0, 0))
    return pl.pallas_call(
        functools.partial(_compact_kernel, cap=cap),
        grid_spec=pltpu.PrefetchScalarGridSpec(
            num_scalar_prefetch=1,
            grid=(bn, ne),
            in_specs=[spec_in, spec_in],
            out_specs=[spec_out, spec_out],
            scratch_shapes=[pltpu.VMEM((cap + COMPACT_WIN, LANES), F32),
                            pltpu.VMEM((cap + COMPACT_WIN, LANES), F32)]),
        out_shape=[jax.ShapeDtypeStruct((bn, ne, cap, 1), jnp.int32),
                   jax.ShapeDtypeStruct((bn, ne, cap, 1), F32)],
        compiler_params=_params(("arbitrary", "arbitrary")),
        name="compact",
    )(row_offs, key5, aff5)


GATHER_UNROLL = 8


def _gather_kernel(idx_sref, h_ref, o_ref, rows_ref, *, cap):
    be = pl.program_id(0) * pl.num_programs(1) + pl.program_id(1)

    def body(i, carry):
        for k in range(GATHER_UNROLL):
            c = i * GATHER_UNROLL + k
            rows_ref[c] = h_ref[idx_sref[be * cap + c]]
        return carry

    lax.fori_loop(0, cap // GATHER_UNROLL, body, 0)
    for m in range(cap // ROW_TILE):
        rows = slice(m * ROW_TILE, (m + 1) * ROW_TILE)
        o_ref[rows, :] = rows_ref[rows].reshape(ROW_TILE, D_MODEL).astype(BF16)


def _gather(idx_flat, h, ne, cap):
    bn, length, _, _ = h.shape
    return pl.pallas_call(
        functools.partial(_gather_kernel, cap=cap),
        grid_spec=pltpu.PrefetchScalarGridSpec(
            num_scalar_prefetch=1,
            grid=(bn, ne),
            in_specs=[pl.BlockSpec((None, length, 1, D_MODEL), lambda b, e, idx: (b, 0, 0, 0),
                                   pipeline_mode=pl.Buffered(1))],
            out_specs=pl.BlockSpec((None, None, cap, D_MODEL), lambda b, e, idx: (b, e, 0, 0)),
            scratch_shapes=[pltpu.VMEM((cap, 1, D_MODEL), F32)]),
        out_shape=jax.ShapeDtypeStruct((bn, ne, cap, D_MODEL), BF16),
        compiler_params=_params(("arbitrary", "arbitrary"), VMEM_LIMIT_BYTES),
        name="gather",
    )(idx_flat, h)


def _experts_kernel(xg_ref, gate_ref, wg_ref, wu_ref, wd_ref, res_ref, acc_ref, *, cap):
    f = pl.program_id(2)

    for m in range(cap // ROW_TILE):
        rows = slice(m * ROW_TILE, (m + 1) * ROW_TILE)
        xr = xg_ref[rows, :]
        hg = jnp.dot(xr, wg_ref[...], preferred_element_type=F32)
        hu = jnp.dot(xr, wu_ref[...], preferred_element_type=F32)
        hid = (hg * jax.nn.sigmoid(hg) * hu).astype(BF16)
        part = jnp.dot(hid, wd_ref[...], preferred_element_type=F32)

        @pl.when(f == 0)
        def _first(part=part, rows=rows):
            acc_ref[rows, :] = part

        @pl.when(f == pl.num_programs(2) - 1)
        def _last(part=part, rows=rows):
            out = (acc_ref[rows, :] + part) * gate_ref[rows, :]
            res_ref[rows] = out.reshape(ROW_TILE, 1, D_MODEL)


def _experts(xg, gate, wg, wu, wd):
    bn, ne, cap, _ = xg.shape
    nf = D_FF // FF_TILE
    assert nf == 2
    return pl.pallas_call(
        functools.partial(_experts_kernel, cap=cap),
        grid=(bn, ne, nf),
        in_specs=[pl.BlockSpec((None, None, cap, D_MODEL), lambda b, e, f: (b, e, 0, 0)),
                  pl.BlockSpec((None, None, cap, 1), lambda b, e, f: (b, e, 0, 0)),
                  pl.BlockSpec((None, D_MODEL, FF_TILE), lambda b, e, f: (e, 0, f)),
                  pl.BlockSpec((None, D_MODEL, FF_TILE), lambda b, e, f: (e, 0, f)),
                  pl.BlockSpec((None, FF_TILE, D_MODEL), lambda b, e, f: (e, f, 0))],
        out_specs=pl.BlockSpec((None, None, cap, 1, D_MODEL), lambda b, e, f: (b, e, 0, 0, 0)),
        scratch_shapes=[pltpu.VMEM((cap, D_MODEL), F32)],
        out_shape=jax.ShapeDtypeStruct((bn, ne, cap, 1, D_MODEL), F32),
        compiler_params=_params(("arbitrary", "arbitrary", "arbitrary"), VMEM_LIMIT_BYTES),
        name="experts",
    )(xg, gate, wg, wu, wd)


SCATTER_UNROLL = 4


def _combine_kernel(idx_sref, bnd_sref, res_ref, x1_ref, mod_ref, ln_ref, o_ref, acc_ref, *, cap, nq):
    b = pl.program_id(0)
    q = pl.program_id(1)
    e = pl.program_id(2)
    ne = pl.num_programs(2)
    be = b * ne + e

    @pl.when(e == 0)
    def _init():
        acc_ref[...] = jnp.zeros_like(acc_ref)

    c_lo = bnd_sref[be * (nq + 1) + q]
    c_hi = bnd_sref[be * (nq + 1) + q + 1]
    t0 = q * COMB_TOK
    n_full = (c_hi - c_lo) // SCATTER_UNROLL

    def body(i, carry):
        base = c_lo + i * SCATTER_UNROLL
        toks = [idx_sref[be * cap + base + k] - t0 for k in range(SCATTER_UNROLL)]
        sums = [acc_ref[toks[k]] + res_ref[base + k] for k in range(SCATTER_UNROLL)]
        for k in range(SCATTER_UNROLL):
            acc_ref[toks[k]] = sums[k]
        return carry

    lax.fori_loop(0, n_full, body, 0)

    def tail(c, carry):
        tok = idx_sref[be * cap + c] - t0
        acc_ref[tok] = acc_ref[tok] + res_ref[c]
        return carry

    lax.fori_loop(c_lo + n_full * SCATTER_UNROLL, c_hi, tail, 0)

    @pl.when(e == ne - 1)
    def _finish():
        for m in range(COMB_TOK // TOK_TILE):
            rows = slice(m * TOK_TILE, (m + 1) * TOK_TILE)
            moe = acc_ref[rows].reshape(TOK_TILE, D_MODEL)
            y = ALPHA * x1_ref[rows, :] + mod_ref[5:6, :] * moe
            o_ref[rows, :] = _layer_norm(y) * ln_ref[0:1, :] + ln_ref[1:2, :]


def _combine(idx_flat, bounds, res, x1, mod, ln2, cap):
    bn, length, _ = x1.shape
    ne = res.shape[1]
    nq = length // COMB_TOK
    tok = lambda b, q, e, idx, bnd: (b, q, 0)
    return pl.pallas_call(
        functools.partial(_combine_kernel, cap=cap, nq=nq),
        grid_spec=pltpu.PrefetchScalarGridSpec(
            num_scalar_prefetch=2,
            grid=(bn, nq, ne),
            in_specs=[pl.BlockSpec((None, None, cap, 1, D_MODEL), lambda b, q, e, idx, bnd: (b, e, 0, 0, 0)),
                      pl.BlockSpec((None, COMB_TOK, D_MODEL), tok),
                      pl.BlockSpec((None, 6, D_MODEL), lambda b, q, e, idx, bnd: (b, 0, 0)),
                      pl.BlockSpec((2, D_MODEL), lambda b, q, e, idx, bnd: (0, 0))],
            out_specs=pl.BlockSpec((None, COMB_TOK, D_MODEL), tok),
            scratch_shapes=[pltpu.VMEM((COMB_TOK, 1, D_MODEL), F32)]),
        out_shape=jax.ShapeDtypeStruct((bn, length, D_MODEL), F32),
        compiler_params=_params(("arbitrary", "arbitrary", "arbitrary"), VMEM_LIMIT_BYTES),
        name="combine",
    )(idx_flat, bounds, res, x1, mod, ln2)


def kernel(x, c, ctx, c_ctx, w_ada, b_ada, w_in, gm_ws, gm_bs, s5_a_re, s5_a_im, s5_log_step, s5_b_re, s5_b_im,
           s5_c_re, s5_c_im, s5_d, s5_w_glu, s5_b_glu, w_out, ln1_g, ln1_b, w_router, moe_w_gate, moe_w_up,
           moe_w_down, ln2_g, ln2_b):
    assert w_ada.shape[0] == DEPTH == 1
    bn, length, _ = x.shape
    assert bn == SCAN_HALF and ctx.shape[1] == SCAN_T and length % COMB_TOK == 0
    cap = CAPACITY_FACTOR * length // N_EXPERTS

    cc = jnp.zeros((SUBLANES, D_MODEL), F32).at[:bn].set(c).at[bn].set(c_ctx)
    mod = _adaln(cc, w_ada[0], b_ada[0][None, :])
    mod_lat = mod[:bn].reshape(bn, 6, D_MODEL)
    mod_ctx = mod[bn:bn + 1].reshape(1, 6, D_MODEL)

    w_in_b = w_in[0].astype(BF16)
    zs, gm = _inproj(x, mod_lat, w_in_b, gm_ws[0].astype(BF16), gm_bs[0][:, :, None])
    zc = _ctxproj(ctx, mod_ctx, w_in_b)

    bmat, cmat, tab = _s5_tables(s5_a_re[0], s5_a_im[0], s5_log_step[0], s5_b_re[0], s5_b_im[0],
                                 s5_c_re[0], s5_c_im[0])
    yf, yb = _s5scan(zs.reshape(length * bn, S5_W), zc.reshape(SCAN_T * bn, S5_W), bmat, cmat, tab)
    yf = yf.reshape(length, bn * S5_W)
    yb = yb.reshape(length, bn * S5_W)

    x1, hp, aff = _outproj(x, zs, yf, yb, gm, mod_lat, s5_d[0].reshape(1, S5_W), s5_w_glu[0].astype(BF16),
                           s5_b_glu[0][None, :], w_out[0].astype(BF16), jnp.stack([ln1_g[0], ln1_b[0]]),
                           w_router[0].T)

    aff4 = aff.reshape(bn, N_EXPERTS, length // LANES, LANES)
    key4, offs4 = _route(aff4, cap)
    row_offs = offs4[..., 0].astype(jnp.int32)
    rows5 = (bn, N_EXPERTS, length // LANES, 1, LANES)
    idx, gate = _compact(row_offs.reshape(-1), key4.reshape(rows5), aff4.reshape(rows5), cap)
    idx_flat = idx.reshape(-1)

    xg = _gather(idx_flat, hp, N_EXPERTS, cap)
    res = _experts(xg, gate, moe_w_gate[0].astype(BF16), moe_w_up[0].astype(BF16), moe_w_down[0].astype(BF16))

    rows_per_q = COMB_TOK // LANES
    bounds = jnp.concatenate([row_offs[..., ::rows_per_q], jnp.full((bn, N_EXPERTS, 1), cap, jnp.int32)], axis=-1)
    return _combine(idx_flat, bounds.reshape(-1), res, x1, mod_lat, jnp.stack([ln2_g[0], ln2_b[0]]), cap)
```

```python
import functools

import jax
import jax.numpy as jnp
from jax import lax
from jax.experimental import pallas as pl
from jax.experimental.pallas import tpu as pltpu

D_MODEL = 1024
CHUNK = 128
S5_W = D_MODEL // 4
GM_W = D_MODEL - S5_W
GM_HEAD_DIM = 128
GM_HEADS = GM_W // GM_HEAD_DIM
S5_GROUP = 16
S5_GROUPS = S5_W // S5_GROUP
S5_STATE = 64
S5_STATES = S5_GROUPS * S5_STATE
IN_COLS = S5_W + 2 * GM_W
N_EXPERTS = 16
CAPACITY_FACTOR = 2
D_FF = 2816
DEPTH = 1
ALPHA = (2.0 * DEPTH) ** 0.25
LN_EPS = 1e-6

F32 = jnp.float32
BF16 = jnp.bfloat16
HIGHEST = lax.Precision.HIGHEST

LANES = 128
SUBLANES = 8
ROW_CHUNKS = D_MODEL // LANES
VMEM_LIMIT_BYTES = 58 * 1024 * 1024

TOK_TILE = 512
SCAN_T = 256
SCAN_LANES = 256
SCAN_HALF = SUBLANES // 2
FF_TILE = D_FF // 2
ROW_TILE = 256
COMPACT_WIN = CHUNK + SUBLANES
F32_TINY = 2.0 ** -126
BRACKET_GEO_STEPS = 8
BRACKET_LIN_STEPS = 26


def _layer_norm(x):
    mu = jnp.mean(x, axis=-1, keepdims=True)
    xc = x - mu
    var = jnp.mean(xc * xc, axis=-1, keepdims=True)
    return xc * lax.rsqrt(var + LN_EPS)


def _params(semantics, vmem=None):
    return pltpu.CompilerParams(dimension_semantics=semantics, vmem_limit_bytes=vmem)


def _store_row_tiles(ref, first_token, x):
    n = x.shape[0]
    for s in range(ROW_CHUNKS):
        ref[pl.ds(first_token * ROW_CHUNKS + s, n, stride=ROW_CHUNKS), :] = x[:, s * LANES:(s + 1) * LANES]


def _load_row_tiles(ref, first_token, n):
    return jnp.concatenate(
        [ref[pl.ds(first_token * ROW_CHUNKS + s, n, stride=ROW_CHUNKS), :] for s in range(ROW_CHUNKS)], axis=1)


def _row_tile(token):
    return pl.ds(pl.multiple_of(token * ROW_CHUNKS, ROW_CHUNKS), ROW_CHUNKS)


def _adaln_kernel(c_ref, w_ref, b_ref, o_ref):
    c = c_ref[...]
    a = c * jax.nn.sigmoid(c)
    o_ref[...] = jnp.dot(a, w_ref[...], precision=HIGHEST, preferred_element_type=F32) + b_ref[...]


def _adaln(cc, w_ada, b_ada):
    n = w_ada.shape[1] // D_MODEL
    return pl.pallas_call(
        _adaln_kernel,
        grid=(n,),
        in_specs=[pl.BlockSpec((SUBLANES, D_MODEL), lambda j: (0, 0)),
                  pl.BlockSpec((D_MODEL, D_MODEL), lambda j: (0, j)),
                  pl.BlockSpec((1, D_MODEL), lambda j: (0, j))],
        out_specs=pl.BlockSpec((SUBLANES, D_MODEL), lambda j: (0, j)),
        out_shape=jax.ShapeDtypeStruct((SUBLANES, w_ada.shape[1]), F32),
        compiler_params=_params(("arbitrary",)),
        name="adaln",
    )(cc, w_ada, b_ada)


def _inproj_kernel(x_ref, mod_ref, w_ref, ws_ref, bs_ref, zs_ref, gm_ref):
    h = _layer_norm(x_ref[...]) * (1.0 + mod_ref[1:2, :]) + mod_ref[0:1, :]
    z = jnp.dot(h.astype(BF16), w_ref[...], preferred_element_type=F32)
    zs_ref[...] = z[:, :S5_W]
    for g in range(GM_HEADS):
        lo = S5_W + g * GM_HEAD_DIM
        u = jax.nn.gelu(z[:, lo:lo + GM_HEAD_DIM])
        v = jax.nn.gelu(z[:, lo + GM_W:lo + GM_W + GM_HEAD_DIM])
        vn = _layer_norm(v).astype(BF16)
        wsg = ws_ref[g]
        bsg = bs_ref[g]
        for c in range(TOK_TILE // CHUNK):
            rows = slice(c * CHUNK, (c + 1) * CHUNK)
            mixed = jnp.dot(wsg, vn[rows, :], preferred_element_type=F32) + bsg
            gm_ref[rows, g * GM_HEAD_DIM:(g + 1) * GM_HEAD_DIM] = (u[rows, :] * mixed).astype(BF16)


def _inproj(x, mod, w_in, gm_ws, gm_bs):
    bn, length, _ = x.shape
    return pl.pallas_call(
        _inproj_kernel,
        grid=(bn, length // TOK_TILE),
        in_specs=[pl.BlockSpec((None, TOK_TILE, D_MODEL), lambda b, i: (b, i, 0)),
                  pl.BlockSpec((None, 6, D_MODEL), lambda b, i: (b, 0, 0)),
                  pl.BlockSpec((D_MODEL, IN_COLS), lambda b, i: (0, 0)),
                  pl.BlockSpec((GM_HEADS, CHUNK, CHUNK), lambda b, i: (0, 0, 0)),
                  pl.BlockSpec((GM_HEADS, CHUNK, 1), lambda b, i: (0, 0, 0))],
        out_specs=[pl.BlockSpec((TOK_TILE, S5_W), lambda b, i: (i, b)),
                   pl.BlockSpec((None, TOK_TILE, GM_W), lambda b, i: (b, i, 0))],
        out_shape=[jax.ShapeDtypeStruct((length, bn * S5_W), F32),
                   jax.ShapeDtypeStruct((bn, length, GM_W), BF16)],
        compiler_params=_params(("arbitrary", "arbitrary"), VMEM_LIMIT_BYTES),
        name="inproj",
    )(x, mod, w_in, gm_ws, gm_bs)


def _ctxproj_kernel(x_ref, mod_ref, w_ref, o_ref):
    h = _layer_norm(x_ref[...]) * (1.0 + mod_ref[1:2, :]) + mod_ref[0:1, :]
    o_ref[...] = jnp.dot(h.astype(BF16), w_ref[...], preferred_element_type=F32)


def _ctxproj(ctx, mod_ctx, w_in):
    bn, clen, _ = ctx.shape
    return pl.pallas_call(
        _ctxproj_kernel,
        grid=(bn,),
        in_specs=[pl.BlockSpec((None, clen, D_MODEL), lambda b: (b, 0, 0)),
                  pl.BlockSpec((None, 6, D_MODEL), lambda b: (0, 0, 0)),
                  pl.BlockSpec((D_MODEL, S5_W), lambda b: (0, 0))],
        out_specs=pl.BlockSpec((clen, S5_W), lambda b: (0, b)),
        out_shape=jax.ShapeDtypeStruct((clen, bn * S5_W), F32),
        compiler_params=_params(("arbitrary",)),
        name="ctxproj",
    )(ctx, mod_ctx, w_in)


def _s5_tables(a_re, a_im, log_step, b_re, b_im, c_re, c_im):
    step = jnp.exp(log_step)[..., None]
    mag = jnp.exp(a_re * step)
    lb_re = mag * jnp.cos(a_im * step)
    lb_im = mag * jnp.sin(a_im * step)
    den = a_re * a_re + a_im * a_im
    q_re = ((lb_re - 1.0) * a_re + lb_im * a_im) / den
    q_im = (lb_im * a_re - (lb_re - 1.0) * a_im) / den
    bb_re = q_re[..., None] * b_re - q_im[..., None] * b_im
    bb_im = q_re[..., None] * b_im + q_im[..., None] * b_re
    eye = jnp.eye(S5_GROUPS, dtype=F32)

    def in_block(m):
        return jnp.einsum('dgph,gk->dghkp', m, eye).reshape(2, S5_W, S5_STATES)

    def out_block(m):
        return jnp.einsum('dghp,gk->dgpkh', m, eye).reshape(2, S5_STATES, S5_W)

    bmat = jnp.concatenate([in_block(bb_re), in_block(bb_im)], axis=2).astype(BF16)
    cmat = jnp.concatenate([out_block(c_re), out_block(-c_im)], axis=1).astype(BF16)
    l1 = jnp.stack([lb_re.reshape(2, S5_STATES), lb_im.reshape(2, S5_STATES)], axis=1)
    l2 = jnp.stack([l1[:, 0] * l1[:, 0] - l1[:, 1] * l1[:, 1], 2.0 * l1[:, 0] * l1[:, 1]], axis=1)
    zero = jnp.zeros_like(l1)

    def halves(top, bottom):
        t = jnp.concatenate([top[:, 0], top[:, 1]], axis=-1)[:, None, :]
        b = jnp.concatenate([bottom[:, 0], bottom[:, 1]], axis=-1)[:, None, :]
        return jnp.concatenate([jnp.repeat(t, SCAN_HALF, axis=1), jnp.repeat(b, SCAN_HALF, axis=1)], axis=1)

    first = jnp.stack([halves(zero, l1)[0], halves(l1, zero)[1]])
    carry = jnp.stack([halves(l1, l2)[0], halves(l2, l1)[1]])
    return bmat, cmat, jnp.stack([first, carry], axis=1)


def _scan_tiles(df_ref, db_ref, s_ref, tab_ref):
    n_tiles = df_ref.shape[0] // SUBLANES
    top = lax.broadcasted_iota(jnp.int32, (SUBLANES, SCAN_LANES), 0) < SCAN_HALF

    def cmul_add(a_re, a_im, x_re, x_im, y_re, y_im):
        return y_re + a_re * x_re - a_im * x_im, y_im + a_re * x_im + a_im * x_re

    for lb in range(S5_STATES // SCAN_LANES):
        re_cols = pl.ds(lb * SCAN_LANES, SCAN_LANES)
        im_cols = pl.ds(S5_STATES + lb * SCAN_LANES, SCAN_LANES)
        tabs = [[(tab_ref[d, k, :, re_cols], tab_ref[d, k, :, im_cols]) for k in range(2)] for d in range(2)]

        def step(d_ref, rows, s, tab, keep_top):
            (f_re, f_im), (c_re, c_im) = tab
            x_re, x_im = d_ref[rows, re_cols], d_ref[rows, im_cols]
            v_re, v_im = cmul_add(f_re, f_im, pltpu.roll(x_re, SCAN_HALF, 0), pltpu.roll(x_im, SCAN_HALF, 0),
                                  x_re, x_im)
            y_re, y_im = cmul_add(c_re, c_im, s[0], s[1], v_re, v_im)
            d_ref[rows, re_cols] = y_re
            d_ref[rows, im_cols] = y_im
            sel = top if keep_top else ~top
            return (jnp.where(sel, y_re, pltpu.roll(y_re, SCAN_HALF, 0)),
                    jnp.where(sel, y_im, pltpu.roll(y_im, SCAN_HALF, 0)))

        def body(i, carry, re_cols=re_cols, im_cols=im_cols, tabs=tabs, step=step):
            sf, sb = carry
            rows_f = pl.ds(pl.multiple_of(i * SUBLANES, SUBLANES), SUBLANES)
            rows_b = pl.ds(pl.multiple_of((n_tiles - 1 - i) * SUBLANES, SUBLANES), SUBLANES)
            return step(df_ref, rows_f, sf, tabs[0], False), step(db_ref, rows_b, sb, tabs[1], True)

        init = ((s_ref[0, :, re_cols], s_ref[0, :, im_cols]), (s_ref[1, :, re_cols], s_ref[1, :, im_cols]))
        sf, sb = lax.fori_loop(0, n_tiles, body, init)
        s_ref[0, :, re_cols], s_ref[0, :, im_cols] = sf
        s_ref[1, :, re_cols], s_ref[1, :, im_cols] = sb


def _s5scan_kernel(uf_ref, ub_ref, uc_ref, bmat_ref, cmat_ref, tab_ref, yf_ref, yb_ref, df_ref, db_ref, s_ref):
    j = pl.program_id(0)

    def drive(u_ref, d_ref, direction):
        d_ref[...] = jnp.dot(u_ref[...].astype(BF16), bmat_ref[direction], preferred_element_type=F32)

    @pl.when(j == 0)
    def _context():
        s_ref[...] = jnp.zeros_like(s_ref)
        drive(uc_ref, df_ref, 0)
        drive(uc_ref, db_ref, 1)
        _scan_tiles(df_ref, db_ref, s_ref, tab_ref)

    drive(uf_ref, df_ref, 0)
    drive(ub_ref, db_ref, 1)
    _scan_tiles(df_ref, db_ref, s_ref, tab_ref)
    yf_ref[...] = jnp.dot(df_ref[...].astype(BF16), cmat_ref[0], preferred_element_type=F32)
    yb_ref[...] = jnp.dot(db_ref[...].astype(BF16), cmat_ref[1], preferred_element_type=F32)


def _s5scan(u_lat, u_ctx, bmat, cmat, tab):
    rows = u_lat.shape[0]
    blk = SCAN_T * SCAN_HALF
    n = rows // blk
    assert u_ctx.shape[0] == blk
    return pl.pallas_call(
        _s5scan_kernel,
        grid=(n,),
        in_specs=[pl.BlockSpec((blk, S5_W), lambda j: (j, 0)),
                  pl.BlockSpec((blk, S5_W), lambda j: (n - 1 - j, 0)),
                  pl.BlockSpec((blk, S5_W), lambda j: (0, 0)),
                  pl.BlockSpec((2, S5_W, 2 * S5_STATES), lambda j: (0, 0, 0)),
                  pl.BlockSpec((2, 2 * S5_STATES, S5_W), lambda j: (0, 0, 0)),
                  pl.BlockSpec((2, 2, SUBLANES, 2 * S5_STATES), lambda j: (0, 0, 0, 0))],
        out_specs=[pl.BlockSpec((blk, S5_W), lambda j: (j, 0)),
                   pl.BlockSpec((blk, S5_W), lambda j: (n - 1 - j, 0))],
        out_shape=[jax.ShapeDtypeStruct((rows, S5_W), F32)] * 2,
        scratch_shapes=[pltpu.VMEM((blk, 2 * S5_STATES), F32),
                        pltpu.VMEM((blk, 2 * S5_STATES), F32),
                        pltpu.VMEM((2, SUBLANES, 2 * S5_STATES), F32)],
        compiler_params=_params(("arbitrary",), VMEM_LIMIT_BYTES),
        name="s5scan",
    )(u_lat, u_lat, u_ctx, bmat, cmat, tab)


def _outproj_kernel(x_ref, zs_ref, yf_ref, yb_ref, gm_ref, mod_ref, d_ref, wglu_ref, bglu_ref, wout_ref,
                    ln_ref, wr_ref, x1_ref, hp_ref, aff_ref):
    y = d_ref[...] * zs_ref[...] + yf_ref[...] + yb_ref[...]
    g = jax.nn.gelu(y)
    s5 = g * jax.nn.sigmoid(jnp.dot(g.astype(BF16), wglu_ref[...], preferred_element_type=F32) + bglu_ref[...])
    mix = (jnp.dot(s5.astype(BF16), wout_ref[0:S5_W, :], preferred_element_type=F32)
           + jnp.dot(gm_ref[...], wout_ref[S5_W:, :], preferred_element_type=F32))
    x1 = _layer_norm(ALPHA * x_ref[...] + mod_ref[2:3, :] * mix) * ln_ref[0:1, :] + ln_ref[1:2, :]
    x1_ref[...] = x1
    h = _layer_norm(x1) * (1.0 + mod_ref[4:5, :]) + mod_ref[3:4, :]
    logits = lax.dot_general(wr_ref[...], h, (((1,), (1,)), ((), ())), precision=HIGHEST,
                             preferred_element_type=F32)
    ex = jnp.exp(logits - jnp.max(logits, axis=0, keepdims=True))
    aff_ref[...] = ex / jnp.sum(ex, axis=0, keepdims=True)
    _store_row_tiles(hp_ref, 0, h)


def _outproj(x, zs, yf, yb, gm, mod, d, w_glu, b_glu, w_out, ln1, w_rt):
    bn, length, _ = x.shape
    tok = lambda b, i: (b, i, 0)
    s5 = lambda b, i: (i, b)
    const2 = lambda b, i: (0, 0)
    return pl.pallas_call(
        _outproj_kernel,
        grid=(bn, length // TOK_TILE),
        in_specs=[pl.BlockSpec((None, TOK_TILE, D_MODEL), tok),
                  pl.BlockSpec((TOK_TILE, S5_W), s5),
                  pl.BlockSpec((TOK_TILE, S5_W), s5),
                  pl.BlockSpec((TOK_TILE, S5_W), s5),
                  pl.BlockSpec((None, TOK_TILE, GM_W), tok),
                  pl.BlockSpec((None, 6, D_MODEL), lambda b, i: (b, 0, 0)),
                  pl.BlockSpec((1, S5_W), const2),
                  pl.BlockSpec((S5_W, S5_W), const2),
                  pl.BlockSpec((1, S5_W), const2),
                  pl.BlockSpec((D_MODEL, D_MODEL), const2),
                  pl.BlockSpec((2, D_MODEL), const2),
                  pl.BlockSpec((N_EXPERTS, D_MODEL), const2)],
        out_specs=[pl.BlockSpec((None, TOK_TILE, D_MODEL), tok),
                   pl.BlockSpec((None, TOK_TILE * ROW_CHUNKS, LANES), tok),
                   pl.BlockSpec((None, N_EXPERTS, TOK_TILE), lambda b, i: (b, 0, i))],
        out_shape=[jax.ShapeDtypeStruct((bn, length, D_MODEL), F32),
                   jax.ShapeDtypeStruct((bn, length * ROW_CHUNKS, LANES), F32),
                   jax.ShapeDtypeStruct((bn, N_EXPERTS, length), F32)],
        compiler_params=_params(("arbitrary", "arbitrary"), VMEM_LIMIT_BYTES),
        name="outproj",
    )(x, zs, yf, yb, gm, mod, d, w_glu, b_glu, w_out, ln1, w_rt)


def _route_kernel(aff_ref, key_ref, offs_ref, *, cap):
    aff = aff_ref[...]
    ne, nr, _ = aff.shape

    def count(mask):
        return jnp.sum(jnp.sum(mask.astype(F32), axis=2, keepdims=True), axis=1, keepdims=True)

    normal = count(aff >= F32_TINY) >= cap
    lo = jnp.where(normal, F32_TINY, 0.0)
    hi = jnp.where(normal, 2.0, F32_TINY)
    for it in range(BRACKET_GEO_STEPS + BRACKET_LIN_STEPS):
        mid = 0.5 * (lo + hi)
        if it < BRACKET_GEO_STEPS:
            mid = jnp.where(normal, jnp.sqrt(lo * hi), mid)
        ok = count(aff >= mid) >= cap
        lo = jnp.where(ok, mid, lo)
        hi = jnp.where(ok, hi, mid)
    thr = jnp.min(jnp.min(jnp.where(aff >= lo, aff, 2.0), axis=2, keepdims=True), axis=1, keepdims=True)
    above = aff > thr
    tied = aff == thr
    need = cap - count(above)

    kk = lax.broadcasted_iota(jnp.int32, (LANES, LANES), 0)
    ll = lax.broadcasted_iota(jnp.int32, (LANES, LANES), 1)
    upper = (kk <= ll).astype(BF16)
    ones = jnp.ones((LANES, LANES), BF16)
    ri = lax.broadcasted_iota(jnp.int32, (nr, nr), 0)
    rj = lax.broadcasted_iota(jnp.int32, (nr, nr), 1)
    lower = (rj < ri).astype(BF16)

    def exclusive_prefix(mask):
        m = mask.astype(BF16).reshape(ne * nr, LANES)
        in_row = jnp.dot(m, upper, preferred_element_type=F32).reshape(ne, nr, LANES)
        row_tot = jnp.dot(m, ones, preferred_element_type=F32).reshape(ne, nr, LANES)
        row_off = jnp.stack([jnp.dot(lower, row_tot[e].astype(BF16), preferred_element_type=F32)
                             for e in range(ne)])
        return in_row - mask.astype(F32) + row_off, row_off

    tie_rank, _ = exclusive_prefix(tied)
    sel = above | (tied & (tie_rank < need))
    pos, row_off = exclusive_prefix(sel)
    key_ref[...] = jnp.where(sel, pos, -1.0)
    offs_ref[...] = row_off


def _route(aff4, cap):
    bn, ne, nr, _ = aff4.shape
    spec = pl.BlockSpec((None, ne, nr, LANES), lambda b: (b, 0, 0, 0))
    return pl.pallas_call(
        functools.partial(_route_kernel, cap=cap),
        grid=(bn,),
        in_specs=[spec],
        out_specs=[spec, spec],
        out_shape=[jax.ShapeDtypeStruct(aff4.shape, F32)] * 2,
        compiler_params=_params(("arbitrary",), VMEM_LIMIT_BYTES),
        name="route",
    )(aff4)


def _compact_kernel(offs_sref, key_ref, aff_ref, idx_ref, gate_ref, acc_i, acc_g, *, cap):
    be = pl.program_id(0) * pl.num_programs(1) + pl.program_id(1)
    nr = key_ref.shape[0]
    acc_i[...] = jnp.zeros_like(acc_i)
    acc_g[...] = jnp.zeros_like(acc_g)
    slot = lax.broadcasted_iota(jnp.int32, (COMPACT_WIN, LANES), 0)
    lane = lax.broadcasted_iota(jnp.int32, (1, LANES), 1)

    def body(r, carry):
        off = offs_sref[be * nr + r]
        base = pl.multiple_of((off // SUBLANES) * SUBLANES, SUBLANES)
        win = pl.ds(base, COMPACT_WIN)
        hit = (slot + base) == key_ref[r].astype(jnp.int32)
        tok = (lane + r * LANES).astype(F32)
        acc_i[win, :] += jnp.where(hit, tok, 0.0)
        acc_g[win, :] += jnp.where(hit, aff_ref[r], 0.0)
        return carry

    lax.fori_loop(0, nr, body, 0)
    idx_ref[...] = jnp.sum(acc_i[0:cap, :], axis=1, keepdims=True).astype(jnp.int32)
    gate_ref[...] = jnp.sum(acc_g[0:cap, :], axis=1, keepdims=True)


def _compact(row_offs, key5, aff5, cap):
    bn, ne, nr, _, _ = key5.shape
    spec_in = pl.BlockSpec((None, None, nr, 1, LANES), lambda b, e, offs: (b, e, 0, 0, 0))
    spec_out = pl.BlockSpec((None, None, cap, 1), lambda b, e, offs: (b, e, 0, 0))
    return pl.pallas_call(
        functools.partial(_compact_kernel, cap=cap),
        grid_spec=pltpu.PrefetchScalarGridSpec(
            num_scalar_prefetch=1,
            grid=(bn, ne),
            in_specs=[spec_in, spec_in],
            out_specs=[spec_out, spec_out],
            scratch_shapes=[pltpu.VMEM((cap + COMPACT_WIN, LANES), F32),
                            pltpu.VMEM((cap + COMPACT_WIN, LANES), F32)]),
        out_shape=[jax.ShapeDtypeStruct((bn, ne, cap, 1), jnp.int32),
                   jax.ShapeDtypeStruct((bn, ne, cap, 1), F32)],
        compiler_params=_params(("arbitrary", "arbitrary")),
        name="compact",
    )(row_offs, key5, aff5)


GATHER_UNROLL = 8


def _gather_kernel(idx_sref, h_ref, o_ref, rows_ref, *, cap):
    be = pl.program_id(0) * pl.num_programs(1) + pl.program_id(1)

    def body(i, carry):
        for k in range(GATHER_UNROLL):
            c = i * GATHER_UNROLL + k
            rows_ref[_row_tile(c), :] = h_ref[_row_tile(idx_sref[be * cap + c]), :]
        return carry

    lax.fori_loop(0, cap // GATHER_UNROLL, body, 0)
    for m in range(cap // ROW_TILE):
        o_ref[m * ROW_TILE:(m + 1) * ROW_TILE, :] = _load_row_tiles(rows_ref, m * ROW_TILE, ROW_TILE).astype(BF16)


def _gather(idx_flat, h, ne, cap):
    bn, rows, _ = h.shape
    return pl.pallas_call(
        functools.partial(_gather_kernel, cap=cap),
        grid_spec=pltpu.PrefetchScalarGridSpec(
            num_scalar_prefetch=1,
            grid=(bn, ne),
            in_specs=[pl.BlockSpec((None, rows, LANES), lambda b, e, idx: (b, 0, 0),
                                   pipeline_mode=pl.Buffered(1))],
            out_specs=pl.BlockSpec((None, None, cap, D_MODEL), lambda b, e, idx: (b, e, 0, 0)),
            scratch_shapes=[pltpu.VMEM((cap * ROW_CHUNKS, LANES), F32)]),
        out_shape=jax.ShapeDtypeStruct((bn, ne, cap, D_MODEL), BF16),
        compiler_params=_params(("arbitrary", "arbitrary"), VMEM_LIMIT_BYTES),
        name="gather",
    )(idx_flat, h)


def _experts_kernel(xg_ref, gate_ref, wg_ref, wu_ref, wd_ref, res_ref, acc_ref, *, cap):
    f = pl.program_id(2)

    for m in range(cap // ROW_TILE):
        rows = slice(m * ROW_TILE, (m + 1) * ROW_TILE)
        xr = xg_ref[rows, :]
        hg = jnp.dot(xr, wg_ref[...], preferred_element_type=F32)
        hu = jnp.dot(xr, wu_ref[...], preferred_element_type=F32)
        hid = (hg * jax.nn.sigmoid(hg) * hu).astype(BF16)
        part = jnp.dot(hid, wd_ref[...], preferred_element_type=F32)

        @pl.when(f == 0)
        def _first(part=part, rows=rows):
            acc_ref[rows, :] = part

        @pl.when(f == pl.num_programs(2) - 1)
        def _last(part=part, rows=rows):
            _store_row_tiles(res_ref, rows.start, (acc_ref[rows, :] + part) * gate_ref[rows, :])


def _experts(xg, gate, wg, wu, wd):
    bn, ne, cap, _ = xg.shape
    nf = D_FF // FF_TILE
    assert nf == 2
    return pl.pallas_call(
        functools.partial(_experts_kernel, cap=cap),
        grid=(bn, ne, nf),
        in_specs=[pl.BlockSpec((None, None, cap, D_MODEL), lambda b, e, f: (b, e, 0, 0)),
                  pl.BlockSpec((None, None, cap, 1), lambda b, e, f: (b, e, 0, 0)),
                  pl.BlockSpec((None, D_MODEL, FF_TILE), lambda b, e, f: (e, 0, f)),
                  pl.BlockSpec((None, D_MODEL, FF_TILE), lambda b, e, f: (e, 0, f)),
                  pl.BlockSpec((None, FF_TILE, D_MODEL), lambda b, e, f: (e, f, 0))],
        out_specs=pl.BlockSpec((None, None, cap * ROW_CHUNKS, LANES), lambda b, e, f: (b, e, 0, 0)),
        scratch_shapes=[pltpu.VMEM((cap, D_MODEL), F32)],
        out_shape=jax.ShapeDtypeStruct((bn, ne, cap * ROW_CHUNKS, LANES), F32),
        compiler_params=_params(("arbitrary", "arbitrary", "arbitrary"), VMEM_LIMIT_BYTES),
        name="experts",
    )(xg, gate, wg, wu, wd)


SCATTER_UNROLL = 4


def _combine_kernel(idx_sref, res_ref, x1_ref, mod_ref, ln_ref, o_ref, acc_ref, *, cap, ne):
    b = pl.program_id(0)
    step = pl.program_id(1)

    @pl.when(step == 0)
    def _init():
        acc_ref[...] = jnp.zeros_like(acc_ref)

    @pl.when(step < ne)
    def _scatter():
        base = (b * ne + step) * cap

        def body(i, carry):
            first = i * SCATTER_UNROLL
            toks = [_row_tile(idx_sref[base + first + k]) for k in range(SCATTER_UNROLL)]
            sums = [acc_ref[toks[k], :] + res_ref[_row_tile(first + k), :] for k in range(SCATTER_UNROLL)]
            for k in range(SCATTER_UNROLL):
                acc_ref[toks[k], :] = sums[k]
            return carry

        lax.fori_loop(0, cap // SCATTER_UNROLL, body, 0)

    @pl.when(step >= ne)
    def _finish():
        moe = _load_row_tiles(acc_ref, (step - ne) * TOK_TILE, TOK_TILE)
        y = ALPHA * x1_ref[...] + mod_ref[5:6, :] * moe
        o_ref[...] = _layer_norm(y) * ln_ref[0:1, :] + ln_ref[1:2, :]


def _combine(idx_flat, res, x1, mod, ln2, cap):
    bn, length, _ = x1.shape
    ne = res.shape[1]
    assert cap % SCATTER_UNROLL == 0
    tok = lambda b, s, idx: (b, jnp.maximum(s - ne, 0), 0)
    return pl.pallas_call(
        functools.partial(_combine_kernel, cap=cap, ne=ne),
        grid_spec=pltpu.PrefetchScalarGridSpec(
            num_scalar_prefetch=1,
            grid=(bn, ne + length // TOK_TILE),
            in_specs=[pl.BlockSpec((None, None, cap * ROW_CHUNKS, LANES),
                                   lambda b, s, idx: (b, jnp.minimum(s, ne - 1), 0, 0)),
                      pl.BlockSpec((None, TOK_TILE, D_MODEL), tok),
                      pl.BlockSpec((None, 6, D_MODEL), lambda b, s, idx: (b, 0, 0)),
                      pl.BlockSpec((2, D_MODEL), lambda b, s, idx: (0, 0))],
            out_specs=pl.BlockSpec((None, TOK_TILE, D_MODEL), tok),
            scratch_shapes=[pltpu.VMEM((length * ROW_CHUNKS, LANES), F32)]),
        out_shape=jax.ShapeDtypeStruct((bn, length, D_MODEL), F32),
        compiler_params=_params(("arbitrary", "arbitrary"), VMEM_LIMIT_BYTES),
        name="combine",
    )(idx_flat, res, x1, mod, ln2)


def kernel(x, c, ctx, c_ctx, w_ada, b_ada, w_in, gm_ws, gm_bs, s5_a_re, s5_a_im, s5_log_step, s5_b_re, s5_b_im,
           s5_c_re, s5_c_im, s5_d, s5_w_glu, s5_b_glu, w_out, ln1_g, ln1_b, w_router, moe_w_gate, moe_w_up,
           moe_w_down, ln2_g, ln2_b):
    assert w_ada.shape[0] == DEPTH == 1
    bn, length, _ = x.shape
    assert bn == SCAN_HALF and ctx.shape[1] == SCAN_T and length % TOK_TILE == 0
    cap = CAPACITY_FACTOR * length // N_EXPERTS

    cc = jnp.zeros((SUBLANES, D_MODEL), F32).at[:bn].set(c).at[bn].set(c_ctx)
    mod = _adaln(cc, w_ada[0], b_ada[0][None, :])
    mod_lat = mod[:bn].reshape(bn, 6, D_MODEL)
    mod_ctx = mod[bn:bn + 1].reshape(1, 6, D_MODEL)

    w_in_b = w_in[0].astype(BF16)
    zs, gm = _inproj(x, mod_lat, w_in_b, gm_ws[0].astype(BF16), gm_bs[0][:, :, None])
    zc = _ctxproj(ctx, mod_ctx, w_in_b)

    bmat, cmat, tab = _s5_tables(s5_a_re[0], s5_a_im[0], s5_log_step[0], s5_b_re[0], s5_b_im[0],
                                 s5_c_re[0], s5_c_im[0])
    yf, yb = _s5scan(zs.reshape(length * bn, S5_W), zc.reshape(SCAN_T * bn, S5_W), bmat, cmat, tab)
    yf = yf.reshape(length, bn * S5_W)
    yb = yb.reshape(length, bn * S5_W)

    x1, hp, aff = _outproj(x, zs, yf, yb, gm, mod_lat, s5_d[0].reshape(1, S5_W), s5_w_glu[0].astype(BF16),
                           s5_b_glu[0][None, :], w_out[0].astype(BF16), jnp.stack([ln1_g[0], ln1_b[0]]),
                           w_router[0].T)

    aff4 = aff.reshape(bn, N_EXPERTS, length // LANES, LANES)
    key4, offs4 = _route(aff4, cap)
    row_offs = offs4[..., 0].astype(jnp.int32)
    rows5 = (bn, N_EXPERTS, length // LANES, 1, LANES)
    idx, gate = _compact(row_offs.reshape(-1), key4.reshape(rows5), aff4.reshape(rows5), cap)
    idx_flat = idx.reshape(-1)

    xg = _gather(idx_flat, hp, N_EXPERTS, cap)
    res = _experts(xg, gate, moe_w_gate[0].astype(BF16), moe_w_up[0].astype(BF16), moe_w_down[0].astype(BF16))

    return _combine(idx_flat, res, x1, mod_lat, jnp.stack([ln2_g[0], ln2_b[0]]), cap)
```

```python
import functools

import jax
import jax.numpy as jnp
from jax import lax
from jax.experimental import pallas as pl
from jax.experimental.pallas import tpu as pltpu

D_MODEL = 1024
CHUNK = 128
S5_W = D_MODEL // 4
GM_W = D_MODEL - S5_W
GM_HEAD_DIM = 128
GM_HEADS = GM_W // GM_HEAD_DIM
S5_GROUP = 16
S5_GROUPS = S5_W // S5_GROUP
S5_STATE = 64
S5_STATES = S5_GROUPS * S5_STATE
IN_COLS = S5_W + 2 * GM_W
N_EXPERTS = 16
CAPACITY_FACTOR = 2
D_FF = 2816
DEPTH = 1
ALPHA = (2.0 * DEPTH) ** 0.25
LN_EPS = 1e-6

F32 = jnp.float32
BF16 = jnp.bfloat16
HIGHEST = lax.Precision.HIGHEST

LANES = 128
SUBLANES = 8
ROW_CHUNKS = D_MODEL // LANES
VMEM_LIMIT_BYTES = 58 * 1024 * 1024

TOK_TILE = 512
SCAN_T = 256
SCAN_LANES = 256
SCAN_HALF = SUBLANES // 2
ROW_TILE = 256
COMPACT_WIN = CHUNK + SUBLANES
F32_TINY = 2.0 ** -126
BRACKET_GEO_STEPS = 8
BRACKET_LIN_STEPS = 26


def _layer_norm(x):
    mu = jnp.mean(x, axis=-1, keepdims=True)
    xc = x - mu
    var = jnp.mean(xc * xc, axis=-1, keepdims=True)
    return xc * lax.rsqrt(var + LN_EPS)


def _params(semantics, vmem=None):
    return pltpu.CompilerParams(dimension_semantics=semantics, vmem_limit_bytes=vmem)


def _store_row_tiles(ref, first_token, x):
    n = x.shape[0]
    for s in range(ROW_CHUNKS):
        ref[pl.ds(first_token * ROW_CHUNKS + s, n, stride=ROW_CHUNKS), :] = x[:, s * LANES:(s + 1) * LANES]


def _load_row_tiles(ref, first_token, n):
    return jnp.concatenate(
        [ref[pl.ds(first_token * ROW_CHUNKS + s, n, stride=ROW_CHUNKS), :] for s in range(ROW_CHUNKS)], axis=1)


def _row_tile(token):
    return pl.ds(pl.multiple_of(token * ROW_CHUNKS, ROW_CHUNKS), ROW_CHUNKS)


def _adaln_kernel(c_ref, w_ref, b_ref, o_ref):
    c = c_ref[...]
    a = c * jax.nn.sigmoid(c)
    o_ref[...] = jnp.dot(a, w_ref[...], precision=HIGHEST, preferred_element_type=F32) + b_ref[...]


def _adaln(cc, w_ada, b_ada):
    n = w_ada.shape[1] // D_MODEL
    return pl.pallas_call(
        _adaln_kernel,
        grid=(n,),
        in_specs=[pl.BlockSpec((SUBLANES, D_MODEL), lambda j: (0, 0)),
                  pl.BlockSpec((D_MODEL, D_MODEL), lambda j: (0, j)),
                  pl.BlockSpec((1, D_MODEL), lambda j: (0, j))],
        out_specs=pl.BlockSpec((SUBLANES, D_MODEL), lambda j: (0, j)),
        out_shape=jax.ShapeDtypeStruct((SUBLANES, w_ada.shape[1]), F32),
        compiler_params=_params(("arbitrary",)),
        name="adaln",
    )(cc, w_ada, b_ada)


def _inproj_kernel(x_ref, mod_ref, w_ref, ws_ref, bs_ref, zs_ref, gm_ref):
    h = _layer_norm(x_ref[...]) * (1.0 + mod_ref[1:2, :]) + mod_ref[0:1, :]
    z = jnp.dot(h.astype(BF16), w_ref[...], preferred_element_type=F32)
    zs_ref[...] = z[:, :S5_W]
    for g in range(GM_HEADS):
        lo = S5_W + g * GM_HEAD_DIM
        u = jax.nn.gelu(z[:, lo:lo + GM_HEAD_DIM])
        v = jax.nn.gelu(z[:, lo + GM_W:lo + GM_W + GM_HEAD_DIM])
        vn = _layer_norm(v).astype(BF16)
        wsg = ws_ref[g]
        bsg = bs_ref[g]
        for c in range(TOK_TILE // CHUNK):
            rows = slice(c * CHUNK, (c + 1) * CHUNK)
            mixed = jnp.dot(wsg, vn[rows, :], preferred_element_type=F32) + bsg
            gm_ref[rows, g * GM_HEAD_DIM:(g + 1) * GM_HEAD_DIM] = (u[rows, :] * mixed).astype(BF16)


def _inproj(x, mod, w_in, gm_ws, gm_bs):
    bn, length, _ = x.shape
    return pl.pallas_call(
        _inproj_kernel,
        grid=(bn, length // TOK_TILE),
        in_specs=[pl.BlockSpec((None, TOK_TILE, D_MODEL), lambda b, i: (b, i, 0)),
                  pl.BlockSpec((None, 6, D_MODEL), lambda b, i: (b, 0, 0)),
                  pl.BlockSpec((D_MODEL, IN_COLS), lambda b, i: (0, 0)),
                  pl.BlockSpec((GM_HEADS, CHUNK, CHUNK), lambda b, i: (0, 0, 0)),
                  pl.BlockSpec((GM_HEADS, CHUNK, 1), lambda b, i: (0, 0, 0))],
        out_specs=[pl.BlockSpec((TOK_TILE, S5_W), lambda b, i: (i, b)),
                   pl.BlockSpec((None, TOK_TILE, GM_W), lambda b, i: (b, i, 0))],
        out_shape=[jax.ShapeDtypeStruct((length, bn * S5_W), F32),
                   jax.ShapeDtypeStruct((bn, length, GM_W), BF16)],
        compiler_params=_params(("arbitrary", "arbitrary"), VMEM_LIMIT_BYTES),
        name="inproj",
    )(x, mod, w_in, gm_ws, gm_bs)


def _ctxproj_kernel(x_ref, mod_ref, w_ref, o_ref):
    h = _layer_norm(x_ref[...]) * (1.0 + mod_ref[1:2, :]) + mod_ref[0:1, :]
    o_ref[...] = jnp.dot(h.astype(BF16), w_ref[...], preferred_element_type=F32)


def _ctxproj(ctx, mod_ctx, w_in):
    bn, clen, _ = ctx.shape
    return pl.pallas_call(
        _ctxproj_kernel,
        grid=(bn,),
        in_specs=[pl.BlockSpec((None, clen, D_MODEL), lambda b: (b, 0, 0)),
                  pl.BlockSpec((None, 6, D_MODEL), lambda b: (0, 0, 0)),
                  pl.BlockSpec((D_MODEL, S5_W), lambda b: (0, 0))],
        out_specs=pl.BlockSpec((clen, S5_W), lambda b: (0, b)),
        out_shape=jax.ShapeDtypeStruct((clen, bn * S5_W), F32),
        compiler_params=_params(("arbitrary",)),
        name="ctxproj",
    )(ctx, mod_ctx, w_in)


def _s5_tables(a_re, a_im, log_step, b_re, b_im, c_re, c_im):
    step = jnp.exp(log_step)[..., None]
    mag = jnp.exp(a_re * step)
    lb_re = mag * jnp.cos(a_im * step)
    lb_im = mag * jnp.sin(a_im * step)
    den = a_re * a_re + a_im * a_im
    q_re = ((lb_re - 1.0) * a_re + lb_im * a_im) / den
    q_im = (lb_im * a_re - (lb_re - 1.0) * a_im) / den
    bb_re = q_re[..., None] * b_re - q_im[..., None] * b_im
    bb_im = q_re[..., None] * b_im + q_im[..., None] * b_re
    eye = jnp.eye(S5_GROUPS, dtype=F32)

    def in_block(m):
        return jnp.einsum('dgph,gk->dghkp', m, eye).reshape(2, S5_W, S5_STATES)

    def out_block(m):
        return jnp.einsum('dghp,gk->dgpkh', m, eye).reshape(2, S5_STATES, S5_W)

    bmat = jnp.concatenate([in_block(bb_re), in_block(bb_im)], axis=2).astype(BF16)
    cmat = jnp.concatenate([out_block(c_re), out_block(-c_im)], axis=1).astype(BF16)
    l1 = jnp.stack([lb_re.reshape(2, S5_STATES), lb_im.reshape(2, S5_STATES)], axis=1)
    l2 = jnp.stack([l1[:, 0] * l1[:, 0] - l1[:, 1] * l1[:, 1], 2.0 * l1[:, 0] * l1[:, 1]], axis=1)
    zero = jnp.zeros_like(l1)

    def halves(top, bottom):
        t = jnp.concatenate([top[:, 0], top[:, 1]], axis=-1)[:, None, :]
        b = jnp.concatenate([bottom[:, 0], bottom[:, 1]], axis=-1)[:, None, :]
        return jnp.concatenate([jnp.repeat(t, SCAN_HALF, axis=1), jnp.repeat(b, SCAN_HALF, axis=1)], axis=1)

    first = jnp.stack([halves(zero, l1)[0], halves(l1, zero)[1]])
    carry = jnp.stack([halves(l1, l2)[0], halves(l2, l1)[1]])
    return bmat, cmat, jnp.stack([first, carry], axis=1)


def _scan_tiles(df_ref, db_ref, s_ref, tab_ref):
    n_tiles = df_ref.shape[0] // SUBLANES
    top = lax.broadcasted_iota(jnp.int32, (SUBLANES, SCAN_LANES), 0) < SCAN_HALF

    def cmul_add(a_re, a_im, x_re, x_im, y_re, y_im):
        return y_re + a_re * x_re - a_im * x_im, y_im + a_re * x_im + a_im * x_re

    for lb in range(S5_STATES // SCAN_LANES):
        re_cols = pl.ds(lb * SCAN_LANES, SCAN_LANES)
        im_cols = pl.ds(S5_STATES + lb * SCAN_LANES, SCAN_LANES)
        tabs = [[(tab_ref[d, k, :, re_cols], tab_ref[d, k, :, im_cols]) for k in range(2)] for d in range(2)]

        def step(d_ref, rows, s, tab, keep_top):
            (f_re, f_im), (c_re, c_im) = tab
            x_re, x_im = d_ref[rows, re_cols], d_ref[rows, im_cols]
            v_re, v_im = cmul_add(f_re, f_im, pltpu.roll(x_re, SCAN_HALF, 0), pltpu.roll(x_im, SCAN_HALF, 0),
                                  x_re, x_im)
            y_re, y_im = cmul_add(c_re, c_im, s[0], s[1], v_re, v_im)
            d_ref[rows, re_cols] = y_re
            d_ref[rows, im_cols] = y_im
            sel = top if keep_top else ~top
            return (jnp.where(sel, y_re, pltpu.roll(y_re, SCAN_HALF, 0)),
                    jnp.where(sel, y_im, pltpu.roll(y_im, SCAN_HALF, 0)))

        def body(i, carry, re_cols=re_cols, im_cols=im_cols, tabs=tabs, step=step):
            sf, sb = carry
            rows_f = pl.ds(pl.multiple_of(i * SUBLANES, SUBLANES), SUBLANES)
            rows_b = pl.ds(pl.multiple_of((n_tiles - 1 - i) * SUBLANES, SUBLANES), SUBLANES)
            return step(df_ref, rows_f, sf, tabs[0], False), step(db_ref, rows_b, sb, tabs[1], True)

        init = ((s_ref[0, :, re_cols], s_ref[0, :, im_cols]), (s_ref[1, :, re_cols], s_ref[1, :, im_cols]))
        sf, sb = lax.fori_loop(0, n_tiles, body, init)
        s_ref[0, :, re_cols], s_ref[0, :, im_cols] = sf
        s_ref[1, :, re_cols], s_ref[1, :, im_cols] = sb


def _s5scan_kernel(uf_ref, ub_ref, uc_ref, bmat_ref, cmat_ref, tab_ref, yf_ref, yb_ref, df_ref, db_ref, s_ref):
    j = pl.program_id(0)

    def drive(u_ref, d_ref, direction):
        d_ref[...] = jnp.dot(u_ref[...].astype(BF16), bmat_ref[direction], preferred_element_type=F32)

    @pl.when(j == 0)
    def _context():
        s_ref[...] = jnp.zeros_like(s_ref)
        drive(uc_ref, df_ref, 0)
        drive(uc_ref, db_ref, 1)
        _scan_tiles(df_ref, db_ref, s_ref, tab_ref)

    drive(uf_ref, df_ref, 0)
    drive(ub_ref, db_ref, 1)
    _scan_tiles(df_ref, db_ref, s_ref, tab_ref)
    yf_ref[...] = jnp.dot(df_ref[...].astype(BF16), cmat_ref[0], preferred_element_type=F32)
    yb_ref[...] = jnp.dot(db_ref[...].astype(BF16), cmat_ref[1], preferred_element_type=F32)


def _s5scan(u_lat, u_ctx, bmat, cmat, tab):
    rows = u_lat.shape[0]
    blk = SCAN_T * SCAN_HALF
    n = rows // blk
    assert u_ctx.shape[0] == blk
    return pl.pallas_call(
        _s5scan_kernel,
        grid=(n,),
        in_specs=[pl.BlockSpec((blk, S5_W), lambda j: (j, 0)),
                  pl.BlockSpec((blk, S5_W), lambda j: (n - 1 - j, 0)),
                  pl.BlockSpec((blk, S5_W), lambda j: (0, 0)),
                  pl.BlockSpec((2, S5_W, 2 * S5_STATES), lambda j: (0, 0, 0)),
                  pl.BlockSpec((2, 2 * S5_STATES, S5_W), lambda j: (0, 0, 0)),
                  pl.BlockSpec((2, 2, SUBLANES, 2 * S5_STATES), lambda j: (0, 0, 0, 0))],
        out_specs=[pl.BlockSpec((blk, S5_W), lambda j: (j, 0)),
                   pl.BlockSpec((blk, S5_W), lambda j: (n - 1 - j, 0))],
        out_shape=[jax.ShapeDtypeStruct((rows, S5_W), F32)] * 2,
        scratch_shapes=[pltpu.VMEM((blk, 2 * S5_STATES), F32),
                        pltpu.VMEM((blk, 2 * S5_STATES), F32),
                        pltpu.VMEM((2, SUBLANES, 2 * S5_STATES), F32)],
        compiler_params=_params(("arbitrary",), VMEM_LIMIT_BYTES),
        name="s5scan",
    )(u_lat, u_lat, u_ctx, bmat, cmat, tab)


def _outproj_kernel(x_ref, zs_ref, yf_ref, yb_ref, gm_ref, mod_ref, d_ref, wglu_ref, bglu_ref, wout_ref,
                    ln_ref, wr_ref, x1_ref, hp_ref, aff_ref):
    y = d_ref[...] * zs_ref[...] + yf_ref[...] + yb_ref[...]
    g = jax.nn.gelu(y)
    s5 = g * jax.nn.sigmoid(jnp.dot(g.astype(BF16), wglu_ref[...], preferred_element_type=F32) + bglu_ref[...])
    mix = (jnp.dot(s5.astype(BF16), wout_ref[0:S5_W, :], preferred_element_type=F32)
           + jnp.dot(gm_ref[...], wout_ref[S5_W:, :], preferred_element_type=F32))
    x1 = _layer_norm(ALPHA * x_ref[...] + mod_ref[2:3, :] * mix) * ln_ref[0:1, :] + ln_ref[1:2, :]
    x1_ref[...] = x1
    h = _layer_norm(x1) * (1.0 + mod_ref[4:5, :]) + mod_ref[3:4, :]
    logits = lax.dot_general(wr_ref[...], h, (((1,), (1,)), ((), ())), precision=HIGHEST,
                             preferred_element_type=F32)
    ex = jnp.exp(logits - jnp.max(logits, axis=0, keepdims=True))
    aff_ref[...] = ex / jnp.sum(ex, axis=0, keepdims=True)
    _store_row_tiles(hp_ref, 0, h)


def _outproj(x, zs, yf, yb, gm, mod, d, w_glu, b_glu, w_out, ln1, w_rt):
    bn, length, _ = x.shape
    tok = lambda b, i: (b, i, 0)
    s5 = lambda b, i: (i, b)
    const2 = lambda b, i: (0, 0)
    return pl.pallas_call(
        _outproj_kernel,
        grid=(bn, length // TOK_TILE),
        in_specs=[pl.BlockSpec((None, TOK_TILE, D_MODEL), tok),
                  pl.BlockSpec((TOK_TILE, S5_W), s5),
                  pl.BlockSpec((TOK_TILE, S5_W), s5),
                  pl.BlockSpec((TOK_TILE, S5_W), s5),
                  pl.BlockSpec((None, TOK_TILE, GM_W), tok),
                  pl.BlockSpec((None, 6, D_MODEL), lambda b, i: (b, 0, 0)),
                  pl.BlockSpec((1, S5_W), const2),
                  pl.BlockSpec((S5_W, S5_W), const2),
                  pl.BlockSpec((1, S5_W), const2),
                  pl.BlockSpec((D_MODEL, D_MODEL), const2),
                  pl.BlockSpec((2, D_MODEL), const2),
                  pl.BlockSpec((N_EXPERTS, D_MODEL), const2)],
        out_specs=[pl.BlockSpec((None, TOK_TILE, D_MODEL), tok),
                   pl.BlockSpec((None, TOK_TILE * ROW_CHUNKS, LANES), tok),
                   pl.BlockSpec((None, N_EXPERTS, TOK_TILE), lambda b, i: (b, 0, i))],
        out_shape=[jax.ShapeDtypeStruct((bn, length, D_MODEL), F32),
                   jax.ShapeDtypeStruct((bn, length * ROW_CHUNKS, LANES), F32),
                   jax.ShapeDtypeStruct((bn, N_EXPERTS, length), F32)],
        compiler_params=_params(("arbitrary", "arbitrary"), VMEM_LIMIT_BYTES),
        name="outproj",
    )(x, zs, yf, yb, gm, mod, d, w_glu, b_glu, w_out, ln1, w_rt)


def _route_kernel(aff_ref, key_ref, offs_ref, *, cap):
    aff = aff_ref[...]
    ne, nr, _ = aff.shape

    def count(mask):
        return jnp.sum(jnp.sum(mask.astype(F32), axis=2, keepdims=True), axis=1, keepdims=True)

    normal = count(aff >= F32_TINY) >= cap
    lo = jnp.where(normal, F32_TINY, 0.0)
    hi = jnp.where(normal, 2.0, F32_TINY)
    for it in range(BRACKET_GEO_STEPS + BRACKET_LIN_STEPS):
        mid = 0.5 * (lo + hi)
        if it < BRACKET_GEO_STEPS:
            mid = jnp.where(normal, jnp.sqrt(lo * hi), mid)
        ok = count(aff >= mid) >= cap
        lo = jnp.where(ok, mid, lo)
        hi = jnp.where(ok, hi, mid)
    thr = jnp.min(jnp.min(jnp.where(aff >= lo, aff, 2.0), axis=2, keepdims=True), axis=1, keepdims=True)
    above = aff > thr
    tied = aff == thr
    need = cap - count(above)

    kk = lax.broadcasted_iota(jnp.int32, (LANES, LANES), 0)
    ll = lax.broadcasted_iota(jnp.int32, (LANES, LANES), 1)
    upper = (kk <= ll).astype(BF16)
    ones = jnp.ones((LANES, LANES), BF16)
    ri = lax.broadcasted_iota(jnp.int32, (nr, nr), 0)
    rj = lax.broadcasted_iota(jnp.int32, (nr, nr), 1)
    lower = (rj < ri).astype(BF16)

    def exclusive_prefix(mask):
        m = mask.astype(BF16).reshape(ne * nr, LANES)
        in_row = jnp.dot(m, upper, preferred_element_type=F32).reshape(ne, nr, LANES)
        row_tot = jnp.dot(m, ones, preferred_element_type=F32).reshape(ne, nr, LANES)
        row_off = jnp.stack([jnp.dot(lower, row_tot[e].astype(BF16), preferred_element_type=F32)
                             for e in range(ne)])
        return in_row - mask.astype(F32) + row_off, row_off

    tie_rank, _ = exclusive_prefix(tied)
    sel = above | (tied & (tie_rank < need))
    pos, row_off = exclusive_prefix(sel)
    key_ref[...] = jnp.where(sel, pos, -1.0)
    offs_ref[...] = row_off


def _route(aff4, cap):
    bn, ne, nr, _ = aff4.shape
    spec = pl.BlockSpec((None, ne, nr, LANES), lambda b: (b, 0, 0, 0))
    return pl.pallas_call(
        functools.partial(_route_kernel, cap=cap),
        grid=(bn,),
        in_specs=[spec],
        out_specs=[spec, spec],
        out_shape=[jax.ShapeDtypeStruct(aff4.shape, F32)] * 2,
        compiler_params=_params(("arbitrary",), VMEM_LIMIT_BYTES),
        name="route",
    )(aff4)


def _compact_kernel(offs_sref, key_ref, aff_ref, idx_ref, gate_ref, acc_i, acc_g, *, cap):
    be = pl.program_id(0) * pl.num_programs(1) + pl.program_id(1)
    nr = key_ref.shape[0]
    acc_i[...] = jnp.zeros_like(acc_i)
    acc_g[...] = jnp.zeros_like(acc_g)
    slot = lax.broadcasted_iota(jnp.int32, (COMPACT_WIN, LANES), 0)
    lane = lax.broadcasted_iota(jnp.int32, (1, LANES), 1)

    def body(r, carry):
        off = offs_sref[be * nr + r]
        base = pl.multiple_of((off // SUBLANES) * SUBLANES, SUBLANES)
        win = pl.ds(base, COMPACT_WIN)
        hit = (slot + base) == key_ref[r].astype(jnp.int32)
        tok = (lane + r * LANES).astype(F32)
        acc_i[win, :] += jnp.where(hit, tok, 0.0)
        acc_g[win, :] += jnp.where(hit, aff_ref[r], 0.0)
        return carry

    lax.fori_loop(0, nr, body, 0)
    idx_ref[...] = jnp.sum(acc_i[0:cap, :], axis=1, keepdims=True).astype(jnp.int32)
    gate_ref[...] = jnp.sum(acc_g[0:cap, :], axis=1, keepdims=True)


def _compact(row_offs, key5, aff5, cap):
    bn, ne, nr, _, _ = key5.shape
    spec_in = pl.BlockSpec((None, None, nr, 1, LANES), lambda b, e, offs: (b, e, 0, 0, 0))
    spec_out = pl.BlockSpec((None, None, cap, 1), lambda b, e, offs: (b, e, 0, 0))
    return pl.pallas_call(
        functools.partial(_compact_kernel, cap=cap),
        grid_spec=pltpu.PrefetchScalarGridSpec(
            num_scalar_prefetch=1,
            grid=(bn, ne),
            in_specs=[spec_in, spec_in],
            out_specs=[spec_out, spec_out],
            scratch_shapes=[pltpu.VMEM((cap + COMPACT_WIN, LANES), F32),
                            pltpu.VMEM((cap + COMPACT_WIN, LANES), F32)]),
        out_shape=[jax.ShapeDtypeStruct((bn, ne, cap, 1), jnp.int32),
                   jax.ShapeDtypeStruct((bn, ne, cap, 1), F32)],
        compiler_params=_params(("arbitrary", "arbitrary")),
        name="compact",
    )(row_offs, key5, aff5)


GATHER_UNROLL = 8


def _gather_kernel(idx_sref, h_ref, o_ref, rows_ref, *, cap):
    be = pl.program_id(0) * pl.num_programs(1) + pl.program_id(1)

    def body(i, carry):
        for k in range(GATHER_UNROLL):
            c = i * GATHER_UNROLL + k
            rows_ref[_row_tile(c), :] = h_ref[_row_tile(idx_sref[be * cap + c]), :]
        return carry

    lax.fori_loop(0, cap // GATHER_UNROLL, body, 0)
    for m in range(cap // ROW_TILE):
        o_ref[m * ROW_TILE:(m + 1) * ROW_TILE, :] = _load_row_tiles(rows_ref, m * ROW_TILE, ROW_TILE).astype(BF16)


def _gather(idx_flat, h, ne, cap):
    bn, rows, _ = h.shape
    return pl.pallas_call(
        functools.partial(_gather_kernel, cap=cap),
        grid_spec=pltpu.PrefetchScalarGridSpec(
            num_scalar_prefetch=1,
            grid=(bn, ne),
            in_specs=[pl.BlockSpec((None, rows, LANES), lambda b, e, idx: (b, 0, 0),
                                   pipeline_mode=pl.Buffered(1))],
            out_specs=pl.BlockSpec((None, None, cap, D_MODEL), lambda b, e, idx: (b, e, 0, 0)),
            scratch_shapes=[pltpu.VMEM((cap * ROW_CHUNKS, LANES), F32)]),
        out_shape=jax.ShapeDtypeStruct((bn, ne, cap, D_MODEL), BF16),
        compiler_params=_params(("arbitrary", "arbitrary"), VMEM_LIMIT_BYTES),
        name="gather",
    )(idx_flat, h)


def _experts_kernel(xg_ref, gate_ref, wg_ref, wu_ref, wd_ref, res_ref, *, cap):
    for m in range(cap // ROW_TILE):
        rows = slice(m * ROW_TILE, (m + 1) * ROW_TILE)
        xr = xg_ref[rows, :]
        hg = jnp.dot(xr, wg_ref[...], preferred_element_type=F32)
        hu = jnp.dot(xr, wu_ref[...], preferred_element_type=F32)
        hid = (hg * jax.nn.sigmoid(hg) * hu).astype(BF16)
        out = jnp.dot(hid, wd_ref[...], preferred_element_type=F32)
        _store_row_tiles(res_ref, rows.start, out * gate_ref[rows, :])


def _experts(xg, gate, wg, wu, wd):
    bn, ne, cap, _ = xg.shape
    return pl.pallas_call(
        functools.partial(_experts_kernel, cap=cap),
        grid=(bn, ne),
        in_specs=[pl.BlockSpec((None, None, cap, D_MODEL), lambda b, e: (b, e, 0, 0)),
                  pl.BlockSpec((None, None, cap, 1), lambda b, e: (b, e, 0, 0)),
                  pl.BlockSpec((None, D_MODEL, D_FF), lambda b, e: (e, 0, 0)),
                  pl.BlockSpec((None, D_MODEL, D_FF), lambda b, e: (e, 0, 0)),
                  pl.BlockSpec((None, D_FF, D_MODEL), lambda b, e: (e, 0, 0))],
        out_specs=pl.BlockSpec((None, None, cap * ROW_CHUNKS, LANES), lambda b, e: (b, e, 0, 0)),
        out_shape=jax.ShapeDtypeStruct((bn, ne, cap * ROW_CHUNKS, LANES), F32),
        compiler_params=_params(("arbitrary", "arbitrary"), VMEM_LIMIT_BYTES),
        name="experts",
    )(xg, gate, wg, wu, wd)


SCATTER_UNROLL = 4


def _combine_kernel(idx_sref, res_ref, x1_ref, mod_ref, ln_ref, o_ref, acc_ref, *, cap, ne):
    b = pl.program_id(0)
    step = pl.program_id(1)

    @pl.when(step == 0)
    def _init():
        acc_ref[...] = jnp.zeros_like(acc_ref)

    @pl.when(step < ne)
    def _scatter():
        base = (b * ne + step) * cap

        def body(i, carry):
            first = i * SCATTER_UNROLL
            toks = [_row_tile(idx_sref[base + first + k]) for k in range(SCATTER_UNROLL)]
            sums = [acc_ref[toks[k], :] + res_ref[_row_tile(first + k), :] for k in range(SCATTER_UNROLL)]
            for k in range(SCATTER_UNROLL):
                acc_ref[toks[k], :] = sums[k]
            return carry

        lax.fori_loop(0, cap // SCATTER_UNROLL, body, 0)

    @pl.when(step >= ne)
    def _finish():
        moe = _load_row_tiles(acc_ref, (step - ne) * TOK_TILE, TOK_TILE)
        y = ALPHA * x1_ref[...] + mod_ref[5:6, :] * moe
        o_ref[...] = _layer_norm(y) * ln_ref[0:1, :] + ln_ref[1:2, :]


def _combine(idx_flat, res, x1, mod, ln2, cap):
    bn, length, _ = x1.shape
    ne = res.shape[1]
    assert cap % SCATTER_UNROLL == 0
    tok = lambda b, s, idx: (b, jnp.maximum(s - ne, 0), 0)
    return pl.pallas_call(
        functools.partial(_combine_kernel, cap=cap, ne=ne),
        grid_spec=pltpu.PrefetchScalarGridSpec(
            num_scalar_prefetch=1,
            grid=(bn, ne + length // TOK_TILE),
            in_specs=[pl.BlockSpec((None, None, cap * ROW_CHUNKS, LANES),
                                   lambda b, s, idx: (b, jnp.minimum(s, ne - 1), 0, 0)),
                      pl.BlockSpec((None, TOK_TILE, D_MODEL), tok),
                      pl.BlockSpec((None, 6, D_MODEL), lambda b, s, idx: (b, 0, 0)),
                      pl.BlockSpec((2, D_MODEL), lambda b, s, idx: (0, 0))],
            out_specs=pl.BlockSpec((None, TOK_TILE, D_MODEL), tok),
            scratch_shapes=[pltpu.VMEM((length * ROW_CHUNKS, LANES), F32)]),
        out_shape=jax.ShapeDtypeStruct((bn, length, D_MODEL), F32),
        compiler_params=_params(("arbitrary", "arbitrary"), VMEM_LIMIT_BYTES),
        name="combine",
    )(idx_flat, res, x1, mod, ln2)


def kernel(x, c, ctx, c_ctx, w_ada, b_ada, w_in, gm_ws, gm_bs, s5_a_re, s5_a_im, s5_log_step, s5_b_re, s5_b_im,
           s5_c_re, s5_c_im, s5_d, s5_w_glu, s5_b_glu, w_out, ln1_g, ln1_b, w_router, moe_w_gate, moe_w_up,
           moe_w_down, ln2_g, ln2_b):
    assert w_ada.shape[0] == DEPTH == 1
    bn, length, _ = x.shape
    assert bn == SCAN_HALF and ctx.shape[1] == SCAN_T and length % TOK_TILE == 0
    cap = CAPACITY_FACTOR * length // N_EXPERTS

    cc = jnp.zeros((SUBLANES, D_MODEL), F32).at[:bn].set(c).at[bn].set(c_ctx)
    mod = _adaln(cc, w_ada[0], b_ada[0][None, :])
    mod_lat = mod[:bn].reshape(bn, 6, D_MODEL)
    mod_ctx = mod[bn:bn + 1].reshape(1, 6, D_MODEL)

    w_in_b = w_in[0].astype(BF16)
    zs, gm = _inproj(x, mod_lat, w_in_b, gm_ws[0].astype(BF16), gm_bs[0][:, :, None])
    zc = _ctxproj(ctx, mod_ctx, w_in_b)

    bmat, cmat, tab = _s5_tables(s5_a_re[0], s5_a_im[0], s5_log_step[0], s5_b_re[0], s5_b_im[0],
                                 s5_c_re[0], s5_c_im[0])
    yf, yb = _s5scan(zs.reshape(length * bn, S5_W), zc.reshape(SCAN_T * bn, S5_W), bmat, cmat, tab)
    yf = yf.reshape(length, bn * S5_W)
    yb = yb.reshape(length, bn * S5_W)

    x1, hp, aff = _outproj(x, zs, yf, yb, gm, mod_lat, s5_d[0].reshape(1, S5_W), s5_w_glu[0].astype(BF16),
                           s5_b_glu[0][None, :], w_out[0].astype(BF16), jnp.stack([ln1_g[0], ln1_b[0]]),
                           w_router[0].T)

    aff4 = aff.reshape(bn, N_EXPERTS, length // LANES, LANES)
    key4, offs4 = _route(aff4, cap)
    row_offs = offs4[..., 0].astype(jnp.int32)
    rows5 = (bn, N_EXPERTS, length // LANES, 1, LANES)
    idx, gate = _compact(row_offs.reshape(-1), key4.reshape(rows5), aff4.reshape(rows5), cap)
    idx_flat = idx.reshape(-1)

    xg = _gather(idx_flat, hp, N_EXPERTS, cap)
    res = _experts(xg, gate, moe_w_gate[0].astype(BF16), moe_w_up[0].astype(BF16), moe_w_down[0].astype(BF16))

    return _combine(idx_flat, res, x1, mod_lat, jnp.stack([ln2_g[0], ln2_b[0]]), cap)
```

```python
import functools

import jax
import jax.numpy as jnp
from jax import lax
from jax.experimental import pallas as pl
from jax.experimental.pallas import tpu as pltpu

D_MODEL = 1024
CHUNK = 128
S5_W = D_MODEL // 4
GM_W = D_MODEL - S5_W
GM_HEAD_DIM = 128
GM_HEADS = GM_W // GM_HEAD_DIM
S5_GROUP = 16
S5_GROUPS = S5_W // S5_GROUP
S5_STATE = 64
S5_STATES = S5_GROUPS * S5_STATE
IN_COLS = S5_W + 2 * GM_W
N_EXPERTS = 16
CAPACITY_FACTOR = 2
D_FF = 2816
DEPTH = 1
ALPHA = (2.0 * DEPTH) ** 0.25
LN_EPS = 1e-6

F32 = jnp.float32
BF16 = jnp.bfloat16
HIGHEST = lax.Precision.HIGHEST

LANES = 128
SUBLANES = 8
ROW_CHUNKS = D_MODEL // LANES
VMEM_LIMIT_BYTES = 58 * 1024 * 1024

TOK_TILE = 512
SCAN_T = 256
SCAN_LANES = 256
SCAN_HALF = SUBLANES // 2
ROW_TILE = 256
COMPACT_WIN = CHUNK + SUBLANES
F32_TINY = 2.0 ** -126
BRACKET_GEO_STEPS = 8
BRACKET_LIN_STEPS = 26


def _layer_norm(x):
    mu = jnp.mean(x, axis=-1, keepdims=True)
    xc = x - mu
    var = jnp.mean(xc * xc, axis=-1, keepdims=True)
    return xc * lax.rsqrt(var + LN_EPS)


def _params(semantics, vmem=None):
    return pltpu.CompilerParams(dimension_semantics=semantics, vmem_limit_bytes=vmem)


def _store_row_tiles(ref, first_token, x):
    n = x.shape[0]
    for s in range(ROW_CHUNKS):
        ref[pl.ds(first_token * ROW_CHUNKS + s, n, stride=ROW_CHUNKS), :] = x[:, s * LANES:(s + 1) * LANES]


def _load_row_tiles(ref, first_token, n):
    return jnp.concatenate(
        [ref[pl.ds(first_token * ROW_CHUNKS + s, n, stride=ROW_CHUNKS), :] for s in range(ROW_CHUNKS)], axis=1)


def _row_tile(token):
    return pl.ds(pl.multiple_of(token * ROW_CHUNKS, ROW_CHUNKS), ROW_CHUNKS)


def _adaln_kernel(c_ref, w_ref, b_ref, o_ref):
    c = c_ref[...]
    a = c * jax.nn.sigmoid(c)
    o_ref[...] = jnp.dot(a, w_ref[...], precision=HIGHEST, preferred_element_type=F32) + b_ref[...]


def _adaln(cc, w_ada, b_ada):
    n = w_ada.shape[1] // D_MODEL
    return pl.pallas_call(
        _adaln_kernel,
        grid=(n,),
        in_specs=[pl.BlockSpec((SUBLANES, D_MODEL), lambda j: (0, 0)),
                  pl.BlockSpec((D_MODEL, D_MODEL), lambda j: (0, j)),
                  pl.BlockSpec((1, D_MODEL), lambda j: (0, j))],
        out_specs=pl.BlockSpec((SUBLANES, D_MODEL), lambda j: (0, j)),
        out_shape=jax.ShapeDtypeStruct((SUBLANES, w_ada.shape[1]), F32),
        compiler_params=_params(("arbitrary",)),
        name="adaln",
    )(cc, w_ada, b_ada)


def _inproj_kernel(x_ref, mod_ref, w_ref, ws_ref, bs_ref, zs_ref, gm_ref):
    h = _layer_norm(x_ref[...]) * (1.0 + mod_ref[1:2, :]) + mod_ref[0:1, :]
    z = jnp.dot(h.astype(BF16), w_ref[...], preferred_element_type=F32)
    zs_ref[...] = z[:, :S5_W]
    for g in range(GM_HEADS):
        lo = S5_W + g * GM_HEAD_DIM
        u = jax.nn.gelu(z[:, lo:lo + GM_HEAD_DIM])
        v = jax.nn.gelu(z[:, lo + GM_W:lo + GM_W + GM_HEAD_DIM])
        vn = _layer_norm(v).astype(BF16)
        wsg = ws_ref[g]
        bsg = bs_ref[g]
        for c in range(TOK_TILE // CHUNK):
            rows = slice(c * CHUNK, (c + 1) * CHUNK)
            mixed = jnp.dot(wsg, vn[rows, :], preferred_element_type=F32) + bsg
            gm_ref[rows, g * GM_HEAD_DIM:(g + 1) * GM_HEAD_DIM] = (u[rows, :] * mixed).astype(BF16)


def _inproj(x, mod, w_in, gm_ws, gm_bs):
    bn, length, _ = x.shape
    return pl.pallas_call(
        _inproj_kernel,
        grid=(bn, length // TOK_TILE),
        in_specs=[pl.BlockSpec((None, TOK_TILE, D_MODEL), lambda b, i: (b, i, 0)),
                  pl.BlockSpec((None, 6, D_MODEL), lambda b, i: (b, 0, 0)),
                  pl.BlockSpec((D_MODEL, IN_COLS), lambda b, i: (0, 0)),
                  pl.BlockSpec((GM_HEADS, CHUNK, CHUNK), lambda b, i: (0, 0, 0)),
                  pl.BlockSpec((GM_HEADS, CHUNK, 1), lambda b, i: (0, 0, 0))],
        out_specs=[pl.BlockSpec((TOK_TILE, S5_W), lambda b, i: (i, b)),
                   pl.BlockSpec((None, TOK_TILE, GM_W), lambda b, i: (b, i, 0))],
        out_shape=[jax.ShapeDtypeStruct((length, bn * S5_W), F32),
                   jax.ShapeDtypeStruct((bn, length, GM_W), BF16)],
        compiler_params=_params(("arbitrary", "arbitrary"), VMEM_LIMIT_BYTES),
        name="inproj",
    )(x, mod, w_in, gm_ws, gm_bs)


def _ctxproj_kernel(x_ref, mod_ref, w_ref, o_ref):
    h = _layer_norm(x_ref[...]) * (1.0 + mod_ref[1:2, :]) + mod_ref[0:1, :]
    o_ref[...] = jnp.dot(h.astype(BF16), w_ref[...], preferred_element_type=F32)


def _ctxproj(ctx, mod_ctx, w_in):
    bn, clen, _ = ctx.shape
    return pl.pallas_call(
        _ctxproj_kernel,
        grid=(bn,),
        in_specs=[pl.BlockSpec((None, clen, D_MODEL), lambda b: (b, 0, 0)),
                  pl.BlockSpec((None, 6, D_MODEL), lambda b: (0, 0, 0)),
                  pl.BlockSpec((D_MODEL, S5_W), lambda b: (0, 0))],
        out_specs=pl.BlockSpec((clen, S5_W), lambda b: (0, b)),
        out_shape=jax.ShapeDtypeStruct((clen, bn * S5_W), F32),
        compiler_params=_params(("arbitrary",)),
        name="ctxproj",
    )(ctx, mod_ctx, w_in)


def _s5_tables(a_re, a_im, log_step, b_re, b_im, c_re, c_im):
    step = jnp.exp(log_step)[..., None]
    mag = jnp.exp(a_re * step)
    lb_re = mag * jnp.cos(a_im * step)
    lb_im = mag * jnp.sin(a_im * step)
    den = a_re * a_re + a_im * a_im
    q_re = ((lb_re - 1.0) * a_re + lb_im * a_im) / den
    q_im = (lb_im * a_re - (lb_re - 1.0) * a_im) / den
    bb_re = q_re[..., None] * b_re - q_im[..., None] * b_im
    bb_im = q_re[..., None] * b_im + q_im[..., None] * b_re
    eye = jnp.eye(S5_GROUPS, dtype=F32)

    def in_block(m):
        return jnp.einsum('dgph,gk->dghkp', m, eye).reshape(2, S5_W, S5_STATES)

    def out_block(m):
        return jnp.einsum('dghp,gk->dgpkh', m, eye).reshape(2, S5_STATES, S5_W)

    bmat = jnp.concatenate([in_block(bb_re), in_block(bb_im)], axis=2).astype(BF16)
    cmat = jnp.concatenate([out_block(c_re), out_block(-c_im)], axis=1).astype(BF16)
    l1 = jnp.stack([lb_re.reshape(2, S5_STATES), lb_im.reshape(2, S5_STATES)], axis=1)
    l2 = jnp.stack([l1[:, 0] * l1[:, 0] - l1[:, 1] * l1[:, 1], 2.0 * l1[:, 0] * l1[:, 1]], axis=1)
    zero = jnp.zeros_like(l1)

    def halves(top, bottom):
        t = jnp.concatenate([top[:, 0], top[:, 1]], axis=-1)[:, None, :]
        b = jnp.concatenate([bottom[:, 0], bottom[:, 1]], axis=-1)[:, None, :]
        return jnp.concatenate([jnp.repeat(t, SCAN_HALF, axis=1), jnp.repeat(b, SCAN_HALF, axis=1)], axis=1)

    first = jnp.stack([halves(zero, l1)[0], halves(l1, zero)[1]])
    carry = jnp.stack([halves(l1, l2)[0], halves(l2, l1)[1]])
    return bmat, cmat, jnp.stack([first, carry], axis=1)


def _scan_tiles(df_ref, db_ref, s_ref, tab_ref):
    n_tiles = df_ref.shape[1] // SUBLANES
    slabs_per_block = SCAN_LANES // LANES
    top = lax.broadcasted_iota(jnp.int32, (SUBLANES, SCAN_LANES), 0) < SCAN_HALF

    def cmul_add(a_re, a_im, x_re, x_im, y_re, y_im):
        return y_re + a_re * x_re - a_im * x_im, y_im + a_re * x_im + a_im * x_re

    for lb in range(S5_STATES // SCAN_LANES):
        re_cols = pl.ds(lb * SCAN_LANES, SCAN_LANES)
        im_cols = pl.ds(S5_STATES + lb * SCAN_LANES, SCAN_LANES)
        re_slabs = [lb * slabs_per_block + k for k in range(slabs_per_block)]
        im_slabs = [S5_STATES // LANES + s for s in re_slabs]
        tabs = [[(tab_ref[d, k, :, re_cols], tab_ref[d, k, :, im_cols]) for k in range(2)] for d in range(2)]

        def step(d_ref, rows, s, tab, keep_top, re_slabs=re_slabs, im_slabs=im_slabs):
            (f_re, f_im), (c_re, c_im) = tab
            x_re = jnp.concatenate([d_ref[k, rows, :] for k in re_slabs], axis=1)
            x_im = jnp.concatenate([d_ref[k, rows, :] for k in im_slabs], axis=1)
            v_re, v_im = cmul_add(f_re, f_im, pltpu.roll(x_re, SCAN_HALF, 0), pltpu.roll(x_im, SCAN_HALF, 0),
                                  x_re, x_im)
            y_re, y_im = cmul_add(c_re, c_im, s[0], s[1], v_re, v_im)
            for n, (k_re, k_im) in enumerate(zip(re_slabs, im_slabs)):
                d_ref[k_re, rows, :] = y_re[:, n * LANES:(n + 1) * LANES]
                d_ref[k_im, rows, :] = y_im[:, n * LANES:(n + 1) * LANES]
            sel = top if keep_top else ~top
            return (jnp.where(sel, y_re, pltpu.roll(y_re, SCAN_HALF, 0)),
                    jnp.where(sel, y_im, pltpu.roll(y_im, SCAN_HALF, 0)))

        def body(i, carry, tabs=tabs, step=step):
            sf, sb = carry
            rows_f = pl.ds(pl.multiple_of(i * SUBLANES, SUBLANES), SUBLANES)
            rows_b = pl.ds(pl.multiple_of((n_tiles - 1 - i) * SUBLANES, SUBLANES), SUBLANES)
            return step(df_ref, rows_f, sf, tabs[0], False), step(db_ref, rows_b, sb, tabs[1], True)

        init = ((s_ref[0, :, re_cols], s_ref[0, :, im_cols]), (s_ref[1, :, re_cols], s_ref[1, :, im_cols]))
        sf, sb = lax.fori_loop(0, n_tiles, body, init)
        s_ref[0, :, re_cols], s_ref[0, :, im_cols] = sf
        s_ref[1, :, re_cols], s_ref[1, :, im_cols] = sb


def _s5scan_kernel(uf_ref, ub_ref, uc_ref, bmat_ref, cmat_ref, tab_ref, yf_ref, yb_ref, df_ref, db_ref, s_ref):
    j = pl.program_id(0)
    n_slabs = 2 * S5_STATES // LANES

    def batch_rows(b):
        return pl.ds(b, SCAN_T, stride=SCAN_HALF)

    def drive(u_ref, d_ref, direction):
        u = jnp.concatenate([u_ref[:, b * S5_W:(b + 1) * S5_W] for b in range(SCAN_HALF)], axis=0)
        d = jnp.dot(u.astype(BF16), bmat_ref[direction], preferred_element_type=F32)
        for b in range(SCAN_HALF):
            for k in range(n_slabs):
                d_ref[k, batch_rows(b), :] = d[b * SCAN_T:(b + 1) * SCAN_T, k * LANES:(k + 1) * LANES]

    def readout(d_ref, y_ref, direction):
        s = jnp.concatenate(
            [jnp.concatenate([d_ref[k, batch_rows(b), :] for k in range(n_slabs)], axis=1)
             for b in range(SCAN_HALF)], axis=0)
        y = jnp.dot(s.astype(BF16), cmat_ref[direction], preferred_element_type=F32)
        for b in range(SCAN_HALF):
            y_ref[:, b * S5_W:(b + 1) * S5_W] = y[b * SCAN_T:(b + 1) * SCAN_T, :]

    @pl.when(j == 0)
    def _context():
        s_ref[...] = jnp.zeros_like(s_ref)
        drive(uc_ref, df_ref, 0)
        drive(uc_ref, db_ref, 1)
        _scan_tiles(df_ref, db_ref, s_ref, tab_ref)

    drive(uf_ref, df_ref, 0)
    drive(ub_ref, db_ref, 1)
    _scan_tiles(df_ref, db_ref, s_ref, tab_ref)
    readout(df_ref, yf_ref, 0)
    readout(db_ref, yb_ref, 1)


def _s5scan(u_lat, u_ctx, bmat, cmat, tab):
    length, cols = u_lat.shape
    n = length // SCAN_T
    assert u_ctx.shape == (SCAN_T, cols) and cols == SCAN_HALF * S5_W
    slab = pltpu.VMEM((2 * S5_STATES // LANES, SCAN_T * SCAN_HALF, LANES), F32)
    return pl.pallas_call(
        _s5scan_kernel,
        grid=(n,),
        in_specs=[pl.BlockSpec((SCAN_T, cols), lambda j: (j, 0)),
                  pl.BlockSpec((SCAN_T, cols), lambda j: (n - 1 - j, 0)),
                  pl.BlockSpec((SCAN_T, cols), lambda j: (0, 0)),
                  pl.BlockSpec((2, S5_W, 2 * S5_STATES), lambda j: (0, 0, 0)),
                  pl.BlockSpec((2, 2 * S5_STATES, S5_W), lambda j: (0, 0, 0)),
                  pl.BlockSpec((2, 2, SUBLANES, 2 * S5_STATES), lambda j: (0, 0, 0, 0))],
        out_specs=[pl.BlockSpec((SCAN_T, cols), lambda j: (j, 0)),
                   pl.BlockSpec((SCAN_T, cols), lambda j: (n - 1 - j, 0))],
        out_shape=[jax.ShapeDtypeStruct((length, cols), F32)] * 2,
        scratch_shapes=[slab, slab, pltpu.VMEM((2, SUBLANES, 2 * S5_STATES), F32)],
        compiler_params=_params(("arbitrary",), VMEM_LIMIT_BYTES),
        name="s5scan",
    )(u_lat, u_lat, u_ctx, bmat, cmat, tab)


def _outproj_kernel(x_ref, zs_ref, yf_ref, yb_ref, gm_ref, mod_ref, d_ref, wglu_ref, bglu_ref, wout_ref,
                    ln_ref, wr_ref, x1_ref, hp_ref, aff_ref):
    y = d_ref[...] * zs_ref[...] + yf_ref[...] + yb_ref[...]
    g = jax.nn.gelu(y)
    s5 = g * jax.nn.sigmoid(jnp.dot(g.astype(BF16), wglu_ref[...], preferred_element_type=F32) + bglu_ref[...])
    mix = (jnp.dot(s5.astype(BF16), wout_ref[0:S5_W, :], preferred_element_type=F32)
           + jnp.dot(gm_ref[...], wout_ref[S5_W:, :], preferred_element_type=F32))
    x1 = _layer_norm(ALPHA * x_ref[...] + mod_ref[2:3, :] * mix) * ln_ref[0:1, :] + ln_ref[1:2, :]
    x1_ref[...] = x1
    h = _layer_norm(x1) * (1.0 + mod_ref[4:5, :]) + mod_ref[3:4, :]
    logits = lax.dot_general(wr_ref[...], h, (((1,), (1,)), ((), ())), precision=HIGHEST,
                             preferred_element_type=F32)
    ex = jnp.exp(logits - jnp.max(logits, axis=0, keepdims=True))
    aff_ref[...] = ex / jnp.sum(ex, axis=0, keepdims=True)
    _store_row_tiles(hp_ref, 0, h)


def _outproj(x, zs, yf, yb, gm, mod, d, w_glu, b_glu, w_out, ln1, w_rt):
    bn, length, _ = x.shape
    tok = lambda b, i: (b, i, 0)
    s5 = lambda b, i: (i, b)
    const2 = lambda b, i: (0, 0)
    return pl.pallas_call(
        _outproj_kernel,
        grid=(bn, length // TOK_TILE),
        in_specs=[pl.BlockSpec((None, TOK_TILE, D_MODEL), tok),
                  pl.BlockSpec((TOK_TILE, S5_W), s5),
                  pl.BlockSpec((TOK_TILE, S5_W), s5),
                  pl.BlockSpec((TOK_TILE, S5_W), s5),
                  pl.BlockSpec((None, TOK_TILE, GM_W), tok),
                  pl.BlockSpec((None, 6, D_MODEL), lambda b, i: (b, 0, 0)),
                  pl.BlockSpec((1, S5_W), const2),
                  pl.BlockSpec((S5_W, S5_W), const2),
                  pl.BlockSpec((1, S5_W), const2),
                  pl.BlockSpec((D_MODEL, D_MODEL), const2),
                  pl.BlockSpec((2, D_MODEL), const2),
                  pl.BlockSpec((N_EXPERTS, D_MODEL), const2)],
        out_specs=[pl.BlockSpec((None, TOK_TILE, D_MODEL), tok),
                   pl.BlockSpec((None, TOK_TILE * ROW_CHUNKS, LANES), tok),
                   pl.BlockSpec((None, N_EXPERTS, TOK_TILE), lambda b, i: (b, 0, i))],
        out_shape=[jax.ShapeDtypeStruct((bn, length, D_MODEL), F32),
                   jax.ShapeDtypeStruct((bn, length * ROW_CHUNKS, LANES), F32),
                   jax.ShapeDtypeStruct((bn, N_EXPERTS, length), F32)],
        compiler_params=_params(("arbitrary", "arbitrary"), VMEM_LIMIT_BYTES),
        name="outproj",
    )(x, zs, yf, yb, gm, mod, d, w_glu, b_glu, w_out, ln1, w_rt)


def _route_kernel(aff_ref, key_ref, offs_ref, *, cap):
    aff = aff_ref[...]
    ne, nr, _ = aff.shape

    def count(mask):
        return jnp.sum(jnp.sum(mask.astype(F32), axis=2, keepdims=True), axis=1, keepdims=True)

    normal = count(aff >= F32_TINY) >= cap
    lo = jnp.where(normal, F32_TINY, 0.0)
    hi = jnp.where(normal, 2.0, F32_TINY)
    for it in range(BRACKET_GEO_STEPS + BRACKET_LIN_STEPS):
        mid = 0.5 * (lo + hi)
        if it < BRACKET_GEO_STEPS:
            mid = jnp.where(normal, jnp.sqrt(lo * hi), mid)
        ok = count(aff >= mid) >= cap
        lo = jnp.where(ok, mid, lo)
        hi = jnp.where(ok, hi, mid)
    thr = jnp.min(jnp.min(jnp.where(aff >= lo, aff, 2.0), axis=2, keepdims=True), axis=1, keepdims=True)
    above = aff > thr
    tied = aff == thr
    need = cap - count(above)

    kk = lax.broadcasted_iota(jnp.int32, (LANES, LANES), 0)
    ll = lax.broadcasted_iota(jnp.int32, (LANES, LANES), 1)
    upper = (kk <= ll).astype(BF16)
    ones = jnp.ones((LANES, LANES), BF16)
    ri = lax.broadcasted_iota(jnp.int32, (nr, nr), 0)
    rj = lax.broadcasted_iota(jnp.int32, (nr, nr), 1)
    lower = (rj < ri).astype(BF16)

    def exclusive_prefix(mask):
        m = mask.astype(BF16).reshape(ne * nr, LANES)
        in_row = jnp.dot(m, upper, preferred_element_type=F32).reshape(ne, nr, LANES)
        row_tot = jnp.dot(m, ones, preferred_element_type=F32).reshape(ne, nr, LANES)
        row_off = jnp.stack([jnp.dot(lower, row_tot[e].astype(BF16), preferred_element_type=F32)
                             for e in range(ne)])
        return in_row - mask.astype(F32) + row_off, row_off

    tie_rank, _ = exclusive_prefix(tied)
    sel = above | (tied & (tie_rank < need))
    pos, row_off = exclusive_prefix(sel)
    key_ref[...] = jnp.where(sel, pos, -1.0)
    offs_ref[...] = row_off


def _route(aff4, cap):
    bn, ne, nr, _ = aff4.shape
    spec = pl.BlockSpec((None, ne, nr, LANES), lambda b: (b, 0, 0, 0))
    return pl.pallas_call(
        functools.partial(_route_kernel, cap=cap),
        grid=(bn,),
        in_specs=[spec],
        out_specs=[spec, spec],
        out_shape=[jax.ShapeDtypeStruct(aff4.shape, F32)] * 2,
        compiler_params=_params(("arbitrary",), VMEM_LIMIT_BYTES),
        name="route",
    )(aff4)


def _compact_kernel(offs_sref, key_ref, aff_ref, idx_ref, gate_ref, acc_i, acc_g, *, cap):
    be = pl.program_id(0) * pl.num_programs(1) + pl.program_id(1)
    nr = key_ref.shape[0]
    acc_i[...] = jnp.zeros_like(acc_i)
    acc_g[...] = jnp.zeros_like(acc_g)
    slot = lax.broadcasted_iota(jnp.int32, (COMPACT_WIN, LANES), 0)
    lane = lax.broadcasted_iota(jnp.int32, (1, LANES), 1)

    def body(r, carry):
        off = offs_sref[be * nr + r]
        base = pl.multiple_of((off // SUBLANES) * SUBLANES, SUBLANES)
        win = pl.ds(base, COMPACT_WIN)
        hit = (slot + base) == key_ref[r].astype(jnp.int32)
        tok = (lane + r * LANES).astype(F32)
        acc_i[win, :] += jnp.where(hit, tok, 0.0)
        acc_g[win, :] += jnp.where(hit, aff_ref[r], 0.0)
        return carry

    lax.fori_loop(0, nr, body, 0)
    idx_ref[...] = jnp.sum(acc_i[0:cap, :], axis=1, keepdims=True).astype(jnp.int32)
    gate_ref[...] = jnp.sum(acc_g[0:cap, :], axis=1, keepdims=True)


def _compact(row_offs, key5, aff5, cap):
    bn, ne, nr, _, _ = key5.shape
    spec_in = pl.BlockSpec((None, None, nr, 1, LANES), lambda b, e, offs: (b, e, 0, 0, 0))
    spec_out = pl.BlockSpec((None, None, cap, 1), lambda b, e, offs: (b, e, 0, 0))
    return pl.pallas_call(
        functools.partial(_compact_kernel, cap=cap),
        grid_spec=pltpu.PrefetchScalarGridSpec(
            num_scalar_prefetch=1,
            grid=(bn, ne),
            in_specs=[spec_in, spec_in],
            out_specs=[spec_out, spec_out],
            scratch_shapes=[pltpu.VMEM((cap + COMPACT_WIN, LANES), F32),
                            pltpu.VMEM((cap + COMPACT_WIN, LANES), F32)]),
        out_shape=[jax.ShapeDtypeStruct((bn, ne, cap, 1), jnp.int32),
                   jax.ShapeDtypeStruct((bn, ne, cap, 1), F32)],
        compiler_params=_params(("arbitrary", "arbitrary")),
        name="compact",
    )(row_offs, key5, aff5)


GATHER_UNROLL = 8


def _gather_kernel(idx_sref, h_ref, o_ref, rows_ref, *, cap):
    be = pl.program_id(0) * pl.num_programs(1) + pl.program_id(1)

    def body(i, carry):
        for k in range(GATHER_UNROLL):
            c = i * GATHER_UNROLL + k
            rows_ref[_row_tile(c), :] = h_ref[_row_tile(idx_sref[be * cap + c]), :]
        return carry

    lax.fori_loop(0, cap // GATHER_UNROLL, body, 0)
    for m in range(cap // ROW_TILE):
        o_ref[m * ROW_TILE:(m + 1) * ROW_TILE, :] = _load_row_tiles(rows_ref, m * ROW_TILE, ROW_TILE).astype(BF16)


def _gather(idx_flat, h, ne, cap):
    bn, rows, _ = h.shape
    return pl.pallas_call(
        functools.partial(_gather_kernel, cap=cap),
        grid_spec=pltpu.PrefetchScalarGridSpec(
            num_scalar_prefetch=1,
            grid=(bn, ne),
            in_specs=[pl.BlockSpec((None, rows, LANES), lambda b, e, idx: (b, 0, 0),
                                   pipeline_mode=pl.Buffered(1))],
            out_specs=pl.BlockSpec((None, None, cap, D_MODEL), lambda b, e, idx: (b, e, 0, 0)),
            scratch_shapes=[pltpu.VMEM((cap * ROW_CHUNKS, LANES), F32)]),
        out_shape=jax.ShapeDtypeStruct((bn, ne, cap, D_MODEL), BF16),
        compiler_params=_params(("arbitrary", "arbitrary"), VMEM_LIMIT_BYTES),
        name="gather",
    )(idx_flat, h)


def _experts_kernel(xg_ref, gate_ref, wg_ref, wu_ref, wd_ref, res_ref, *, cap):
    for m in range(cap // ROW_TILE):
        rows = slice(m * ROW_TILE, (m + 1) * ROW_TILE)
        xr = xg_ref[rows, :]
        hg = jnp.dot(xr, wg_ref[...], preferred_element_type=F32)
        hu = jnp.dot(xr, wu_ref[...], preferred_element_type=F32)
        hid = (hg * jax.nn.sigmoid(hg) * hu).astype(BF16)
        out = jnp.dot(hid, wd_ref[...], preferred_element_type=F32)
        _store_row_tiles(res_ref, rows.start, out * gate_ref[rows, :])


def _experts(xg, gate, wg, wu, wd):
    bn, ne, cap, _ = xg.shape
    return pl.pallas_call(
        functools.partial(_experts_kernel, cap=cap),
        grid=(bn, ne),
        in_specs=[pl.BlockSpec((None, None, cap, D_MODEL), lambda b, e: (b, e, 0, 0)),
                  pl.BlockSpec((None, None, cap, 1), lambda b, e: (b, e, 0, 0)),
                  pl.BlockSpec((None, D_MODEL, D_FF), lambda b, e: (e, 0, 0)),
                  pl.BlockSpec((None, D_MODEL, D_FF), lambda b, e: (e, 0, 0)),
                  pl.BlockSpec((None, D_FF, D_MODEL), lambda b, e: (e, 0, 0))],
        out_specs=pl.BlockSpec((None, None, cap * ROW_CHUNKS, LANES), lambda b, e: (b, e, 0, 0)),
        out_shape=jax.ShapeDtypeStruct((bn, ne, cap * ROW_CHUNKS, LANES), F32),
        compiler_params=_params(("arbitrary", "arbitrary"), VMEM_LIMIT_BYTES),
        name="experts",
    )(xg, gate, wg, wu, wd)


SCATTER_UNROLL = 4


def _combine_kernel(idx_sref, res_ref, x1_ref, mod_ref, ln_ref, o_ref, acc_ref, *, cap, ne):
    b = pl.program_id(0)
    step = pl.program_id(1)

    @pl.when(step == 0)
    def _init():
        acc_ref[...] = jnp.zeros_like(acc_ref)

    @pl.when(step < ne)
    def _scatter():
        base = (b * ne + step) * cap

        def body(i, carry):
            first = i * SCATTER_UNROLL
            toks = [_row_tile(idx_sref[base + first + k]) for k in range(SCATTER_UNROLL)]
            sums = [acc_ref[toks[k], :] + res_ref[_row_tile(first + k), :] for k in range(SCATTER_UNROLL)]
            for k in range(SCATTER_UNROLL):
                acc_ref[toks[k], :] = sums[k]
            return carry

        lax.fori_loop(0, cap // SCATTER_UNROLL, body, 0)

    @pl.when(step >= ne)
    def _finish():
        moe = _load_row_tiles(acc_ref, (step - ne) * TOK_TILE, TOK_TILE)
        y = ALPHA * x1_ref[...] + mod_ref[5:6, :] * moe
        o_ref[...] = _layer_norm(y) * ln_ref[0:1, :] + ln_ref[1:2, :]


def _combine(idx_flat, res, x1, mod, ln2, cap):
    bn, length, _ = x1.shape
    ne = res.shape[1]
    assert cap % SCATTER_UNROLL == 0
    tok = lambda b, s, idx: (b, jnp.maximum(s - ne, 0), 0)
    return pl.pallas_call(
        functools.partial(_combine_kernel, cap=cap, ne=ne),
        grid_spec=pltpu.PrefetchScalarGridSpec(
            num_scalar_prefetch=1,
            grid=(bn, ne + length // TOK_TILE),
            in_specs=[pl.BlockSpec((None, None, cap * ROW_CHUNKS, LANES),
                                   lambda b, s, idx: (b, jnp.minimum(s, ne - 1), 0, 0)),
                      pl.BlockSpec((None, TOK_TILE, D_MODEL), tok),
                      pl.BlockSpec((None, 6, D_MODEL), lambda b, s, idx: (b, 0, 0)),
                      pl.BlockSpec((2, D_MODEL), lambda b, s, idx: (0, 0))],
            out_specs=pl.BlockSpec((None, TOK_TILE, D_MODEL), tok),
            scratch_shapes=[pltpu.VMEM((length * ROW_CHUNKS, LANES), F32)]),
        out_shape=jax.ShapeDtypeStruct((bn, length, D_MODEL), F32),
        compiler_params=_params(("arbitrary", "arbitrary"), VMEM_LIMIT_BYTES),
        name="combine",
    )(idx_flat, res, x1, mod, ln2)


def kernel(x, c, ctx, c_ctx, w_ada, b_ada, w_in, gm_ws, gm_bs, s5_a_re, s5_a_im, s5_log_step, s5_b_re, s5_b_im,
           s5_c_re, s5_c_im, s5_d, s5_w_glu, s5_b_glu, w_out, ln1_g, ln1_b, w_router, moe_w_gate, moe_w_up,
           moe_w_down, ln2_g, ln2_b):
    assert w_ada.shape[0] == DEPTH == 1
    bn, length, _ = x.shape
    assert bn == SCAN_HALF and ctx.shape[1] == SCAN_T and length % TOK_TILE == 0
    cap = CAPACITY_FACTOR * length // N_EXPERTS

    cc = jnp.zeros((SUBLANES, D_MODEL), F32).at[:bn].set(c).at[bn].set(c_ctx)
    mod = _adaln(cc, w_ada[0], b_ada[0][None, :])
    mod_lat = mod[:bn].reshape(bn, 6, D_MODEL)
    mod_ctx = mod[bn:bn + 1].reshape(1, 6, D_MODEL)

    w_in_b = w_in[0].astype(BF16)
    zs, gm = _inproj(x, mod_lat, w_in_b, gm_ws[0].astype(BF16), gm_bs[0][:, :, None])
    zc = _ctxproj(ctx, mod_ctx, w_in_b)

    bmat, cmat, tab = _s5_tables(s5_a_re[0], s5_a_im[0], s5_log_step[0], s5_b_re[0], s5_b_im[0],
                                 s5_c_re[0], s5_c_im[0])
    yf, yb = _s5scan(zs, zc, bmat, cmat, tab)

    x1, hp, aff = _outproj(x, zs, yf, yb, gm, mod_lat, s5_d[0].reshape(1, S5_W), s5_w_glu[0].astype(BF16),
                           s5_b_glu[0][None, :], w_out[0].astype(BF16), jnp.stack([ln1_g[0], ln1_b[0]]),
                           w_router[0].T)

    aff4 = aff.reshape(bn, N_EXPERTS, length // LANES, LANES)
    key4, offs4 = _route(aff4, cap)
    row_offs = offs4[..., 0].astype(jnp.int32)
    rows5 = (bn, N_EXPERTS, length // LANES, 1, LANES)
    idx, gate = _compact(row_offs.reshape(-1), key4.reshape(rows5), aff4.reshape(rows5), cap)
    idx_flat = idx.reshape(-1)

    xg = _gather(idx_flat, hp, N_EXPERTS, cap)
    res = _experts(xg, gate, moe_w_gate[0].astype(BF16), moe_w_up[0].astype(BF16), moe_w_down[0].astype(BF16))

    return _combine(idx_flat, res, x1, mod_lat, jnp.stack([ln2_g[0], ln2_b[0]]), cap)
```

```python
import functools

import jax
import jax.numpy as jnp
from jax import lax
from jax.experimental import pallas as pl
from jax.experimental.pallas import tpu as pltpu

D_MODEL = 1024
CHUNK = 128
S5_W = D_MODEL // 4
GM_W = D_MODEL - S5_W
GM_HEAD_DIM = 128
GM_HEADS = GM_W // GM_HEAD_DIM
S5_GROUP = 16
S5_GROUPS = S5_W // S5_GROUP
S5_STATE = 64
S5_STATES = S5_GROUPS * S5_STATE
IN_COLS = S5_W + 2 * GM_W
N_EXPERTS = 16
CAPACITY_FACTOR = 2
D_FF = 2816
DEPTH = 1
ALPHA = (2.0 * DEPTH) ** 0.25
LN_EPS = 1e-6

F32 = jnp.float32
BF16 = jnp.bfloat16
HIGHEST = lax.Precision.HIGHEST

LANES = 128
SUBLANES = 8
ROW_CHUNKS = D_MODEL // LANES
VMEM_LIMIT_BYTES = 58 * 1024 * 1024

TOK_TILE = 512
SCAN_T = 256
SCAN_LANES = 256
SCAN_HALF = SUBLANES // 2
ROW_TILE = 256
COMPACT_WIN = CHUNK + SUBLANES
F32_TINY = 2.0 ** -126
BRACKET_GEO_STEPS = 8
BRACKET_LIN_STEPS = 26


def _layer_norm(x):
    mu = jnp.mean(x, axis=-1, keepdims=True)
    xc = x - mu
    var = jnp.mean(xc * xc, axis=-1, keepdims=True)
    return xc * lax.rsqrt(var + LN_EPS)


def _params(semantics, vmem=None):
    return pltpu.CompilerParams(dimension_semantics=semantics, vmem_limit_bytes=vmem)


def _store_row_tiles(ref, first_token, x):
    n = x.shape[0]
    for s in range(ROW_CHUNKS):
        ref[pl.ds(first_token * ROW_CHUNKS + s, n, stride=ROW_CHUNKS), :] = x[:, s * LANES:(s + 1) * LANES]


def _load_row_tiles(ref, first_token, n):
    return jnp.concatenate(
        [ref[pl.ds(first_token * ROW_CHUNKS + s, n, stride=ROW_CHUNKS), :] for s in range(ROW_CHUNKS)], axis=1)


def _row_tile(token):
    return pl.ds(pl.multiple_of(token * ROW_CHUNKS, ROW_CHUNKS), ROW_CHUNKS)


def _cast_spec(w, n_steps, step_of):
    ne, rows, cols = w.shape
    per_expert = n_steps // ne
    assert per_expert * ne == n_steps and rows % (per_expert * 2 * SUBLANES) == 0
    return pl.BlockSpec((None, rows // per_expert, cols),
                        lambda *ids: (step_of(*ids) // per_expert, step_of(*ids) % per_expert, 0))


def _cast_block(w_ref, o_ref):
    o_ref[...] = w_ref[...].astype(BF16)


def _adaln_kernel(c_ref, w_ref, b_ref, o_ref):
    c = c_ref[...]
    a = c * jax.nn.sigmoid(c)
    o_ref[...] = jnp.dot(a, w_ref[...], precision=HIGHEST, preferred_element_type=F32) + b_ref[...]


def _adaln(cc, w_ada, b_ada):
    n = w_ada.shape[1] // D_MODEL
    return pl.pallas_call(
        _adaln_kernel,
        grid=(n,),
        in_specs=[pl.BlockSpec((SUBLANES, D_MODEL), lambda j: (0, 0)),
                  pl.BlockSpec((D_MODEL, D_MODEL), lambda j: (0, j)),
                  pl.BlockSpec((1, D_MODEL), lambda j: (0, j))],
        out_specs=pl.BlockSpec((SUBLANES, D_MODEL), lambda j: (0, j)),
        out_shape=jax.ShapeDtypeStruct((SUBLANES, w_ada.shape[1]), F32),
        compiler_params=_params(("arbitrary",)),
        name="adaln",
    )(cc, w_ada, b_ada)


def _inproj_kernel(x_ref, mod_ref, w_ref, ws_ref, bs_ref, wcast_ref, zs_ref, gm_ref, wcast_out_ref):
    _cast_block(wcast_ref, wcast_out_ref)
    h = _layer_norm(x_ref[...]) * (1.0 + mod_ref[1:2, :]) + mod_ref[0:1, :]
    z = jnp.dot(h.astype(BF16), w_ref[...], preferred_element_type=F32)
    zs_ref[...] = z[:, :S5_W]
    for g in range(GM_HEADS):
        lo = S5_W + g * GM_HEAD_DIM
        u = jax.nn.gelu(z[:, lo:lo + GM_HEAD_DIM])
        v = jax.nn.gelu(z[:, lo + GM_W:lo + GM_W + GM_HEAD_DIM])
        vn = _layer_norm(v).astype(BF16)
        wsg = ws_ref[g]
        bsg = bs_ref[g]
        for c in range(TOK_TILE // CHUNK):
            rows = slice(c * CHUNK, (c + 1) * CHUNK)
            mixed = jnp.dot(wsg, vn[rows, :], preferred_element_type=F32) + bsg
            gm_ref[rows, g * GM_HEAD_DIM:(g + 1) * GM_HEAD_DIM] = (u[rows, :] * mixed).astype(BF16)


def _inproj(x, mod, w_in, gm_ws, gm_bs, w_cast):
    bn, length, _ = x.shape
    nt = length // TOK_TILE
    cast = _cast_spec(w_cast, bn * nt, lambda b, i: b * nt + i)
    return pl.pallas_call(
        _inproj_kernel,
        grid=(bn, nt),
        in_specs=[pl.BlockSpec((None, TOK_TILE, D_MODEL), lambda b, i: (b, i, 0)),
                  pl.BlockSpec((None, 6, D_MODEL), lambda b, i: (b, 0, 0)),
                  pl.BlockSpec((D_MODEL, IN_COLS), lambda b, i: (0, 0)),
                  pl.BlockSpec((GM_HEADS, CHUNK, CHUNK), lambda b, i: (0, 0, 0)),
                  pl.BlockSpec((GM_HEADS, CHUNK, 1), lambda b, i: (0, 0, 0)),
                  cast],
        out_specs=[pl.BlockSpec((TOK_TILE, S5_W), lambda b, i: (i, b)),
                   pl.BlockSpec((None, TOK_TILE, GM_W), lambda b, i: (b, i, 0)),
                   cast],
        out_shape=[jax.ShapeDtypeStruct((length, bn * S5_W), F32),
                   jax.ShapeDtypeStruct((bn, length, GM_W), BF16),
                   jax.ShapeDtypeStruct(w_cast.shape, BF16)],
        compiler_params=_params(("arbitrary", "arbitrary"), VMEM_LIMIT_BYTES),
        name="inproj",
    )(x, mod, w_in, gm_ws, gm_bs, w_cast)


def _ctxproj_kernel(x_ref, mod_ref, w_ref, o_ref):
    h = _layer_norm(x_ref[...]) * (1.0 + mod_ref[1:2, :]) + mod_ref[0:1, :]
    o_ref[...] = jnp.dot(h.astype(BF16), w_ref[...], preferred_element_type=F32)


def _ctxproj(ctx, mod_ctx, w_in):
    bn, clen, _ = ctx.shape
    return pl.pallas_call(
        _ctxproj_kernel,
        grid=(bn,),
        in_specs=[pl.BlockSpec((None, clen, D_MODEL), lambda b: (b, 0, 0)),
                  pl.BlockSpec((None, 6, D_MODEL), lambda b: (0, 0, 0)),
                  pl.BlockSpec((D_MODEL, S5_W), lambda b: (0, 0))],
        out_specs=pl.BlockSpec((clen, S5_W), lambda b: (0, b)),
        out_shape=jax.ShapeDtypeStruct((clen, bn * S5_W), F32),
        compiler_params=_params(("arbitrary",)),
        name="ctxproj",
    )(ctx, mod_ctx, w_in)


def _s5_tables(a_re, a_im, log_step, b_re, b_im, c_re, c_im):
    step = jnp.exp(log_step)[..., None]
    mag = jnp.exp(a_re * step)
    lb_re = mag * jnp.cos(a_im * step)
    lb_im = mag * jnp.sin(a_im * step)
    den = a_re * a_re + a_im * a_im
    q_re = ((lb_re - 1.0) * a_re + lb_im * a_im) / den
    q_im = (lb_im * a_re - (lb_re - 1.0) * a_im) / den
    bb_re = q_re[..., None] * b_re - q_im[..., None] * b_im
    bb_im = q_re[..., None] * b_im + q_im[..., None] * b_re
    eye = jnp.eye(S5_GROUPS, dtype=F32)

    def in_block(m):
        return jnp.einsum('dgph,gk->dghkp', m, eye).reshape(2, S5_W, S5_STATES)

    def out_block(m):
        return jnp.einsum('dghp,gk->dgpkh', m, eye).reshape(2, S5_STATES, S5_W)

    bmat = jnp.concatenate([in_block(bb_re), in_block(bb_im)], axis=2).astype(BF16)
    cmat = jnp.concatenate([out_block(c_re), out_block(-c_im)], axis=1).astype(BF16)
    l1 = jnp.stack([lb_re.reshape(2, S5_STATES), lb_im.reshape(2, S5_STATES)], axis=1)
    l2 = jnp.stack([l1[:, 0] * l1[:, 0] - l1[:, 1] * l1[:, 1], 2.0 * l1[:, 0] * l1[:, 1]], axis=1)
    zero = jnp.zeros_like(l1)

    def halves(top, bottom):
        t = jnp.concatenate([top[:, 0], top[:, 1]], axis=-1)[:, None, :]
        b = jnp.concatenate([bottom[:, 0], bottom[:, 1]], axis=-1)[:, None, :]
        return jnp.concatenate([jnp.repeat(t, SCAN_HALF, axis=1), jnp.repeat(b, SCAN_HALF, axis=1)], axis=1)

    first = jnp.stack([halves(zero, l1)[0], halves(l1, zero)[1]])
    carry = jnp.stack([halves(l1, l2)[0], halves(l2, l1)[1]])
    return bmat, cmat, jnp.stack([first, carry], axis=1)


def _scan_tiles(df_ref, db_ref, s_ref, tab_ref):
    n_tiles = df_ref.shape[1] // SUBLANES
    slabs_per_block = SCAN_LANES // LANES
    top = lax.broadcasted_iota(jnp.int32, (SUBLANES, SCAN_LANES), 0) < SCAN_HALF

    def cmul_add(a_re, a_im, x_re, x_im, y_re, y_im):
        return y_re + a_re * x_re - a_im * x_im, y_im + a_re * x_im + a_im * x_re

    for lb in range(S5_STATES // SCAN_LANES):
        re_cols = pl.ds(lb * SCAN_LANES, SCAN_LANES)
        im_cols = pl.ds(S5_STATES + lb * SCAN_LANES, SCAN_LANES)
        re_slabs = [lb * slabs_per_block + k for k in range(slabs_per_block)]
        im_slabs = [S5_STATES // LANES + s for s in re_slabs]
        tabs = [[(tab_ref[d, k, :, re_cols], tab_ref[d, k, :, im_cols]) for k in range(2)] for d in range(2)]

        def step(d_ref, rows, s, tab, keep_top, re_slabs=re_slabs, im_slabs=im_slabs):
            (f_re, f_im), (c_re, c_im) = tab
            x_re = jnp.concatenate([d_ref[k, rows, :] for k in re_slabs], axis=1)
            x_im = jnp.concatenate([d_ref[k, rows, :] for k in im_slabs], axis=1)
            v_re, v_im = cmul_add(f_re, f_im, pltpu.roll(x_re, SCAN_HALF, 0), pltpu.roll(x_im, SCAN_HALF, 0),
                                  x_re, x_im)
            y_re, y_im = cmul_add(c_re, c_im, s[0], s[1], v_re, v_im)
            for n, (k_re, k_im) in enumerate(zip(re_slabs, im_slabs)):
                d_ref[k_re, rows, :] = y_re[:, n * LANES:(n + 1) * LANES]
                d_ref[k_im, rows, :] = y_im[:, n * LANES:(n + 1) * LANES]
            sel = top if keep_top else ~top
            return (jnp.where(sel, y_re, pltpu.roll(y_re, SCAN_HALF, 0)),
                    jnp.where(sel, y_im, pltpu.roll(y_im, SCAN_HALF, 0)))

        def body(i, carry, tabs=tabs, step=step):
            sf, sb = carry
            rows_f = pl.ds(pl.multiple_of(i * SUBLANES, SUBLANES), SUBLANES)
            rows_b = pl.ds(pl.multiple_of((n_tiles - 1 - i) * SUBLANES, SUBLANES), SUBLANES)
            return step(df_ref, rows_f, sf, tabs[0], False), step(db_ref, rows_b, sb, tabs[1], True)

        init = ((s_ref[0, :, re_cols], s_ref[0, :, im_cols]), (s_ref[1, :, re_cols], s_ref[1, :, im_cols]))
        sf, sb = lax.fori_loop(0, n_tiles, body, init)
        s_ref[0, :, re_cols], s_ref[0, :, im_cols] = sf
        s_ref[1, :, re_cols], s_ref[1, :, im_cols] = sb


def _s5scan_kernel(uf_ref, ub_ref, uc_ref, bmat_ref, cmat_ref, tab_ref, wcast_ref, yf_ref, yb_ref, wcast_out_ref,
                   df_ref, db_ref, s_ref):
    _cast_block(wcast_ref, wcast_out_ref)
    j = pl.program_id(0)
    n_slabs = 2 * S5_STATES // LANES

    def batch_rows(b):
        return pl.ds(b, SCAN_T, stride=SCAN_HALF)

    def drive(u_ref, d_ref, direction):
        u = jnp.concatenate([u_ref[:, b * S5_W:(b + 1) * S5_W] for b in range(SCAN_HALF)], axis=0)
        d = jnp.dot(u.astype(BF16), bmat_ref[direction], preferred_element_type=F32)
        for b in range(SCAN_HALF):
            for k in range(n_slabs):
                d_ref[k, batch_rows(b), :] = d[b * SCAN_T:(b + 1) * SCAN_T, k * LANES:(k + 1) * LANES]

    def readout(d_ref, y_ref, direction):
        s = jnp.concatenate(
            [jnp.concatenate([d_ref[k, batch_rows(b), :] for k in range(n_slabs)], axis=1)
             for b in range(SCAN_HALF)], axis=0)
        y = jnp.dot(s.astype(BF16), cmat_ref[direction], preferred_element_type=F32)
        for b in range(SCAN_HALF):
            y_ref[:, b * S5_W:(b + 1) * S5_W] = y[b * SCAN_T:(b + 1) * SCAN_T, :]

    @pl.when(j == 0)
    def _context():
        s_ref[...] = jnp.zeros_like(s_ref)
        drive(uc_ref, df_ref, 0)
        drive(uc_ref, db_ref, 1)
        _scan_tiles(df_ref, db_ref, s_ref, tab_ref)

    drive(uf_ref, df_ref, 0)
    drive(ub_ref, db_ref, 1)
    _scan_tiles(df_ref, db_ref, s_ref, tab_ref)
    readout(df_ref, yf_ref, 0)
    readout(db_ref, yb_ref, 1)


def _s5scan(u_lat, u_ctx, bmat, cmat, tab, w_cast):
    length, cols = u_lat.shape
    n = length // SCAN_T
    assert u_ctx.shape == (SCAN_T, cols) and cols == SCAN_HALF * S5_W
    slab = pltpu.VMEM((2 * S5_STATES // LANES, SCAN_T * SCAN_HALF, LANES), F32)
    cast = _cast_spec(w_cast, n, lambda j: j)
    return pl.pallas_call(
        _s5scan_kernel,
        grid=(n,),
        in_specs=[pl.BlockSpec((SCAN_T, cols), lambda j: (j, 0)),
                  pl.BlockSpec((SCAN_T, cols), lambda j: (n - 1 - j, 0)),
                  pl.BlockSpec((SCAN_T, cols), lambda j: (0, 0)),
                  pl.BlockSpec((2, S5_W, 2 * S5_STATES), lambda j: (0, 0, 0)),
                  pl.BlockSpec((2, 2 * S5_STATES, S5_W), lambda j: (0, 0, 0)),
                  pl.BlockSpec((2, 2, SUBLANES, 2 * S5_STATES), lambda j: (0, 0, 0, 0)),
                  cast],
        out_specs=[pl.BlockSpec((SCAN_T, cols), lambda j: (j, 0)),
                   pl.BlockSpec((SCAN_T, cols), lambda j: (n - 1 - j, 0)),
                   cast],
        out_shape=[jax.ShapeDtypeStruct((length, cols), F32)] * 2 + [jax.ShapeDtypeStruct(w_cast.shape, BF16)],
        scratch_shapes=[slab, slab, pltpu.VMEM((2, SUBLANES, 2 * S5_STATES), F32)],
        compiler_params=_params(("arbitrary",), VMEM_LIMIT_BYTES),
        name="s5scan",
    )(u_lat, u_lat, u_ctx, bmat, cmat, tab, w_cast)


def _outproj_kernel(x_ref, zs_ref, yf_ref, yb_ref, gm_ref, mod_ref, d_ref, wglu_ref, bglu_ref, wout_ref,
                    ln_ref, wr_ref, x1_ref, hp_ref, aff_ref):
    y = d_ref[...] * zs_ref[...] + yf_ref[...] + yb_ref[...]
    g = jax.nn.gelu(y)
    s5 = g * jax.nn.sigmoid(jnp.dot(g.astype(BF16), wglu_ref[...], preferred_element_type=F32) + bglu_ref[...])
    mix = (jnp.dot(s5.astype(BF16), wout_ref[0:S5_W, :], preferred_element_type=F32)
           + jnp.dot(gm_ref[...], wout_ref[S5_W:, :], preferred_element_type=F32))
    x1 = _layer_norm(ALPHA * x_ref[...] + mod_ref[2:3, :] * mix) * ln_ref[0:1, :] + ln_ref[1:2, :]
    x1_ref[...] = x1
    h = _layer_norm(x1) * (1.0 + mod_ref[4:5, :]) + mod_ref[3:4, :]
    logits = lax.dot_general(wr_ref[...], h, (((1,), (1,)), ((), ())), precision=HIGHEST,
                             preferred_element_type=F32)
    ex = jnp.exp(logits - jnp.max(logits, axis=0, keepdims=True))
    aff_ref[...] = ex / jnp.sum(ex, axis=0, keepdims=True)
    _store_row_tiles(hp_ref, 0, h)


def _outproj(x, zs, yf, yb, gm, mod, d, w_glu, b_glu, w_out, ln1, w_rt):
    bn, length, _ = x.shape
    tok = lambda b, i: (b, i, 0)
    s5 = lambda b, i: (i, b)
    const2 = lambda b, i: (0, 0)
    return pl.pallas_call(
        _outproj_kernel,
        grid=(bn, length // TOK_TILE),
        in_specs=[pl.BlockSpec((None, TOK_TILE, D_MODEL), tok),
                  pl.BlockSpec((TOK_TILE, S5_W), s5),
                  pl.BlockSpec((TOK_TILE, S5_W), s5),
                  pl.BlockSpec((TOK_TILE, S5_W), s5),
                  pl.BlockSpec((None, TOK_TILE, GM_W), tok),
                  pl.BlockSpec((None, 6, D_MODEL), lambda b, i: (b, 0, 0)),
                  pl.BlockSpec((1, S5_W), const2),
                  pl.BlockSpec((S5_W, S5_W), const2),
                  pl.BlockSpec((1, S5_W), const2),
                  pl.BlockSpec((D_MODEL, D_MODEL), const2),
                  pl.BlockSpec((2, D_MODEL), const2),
                  pl.BlockSpec((N_EXPERTS, D_MODEL), const2)],
        out_specs=[pl.BlockSpec((None, TOK_TILE, D_MODEL), tok),
                   pl.BlockSpec((None, TOK_TILE * ROW_CHUNKS, LANES), tok),
                   pl.BlockSpec((None, N_EXPERTS, TOK_TILE), lambda b, i: (b, 0, i))],
        out_shape=[jax.ShapeDtypeStruct((bn, length, D_MODEL), F32),
                   jax.ShapeDtypeStruct((bn, length * ROW_CHUNKS, LANES), F32),
                   jax.ShapeDtypeStruct((bn, N_EXPERTS, length), F32)],
        compiler_params=_params(("arbitrary", "arbitrary"), VMEM_LIMIT_BYTES),
        name="outproj",
    )(x, zs, yf, yb, gm, mod, d, w_glu, b_glu, w_out, ln1, w_rt)


def _route_kernel(aff_ref, key_ref, offs_ref, *, cap):
    aff = aff_ref[...]
    ne, nr, _ = aff.shape

    def count(mask):
        return jnp.sum(jnp.sum(mask.astype(F32), axis=2, keepdims=True), axis=1, keepdims=True)

    normal = count(aff >= F32_TINY) >= cap
    lo = jnp.where(normal, F32_TINY, 0.0)
    hi = jnp.where(normal, 2.0, F32_TINY)
    for it in range(BRACKET_GEO_STEPS + BRACKET_LIN_STEPS):
        mid = 0.5 * (lo + hi)
        if it < BRACKET_GEO_STEPS:
            mid = jnp.where(normal, jnp.sqrt(lo * hi), mid)
        ok = count(aff >= mid) >= cap
        lo = jnp.where(ok, mid, lo)
        hi = jnp.where(ok, hi, mid)
    thr = jnp.min(jnp.min(jnp.where(aff >= lo, aff, 2.0), axis=2, keepdims=True), axis=1, keepdims=True)
    above = aff > thr
    tied = aff == thr
    need = cap - count(above)

    kk = lax.broadcasted_iota(jnp.int32, (LANES, LANES), 0)
    ll = lax.broadcasted_iota(jnp.int32, (LANES, LANES), 1)
    upper = (kk <= ll).astype(BF16)
    ones = jnp.ones((LANES, LANES), BF16)
    ri = lax.broadcasted_iota(jnp.int32, (nr, nr), 0)
    rj = lax.broadcasted_iota(jnp.int32, (nr, nr), 1)
    lower = (rj < ri).astype(BF16)

    def exclusive_prefix(mask):
        m = mask.astype(BF16).reshape(ne * nr, LANES)
        in_row = jnp.dot(m, upper, preferred_element_type=F32).reshape(ne, nr, LANES)
        row_tot = jnp.dot(m, ones, preferred_element_type=F32).reshape(ne, nr, LANES)
        row_off = jnp.stack([jnp.dot(lower, row_tot[e].astype(BF16), preferred_element_type=F32)
                             for e in range(ne)])
        return in_row - mask.astype(F32) + row_off, row_off

    tie_rank, _ = exclusive_prefix(tied)
    sel = above | (tied & (tie_rank < need))
    pos, row_off = exclusive_prefix(sel)
    key_ref[...] = jnp.where(sel, pos, -1.0)
    offs_ref[...] = row_off


def _route(aff4, cap):
    bn, ne, nr, _ = aff4.shape
    spec = pl.BlockSpec((None, ne, nr, LANES), lambda b: (b, 0, 0, 0))
    return pl.pallas_call(
        functools.partial(_route_kernel, cap=cap),
        grid=(bn,),
        in_specs=[spec],
        out_specs=[spec, spec],
        out_shape=[jax.ShapeDtypeStruct(aff4.shape, F32)] * 2,
        compiler_params=_params(("arbitrary",), VMEM_LIMIT_BYTES),
        name="route",
    )(aff4)


def _compact_kernel(offs_sref, key_ref, aff_ref, wcast_ref, idx_ref, gate_ref, wcast_out_ref, acc_i, acc_g, *, cap):
    _cast_block(wcast_ref, wcast_out_ref)
    be = pl.program_id(0) * pl.num_programs(1) + pl.program_id(1)
    nr = key_ref.shape[0]
    acc_i[...] = jnp.zeros_like(acc_i)
    acc_g[...] = jnp.zeros_like(acc_g)
    slot = lax.broadcasted_iota(jnp.int32, (COMPACT_WIN, LANES), 0)
    lane = lax.broadcasted_iota(jnp.int32, (1, LANES), 1)

    def body(r, carry):
        off = offs_sref[be * nr + r]
        base = pl.multiple_of((off // SUBLANES) * SUBLANES, SUBLANES)
        win = pl.ds(base, COMPACT_WIN)
        hit = (slot + base) == key_ref[r].astype(jnp.int32)
        tok = (lane + r * LANES).astype(F32)
        acc_i[win, :] += jnp.where(hit, tok, 0.0)
        acc_g[win, :] += jnp.where(hit, aff_ref[r], 0.0)
        return carry

    lax.fori_loop(0, nr, body, 0)
    idx_ref[...] = jnp.sum(acc_i[0:cap, :], axis=1, keepdims=True).astype(jnp.int32)
    gate_ref[...] = jnp.sum(acc_g[0:cap, :], axis=1, keepdims=True)


def _compact(row_offs, key5, aff5, cap, w_cast):
    bn, ne, nr, _, _ = key5.shape
    spec_in = pl.BlockSpec((None, None, nr, 1, LANES), lambda b, e, offs: (b, e, 0, 0, 0))
    spec_out = pl.BlockSpec((None, None, cap, 1), lambda b, e, offs: (b, e, 0, 0))
    cast = _cast_spec(w_cast, bn * ne, lambda b, e, offs: b * ne + e)
    return pl.pallas_call(
        functools.partial(_compact_kernel, cap=cap),
        grid_spec=pltpu.PrefetchScalarGridSpec(
            num_scalar_prefetch=1,
            grid=(bn, ne),
            in_specs=[spec_in, spec_in, cast],
            out_specs=[spec_out, spec_out, cast],
            scratch_shapes=[pltpu.VMEM((cap + COMPACT_WIN, LANES), F32),
                            pltpu.VMEM((cap + COMPACT_WIN, LANES), F32)]),
        out_shape=[jax.ShapeDtypeStruct((bn, ne, cap, 1), jnp.int32),
                   jax.ShapeDtypeStruct((bn, ne, cap, 1), F32),
                   jax.ShapeDtypeStruct(w_cast.shape, BF16)],
        compiler_params=_params(("arbitrary", "arbitrary"), VMEM_LIMIT_BYTES),
        name="compact",
    )(row_offs, key5, aff5, w_cast)


GATHER_UNROLL = 8


def _gather_kernel(idx_sref, h_ref, o_ref, rows_ref, *, cap):
    be = pl.program_id(0) * pl.num_programs(1) + pl.program_id(1)

    def body(i, carry):
        for k in range(GATHER_UNROLL):
            c = i * GATHER_UNROLL + k
            rows_ref[_row_tile(c), :] = h_ref[_row_tile(idx_sref[be * cap + c]), :]
        return carry

    lax.fori_loop(0, cap // GATHER_UNROLL, body, 0)
    for m in range(cap // ROW_TILE):
        o_ref[m * ROW_TILE:(m + 1) * ROW_TILE, :] = _load_row_tiles(rows_ref, m * ROW_TILE, ROW_TILE).astype(BF16)


def _gather(idx_flat, h, ne, cap):
    bn, rows, _ = h.shape
    return pl.pallas_call(
        functools.partial(_gather_kernel, cap=cap),
        grid_spec=pltpu.PrefetchScalarGridSpec(
            num_scalar_prefetch=1,
            grid=(bn, ne),
            in_specs=[pl.BlockSpec((None, rows, LANES), lambda b, e, idx: (b, 0, 0),
                                   pipeline_mode=pl.Buffered(1))],
            out_specs=pl.BlockSpec((None, None, cap, D_MODEL), lambda b, e, idx: (b, e, 0, 0)),
            scratch_shapes=[pltpu.VMEM((cap * ROW_CHUNKS, LANES), F32)]),
        out_shape=jax.ShapeDtypeStruct((bn, ne, cap, D_MODEL), BF16),
        compiler_params=_params(("arbitrary", "arbitrary"), VMEM_LIMIT_BYTES),
        name="gather",
    )(idx_flat, h)


def _experts_kernel(xg_ref, gate_ref, wg_ref, wu_ref, wd_ref, res_ref, *, cap):
    for m in range(cap // ROW_TILE):
        rows = slice(m * ROW_TILE, (m + 1) * ROW_TILE)
        xr = xg_ref[rows, :]
        hg = jnp.dot(xr, wg_ref[...], preferred_element_type=F32)
        hu = jnp.dot(xr, wu_ref[...], preferred_element_type=F32)
        hid = (hg * jax.nn.sigmoid(hg) * hu).astype(BF16)
        out = jnp.dot(hid, wd_ref[...], preferred_element_type=F32)
        _store_row_tiles(res_ref, rows.start, out * gate_ref[rows, :])


def _experts(xg, gate, wg, wu, wd):
    bn, ne, cap, _ = xg.shape
    return pl.pallas_call(
        functools.partial(_experts_kernel, cap=cap),
        grid=(bn, ne),
        in_specs=[pl.BlockSpec((None, None, cap, D_MODEL), lambda b, e: (b, e, 0, 0)),
                  pl.BlockSpec((None, None, cap, 1), lambda b, e: (b, e, 0, 0)),
                  pl.BlockSpec((None, D_MODEL, D_FF), lambda b, e: (e, 0, 0)),
                  pl.BlockSpec((None, D_MODEL, D_FF), lambda b, e: (e, 0, 0)),
                  pl.BlockSpec((None, D_FF, D_MODEL), lambda b, e: (e, 0, 0))],
        out_specs=pl.BlockSpec((None, None, cap * ROW_CHUNKS, LANES), lambda b, e: (b, e, 0, 0)),
        out_shape=jax.ShapeDtypeStruct((bn, ne, cap * ROW_CHUNKS, LANES), F32),
        compiler_params=_params(("arbitrary", "arbitrary"), VMEM_LIMIT_BYTES),
        name="experts",
    )(xg, gate, wg, wu, wd)


SCATTER_UNROLL = 4


def _combine_kernel(idx_sref, res_ref, x1_ref, mod_ref, ln_ref, o_ref, acc_ref, *, cap, ne):
    b = pl.program_id(0)
    step = pl.program_id(1)

    @pl.when(step == 0)
    def _init():
        acc_ref[...] = jnp.zeros_like(acc_ref)

    @pl.when(step < ne)
    def _scatter():
        base = (b * ne + step) * cap

        def body(i, carry):
            first = i * SCATTER_UNROLL
            toks = [_row_tile(idx_sref[base + first + k]) for k in range(SCATTER_UNROLL)]
            sums = [acc_ref[toks[k], :] + res_ref[_row_tile(first + k), :] for k in range(SCATTER_UNROLL)]
            for k in range(SCATTER_UNROLL):
                acc_ref[toks[k], :] = sums[k]
            return carry

        lax.fori_loop(0, cap // SCATTER_UNROLL, body, 0)

    @pl.when(step >= ne)
    def _finish():
        moe = _load_row_tiles(acc_ref, (step - ne) * TOK_TILE, TOK_TILE)
        y = ALPHA * x1_ref[...] + mod_ref[5:6, :] * moe
        o_ref[...] = _layer_norm(y) * ln_ref[0:1, :] + ln_ref[1:2, :]


def _combine(idx_flat, res, x1, mod, ln2, cap):
    bn, length, _ = x1.shape
    ne = res.shape[1]
    assert cap % SCATTER_UNROLL == 0
    tok = lambda b, s, idx: (b, jnp.maximum(s - ne, 0), 0)
    return pl.pallas_call(
        functools.partial(_combine_kernel, cap=cap, ne=ne),
        grid_spec=pltpu.PrefetchScalarGridSpec(
            num_scalar_prefetch=1,
            grid=(bn, ne + length // TOK_TILE),
            in_specs=[pl.BlockSpec((None, None, cap * ROW_CHUNKS, LANES),
                                   lambda b, s, idx: (b, jnp.minimum(s, ne - 1), 0, 0)),
                      pl.BlockSpec((None, TOK_TILE, D_MODEL), tok),
                      pl.BlockSpec((None, 6, D_MODEL), lambda b, s, idx: (b, 0, 0)),
                      pl.BlockSpec((2, D_MODEL), lambda b, s, idx: (0, 0))],
            out_specs=pl.BlockSpec((None, TOK_TILE, D_MODEL), tok),
            scratch_shapes=[pltpu.VMEM((length * ROW_CHUNKS, LANES), F32)]),
        out_shape=jax.ShapeDtypeStruct((bn, length, D_MODEL), F32),
        compiler_params=_params(("arbitrary", "arbitrary"), VMEM_LIMIT_BYTES),
        name="combine",
    )(idx_flat, res, x1, mod, ln2)


def kernel(x, c, ctx, c_ctx, w_ada, b_ada, w_in, gm_ws, gm_bs, s5_a_re, s5_a_im, s5_log_step, s5_b_re, s5_b_im,
           s5_c_re, s5_c_im, s5_d, s5_w_glu, s5_b_glu, w_out, ln1_g, ln1_b, w_router, moe_w_gate, moe_w_up,
           moe_w_down, ln2_g, ln2_b):
    assert w_ada.shape[0] == DEPTH == 1
    bn, length, _ = x.shape
    assert bn == SCAN_HALF and ctx.shape[1] == SCAN_T and length % TOK_TILE == 0
    cap = CAPACITY_FACTOR * length // N_EXPERTS

    cc = jnp.zeros((SUBLANES, D_MODEL), F32).at[:bn].set(c).at[bn].set(c_ctx)
    mod = _adaln(cc, w_ada[0], b_ada[0][None, :])
    mod_lat = mod[:bn].reshape(bn, 6, D_MODEL)
    mod_ctx = mod[bn:bn + 1].reshape(1, 6, D_MODEL)

    w_in_b = w_in[0].astype(BF16)
    zs, gm, wg_b = _inproj(x, mod_lat, w_in_b, gm_ws[0].astype(BF16), gm_bs[0][:, :, None], moe_w_gate[0])
    zc = _ctxproj(ctx, mod_ctx, w_in_b)

    bmat, cmat, tab = _s5_tables(s5_a_re[0], s5_a_im[0], s5_log_step[0], s5_b_re[0], s5_b_im[0],
                                 s5_c_re[0], s5_c_im[0])
    yf, yb, wd_b = _s5scan(zs, zc, bmat, cmat, tab, moe_w_down[0])

    x1, hp, aff = _outproj(x, zs, yf, yb, gm, mod_lat, s5_d[0].reshape(1, S5_W), s5_w_glu[0].astype(BF16),
                           s5_b_glu[0][None, :], w_out[0].astype(BF16), jnp.stack([ln1_g[0], ln1_b[0]]),
                           w_router[0].T)

    aff4 = aff.reshape(bn, N_EXPERTS, length // LANES, LANES)
    key4, offs4 = _route(aff4, cap)
    row_offs = offs4[..., 0].astype(jnp.int32)
    rows5 = (bn, N_EXPERTS, length // LANES, 1, LANES)
    idx, gate, wu_b = _compact(row_offs.reshape(-1), key4.reshape(rows5), aff4.reshape(rows5), cap, moe_w_up[0])
    idx_flat = idx.reshape(-1)

    xg = _gather(idx_flat, hp, N_EXPERTS, cap)
    res = _experts(xg, gate, wg_b, wu_b, wd_b)

    return _combine(idx_flat, res, x1, mod_lat, jnp.stack([ln2_g[0], ln2_b[0]]), cap)
```

```python
import functools

import jax
import jax.numpy as jnp
from jax import lax
from jax.experimental import pallas as pl
from jax.experimental.pallas import tpu as pltpu

D_MODEL = 1024
CHUNK = 128
S5_W = D_MODEL // 4
GM_W = D_MODEL - S5_W
GM_HEAD_DIM = 128
GM_HEADS = GM_W // GM_HEAD_DIM
S5_GROUP = 16
S5_GROUPS = S5_W // S5_GROUP
S5_STATE = 64
S5_STATES = S5_GROUPS * S5_STATE
IN_COLS = S5_W + 2 * GM_W
N_EXPERTS = 16
CAPACITY_FACTOR = 2
D_FF = 2816
DEPTH = 1
ALPHA = (2.0 * DEPTH) ** 0.25
LN_EPS = 1e-6

F32 = jnp.float32
BF16 = jnp.bfloat16
HIGHEST = lax.Precision.HIGHEST

LANES = 128
SUBLANES = 8
ROW_CHUNKS = D_MODEL // LANES
VMEM_LIMIT_BYTES = 58 * 1024 * 1024

TOK_TILE = 512
SCAN_T = 256
SCAN_LANES = 512
SCAN_HALF = SUBLANES // 2
ROW_TILE = 256
COMPACT_WIN = CHUNK + SUBLANES
F32_TINY = 2.0 ** -126
BRACKET_GEO_STEPS = 8
BRACKET_LIN_STEPS = 26


def _layer_norm(x):
    mu = jnp.mean(x, axis=-1, keepdims=True)
    xc = x - mu
    var = jnp.mean(xc * xc, axis=-1, keepdims=True)
    return xc * lax.rsqrt(var + LN_EPS)


def _params(semantics, vmem=None):
    return pltpu.CompilerParams(dimension_semantics=semantics, vmem_limit_bytes=vmem)


def _store_row_tiles(ref, first_token, x):
    n = x.shape[0]
    for s in range(ROW_CHUNKS):
        ref[pl.ds(first_token * ROW_CHUNKS + s, n, stride=ROW_CHUNKS), :] = x[:, s * LANES:(s + 1) * LANES]


def _load_row_tiles(ref, first_token, n):
    return jnp.concatenate(
        [ref[pl.ds(first_token * ROW_CHUNKS + s, n, stride=ROW_CHUNKS), :] for s in range(ROW_CHUNKS)], axis=1)


def _row_tile(token):
    return pl.ds(pl.multiple_of(token * ROW_CHUNKS, ROW_CHUNKS), ROW_CHUNKS)


def _cast_spec(w, n_steps, step_of):
    ne, rows, cols = w.shape
    per_expert = n_steps // ne
    assert per_expert * ne == n_steps and rows % (per_expert * 2 * SUBLANES) == 0
    return pl.BlockSpec((None, rows // per_expert, cols),
                        lambda *ids: (step_of(*ids) // per_expert, step_of(*ids) % per_expert, 0))


def _cast_block(w_ref, o_ref):
    o_ref[...] = w_ref[...].astype(BF16)


def _adaln_kernel(c_ref, w_ref, b_ref, o_ref):
    c = c_ref[...]
    a = c * jax.nn.sigmoid(c)
    o_ref[...] = jnp.dot(a, w_ref[...], precision=HIGHEST, preferred_element_type=F32) + b_ref[...]


def _adaln(cc, w_ada, b_ada):
    n = w_ada.shape[1] // D_MODEL
    return pl.pallas_call(
        _adaln_kernel,
        grid=(n,),
        in_specs=[pl.BlockSpec((SUBLANES, D_MODEL), lambda j: (0, 0)),
                  pl.BlockSpec((D_MODEL, D_MODEL), lambda j: (0, j)),
                  pl.BlockSpec((1, D_MODEL), lambda j: (0, j))],
        out_specs=pl.BlockSpec((SUBLANES, D_MODEL), lambda j: (0, j)),
        out_shape=jax.ShapeDtypeStruct((SUBLANES, w_ada.shape[1]), F32),
        compiler_params=_params(("arbitrary",)),
        name="adaln",
    )(cc, w_ada, b_ada)


def _inproj_kernel(x_ref, mod_ref, w_ref, ws_ref, bs_ref, wcast_ref, zs_ref, gm_ref, wcast_out_ref):
    _cast_block(wcast_ref, wcast_out_ref)
    h = _layer_norm(x_ref[...]) * (1.0 + mod_ref[1:2, :]) + mod_ref[0:1, :]
    z = jnp.dot(h.astype(BF16), w_ref[...], preferred_element_type=F32)
    zs_ref[...] = z[:, :S5_W]
    for g in range(GM_HEADS):
        lo = S5_W + g * GM_HEAD_DIM
        u = jax.nn.gelu(z[:, lo:lo + GM_HEAD_DIM])
        v = jax.nn.gelu(z[:, lo + GM_W:lo + GM_W + GM_HEAD_DIM])
        vn = _layer_norm(v).astype(BF16)
        wsg = ws_ref[g]
        bsg = bs_ref[g]
        for c in range(TOK_TILE // CHUNK):
            rows = slice(c * CHUNK, (c + 1) * CHUNK)
            mixed = jnp.dot(wsg, vn[rows, :], preferred_element_type=F32) + bsg
            gm_ref[rows, g * GM_HEAD_DIM:(g + 1) * GM_HEAD_DIM] = (u[rows, :] * mixed).astype(BF16)


def _inproj(x, mod, w_in, gm_ws, gm_bs, w_cast):
    bn, length, _ = x.shape
    nt = length // TOK_TILE
    cast = _cast_spec(w_cast, bn * nt, lambda b, i: b * nt + i)
    return pl.pallas_call(
        _inproj_kernel,
        grid=(bn, nt),
        in_specs=[pl.BlockSpec((None, TOK_TILE, D_MODEL), lambda b, i: (b, i, 0)),
                  pl.BlockSpec((None, 6, D_MODEL), lambda b, i: (b, 0, 0)),
                  pl.BlockSpec((D_MODEL, IN_COLS), lambda b, i: (0, 0)),
                  pl.BlockSpec((GM_HEADS, CHUNK, CHUNK), lambda b, i: (0, 0, 0)),
                  pl.BlockSpec((GM_HEADS, CHUNK, 1), lambda b, i: (0, 0, 0)),
                  cast],
        out_specs=[pl.BlockSpec((TOK_TILE, S5_W), lambda b, i: (i, b)),
                   pl.BlockSpec((None, TOK_TILE, GM_W), lambda b, i: (b, i, 0)),
                   cast],
        out_shape=[jax.ShapeDtypeStruct((length, bn * S5_W), F32),
                   jax.ShapeDtypeStruct((bn, length, GM_W), BF16),
                   jax.ShapeDtypeStruct(w_cast.shape, BF16)],
        compiler_params=_params(("arbitrary", "arbitrary"), VMEM_LIMIT_BYTES),
        name="inproj",
    )(x, mod, w_in, gm_ws, gm_bs, w_cast)


def _ctxproj_kernel(x_ref, mod_ref, w_ref, o_ref):
    h = _layer_norm(x_ref[...]) * (1.0 + mod_ref[1:2, :]) + mod_ref[0:1, :]
    o_ref[...] = jnp.dot(h.astype(BF16), w_ref[...], preferred_element_type=F32)


def _ctxproj(ctx, mod_ctx, w_in):
    bn, clen, _ = ctx.shape
    return pl.pallas_call(
        _ctxproj_kernel,
        grid=(bn,),
        in_specs=[pl.BlockSpec((None, clen, D_MODEL), lambda b: (b, 0, 0)),
                  pl.BlockSpec((None, 6, D_MODEL), lambda b: (0, 0, 0)),
                  pl.BlockSpec((D_MODEL, S5_W), lambda b: (0, 0))],
        out_specs=pl.BlockSpec((clen, S5_W), lambda b: (0, b)),
        out_shape=jax.ShapeDtypeStruct((clen, bn * S5_W), F32),
        compiler_params=_params(("arbitrary",)),
        name="ctxproj",
    )(ctx, mod_ctx, w_in)


def _s5_tables(a_re, a_im, log_step, b_re, b_im, c_re, c_im):
    step = jnp.exp(log_step)[..., None]
    mag = jnp.exp(a_re * step)
    lb_re = mag * jnp.cos(a_im * step)
    lb_im = mag * jnp.sin(a_im * step)
    den = a_re * a_re + a_im * a_im
    q_re = ((lb_re - 1.0) * a_re + lb_im * a_im) / den
    q_im = (lb_im * a_re - (lb_re - 1.0) * a_im) / den
    bb_re = q_re[..., None] * b_re - q_im[..., None] * b_im
    bb_im = q_re[..., None] * b_im + q_im[..., None] * b_re
    eye = jnp.eye(S5_GROUPS, dtype=F32)

    def in_block(m):
        return jnp.einsum('dgph,gk->dghkp', m, eye).reshape(2, S5_W, S5_STATES)

    def out_block(m):
        return jnp.einsum('dghp,gk->dgpkh', m, eye).reshape(2, S5_STATES, S5_W)

    bmat = jnp.concatenate([in_block(bb_re), in_block(bb_im)], axis=2).astype(BF16)
    cmat = jnp.concatenate([out_block(c_re), out_block(-c_im)], axis=1).astype(BF16)
    lam = jnp.concatenate([lb_re.reshape(2, 1, S5_STATES), lb_im.reshape(2, 1, S5_STATES)], axis=2)
    tab = jnp.concatenate([jnp.repeat(lam[0], SCAN_HALF, axis=0), jnp.repeat(lam[1], SCAN_HALF, axis=0)], axis=0)
    return bmat, cmat, tab


def _scan_tiles(df_ref, db_ref, s_ref, tab_ref):
    n_tiles = df_ref.shape[1] // SUBLANES
    slabs_per_block = SCAN_LANES // LANES
    top = lax.broadcasted_iota(jnp.int32, (SUBLANES, SCAN_LANES), 0) < SCAN_HALF

    def advance(l_re, l_im, s_re, s_im, x_re, x_im):
        return x_re + l_re * s_re - l_im * s_im, x_im + l_re * s_im + l_im * s_re

    def swap(x):
        return pltpu.roll(x, SCAN_HALF, 0)

    for lb in range(S5_STATES // SCAN_LANES):
        re_cols = pl.ds(lb * SCAN_LANES, SCAN_LANES)
        im_cols = pl.ds(S5_STATES + lb * SCAN_LANES, SCAN_LANES)
        re_slabs = [lb * slabs_per_block + k for k in range(slabs_per_block)]
        im_slabs = [S5_STATES // LANES + s for s in re_slabs]
        l_re, l_im = tab_ref[:, re_cols], tab_ref[:, im_cols]

        def body(i, s, re_slabs=re_slabs, im_slabs=im_slabs, l_re=l_re, l_im=l_im):
            rows_f = pl.ds(pl.multiple_of(i * SUBLANES, SUBLANES), SUBLANES)
            rows_b = pl.ds(pl.multiple_of((n_tiles - 1 - i) * SUBLANES, SUBLANES), SUBLANES)
            out = []
            for slabs, s_part in ((re_slabs, 0), (im_slabs, 1)):
                f = jnp.concatenate([df_ref[k, rows_f, :] for k in slabs], axis=1)
                b = jnp.concatenate([db_ref[k, rows_b, :] for k in slabs], axis=1)
                out.append((jnp.where(top, f, b), swap(jnp.where(top, b, f))))
            (x0_re, x1_re), (x0_im, x1_im) = out
            y0_re, y0_im = advance(l_re, l_im, s[0], s[1], x0_re, x0_im)
            y1_re, y1_im = advance(l_re, l_im, y0_re, y0_im, x1_re, x1_im)
            for slabs, y0, y1 in ((re_slabs, y0_re, y1_re), (im_slabs, y0_im, y1_im)):
                r1 = swap(y1)
                f_new = jnp.where(top, y0, r1)
                b_new = jnp.where(top, r1, y0)
                for n, k in enumerate(slabs):
                    df_ref[k, rows_f, :] = f_new[:, n * LANES:(n + 1) * LANES]
                    db_ref[k, rows_b, :] = b_new[:, n * LANES:(n + 1) * LANES]
            return y1_re, y1_im

        s_re, s_im = lax.fori_loop(0, n_tiles, body, (s_ref[:, re_cols], s_ref[:, im_cols]))
        s_ref[:, re_cols] = s_re
        s_ref[:, im_cols] = s_im


def _s5scan_kernel(uf_ref, ub_ref, uc_ref, bmat_ref, cmat_ref, tab_ref, wcast_ref, yf_ref, yb_ref, wcast_out_ref,
                   df_ref, db_ref, s_ref):
    _cast_block(wcast_ref, wcast_out_ref)
    j = pl.program_id(0)
    n_slabs = 2 * S5_STATES // LANES

    def batch_rows(b):
        return pl.ds(b, SCAN_T, stride=SCAN_HALF)

    def drive(u_ref, d_ref, direction):
        u = jnp.concatenate([u_ref[:, b * S5_W:(b + 1) * S5_W] for b in range(SCAN_HALF)], axis=0)
        d = jnp.dot(u.astype(BF16), bmat_ref[direction], preferred_element_type=F32)
        for b in range(SCAN_HALF):
            for k in range(n_slabs):
                d_ref[k, batch_rows(b), :] = d[b * SCAN_T:(b + 1) * SCAN_T, k * LANES:(k + 1) * LANES]

    def readout(d_ref, y_ref, direction):
        s = jnp.concatenate(
            [jnp.concatenate([d_ref[k, batch_rows(b), :] for k in range(n_slabs)], axis=1)
             for b in range(SCAN_HALF)], axis=0)
        y = jnp.dot(s.astype(BF16), cmat_ref[direction], preferred_element_type=F32)
        for b in range(SCAN_HALF):
            y_ref[:, b * S5_W:(b + 1) * S5_W] = y[b * SCAN_T:(b + 1) * SCAN_T, :]

    @pl.when(j == 0)
    def _context():
        s_ref[...] = jnp.zeros_like(s_ref)
        drive(uc_ref, df_ref, 0)
        drive(uc_ref, db_ref, 1)
        _scan_tiles(df_ref, db_ref, s_ref, tab_ref)

    drive(uf_ref, df_ref, 0)
    drive(ub_ref, db_ref, 1)
    _scan_tiles(df_ref, db_ref, s_ref, tab_ref)
    readout(df_ref, yf_ref, 0)
    readout(db_ref, yb_ref, 1)


def _s5scan(u_lat, u_ctx, bmat, cmat, tab, w_cast):
    length, cols = u_lat.shape
    n = length // SCAN_T
    assert u_ctx.shape == (SCAN_T, cols) and cols == SCAN_HALF * S5_W
    slab = pltpu.VMEM((2 * S5_STATES // LANES, SCAN_T * SCAN_HALF, LANES), F32)
    cast = _cast_spec(w_cast, n, lambda j: j)
    return pl.pallas_call(
        _s5scan_kernel,
        grid=(n,),
        in_specs=[pl.BlockSpec((SCAN_T, cols), lambda j: (j, 0)),
                  pl.BlockSpec((SCAN_T, cols), lambda j: (n - 1 - j, 0)),
                  pl.BlockSpec((SCAN_T, cols), lambda j: (0, 0)),
                  pl.BlockSpec((2, S5_W, 2 * S5_STATES), lambda j: (0, 0, 0)),
                  pl.BlockSpec((2, 2 * S5_STATES, S5_W), lambda j: (0, 0, 0)),
                  pl.BlockSpec((SUBLANES, 2 * S5_STATES), lambda j: (0, 0)),
                  cast],
        out_specs=[pl.BlockSpec((SCAN_T, cols), lambda j: (j, 0)),
                   pl.BlockSpec((SCAN_T, cols), lambda j: (n - 1 - j, 0)),
                   cast],
        out_shape=[jax.ShapeDtypeStruct((length, cols), F32)] * 2 + [jax.ShapeDtypeStruct(w_cast.shape, BF16)],
        scratch_shapes=[slab, slab, pltpu.VMEM((SUBLANES, 2 * S5_STATES), F32)],
        compiler_params=_params(("arbitrary",), VMEM_LIMIT_BYTES),
        name="s5scan",
    )(u_lat, u_lat, u_ctx, bmat, cmat, tab, w_cast)


def _outproj_kernel(x_ref, zs_ref, yf_ref, yb_ref, gm_ref, mod_ref, d_ref, wglu_ref, bglu_ref, wout_ref,
                    ln_ref, wr_ref, x1_ref, hp_ref, aff_ref):
    y = d_ref[...] * zs_ref[...] + yf_ref[...] + yb_ref[...]
    g = jax.nn.gelu(y)
    s5 = g * jax.nn.sigmoid(jnp.dot(g.astype(BF16), wglu_ref[...], preferred_element_type=F32) + bglu_ref[...])
    mix = (jnp.dot(s5.astype(BF16), wout_ref[0:S5_W, :], preferred_element_type=F32)
           + jnp.dot(gm_ref[...], wout_ref[S5_W:, :], preferred_element_type=F32))
    x1 = _layer_norm(ALPHA * x_ref[...] + mod_ref[2:3, :] * mix) * ln_ref[0:1, :] + ln_ref[1:2, :]
    x1_ref[...] = x1
    h = _layer_norm(x1) * (1.0 + mod_ref[4:5, :]) + mod_ref[3:4, :]
    logits = lax.dot_general(wr_ref[...], h, (((1,), (1,)), ((), ())), precision=HIGHEST,
                             preferred_element_type=F32)
    ex = jnp.exp(logits - jnp.max(logits, axis=0, keepdims=True))
    aff_ref[...] = ex / jnp.sum(ex, axis=0, keepdims=True)
    _store_row_tiles(hp_ref, 0, h)


def _outproj(x, zs, yf, yb, gm, mod, d, w_glu, b_glu, w_out, ln1, w_rt):
    bn, length, _ = x.shape
    tok = lambda b, i: (b, i, 0)
    s5 = lambda b, i: (i, b)
    const2 = lambda b, i: (0, 0)
    return pl.pallas_call(
        _outproj_kernel,
        grid=(bn, length // TOK_TILE),
        in_specs=[pl.BlockSpec((None, TOK_TILE, D_MODEL), tok),
                  pl.BlockSpec((TOK_TILE, S5_W), s5),
                  pl.BlockSpec((TOK_TILE, S5_W), s5),
                  pl.BlockSpec((TOK_TILE, S5_W), s5),
                  pl.BlockSpec((None, TOK_TILE, GM_W), tok),
                  pl.BlockSpec((None, 6, D_MODEL), lambda b, i: (b, 0, 0)),
                  pl.BlockSpec((1, S5_W), const2),
                  pl.BlockSpec((S5_W, S5_W), const2),
                  pl.BlockSpec((1, S5_W), const2),
                  pl.BlockSpec((D_MODEL, D_MODEL), const2),
                  pl.BlockSpec((2, D_MODEL), const2),
                  pl.BlockSpec((N_EXPERTS, D_MODEL), const2)],
        out_specs=[pl.BlockSpec((None, TOK_TILE, D_MODEL), tok),
                   pl.BlockSpec((None, TOK_TILE * ROW_CHUNKS, LANES), tok),
                   pl.BlockSpec((None, N_EXPERTS, TOK_TILE), lambda b, i: (b, 0, i))],
        out_shape=[jax.ShapeDtypeStruct((bn, length, D_MODEL), F32),
                   jax.ShapeDtypeStruct((bn, length * ROW_CHUNKS, LANES), F32),
                   jax.ShapeDtypeStruct((bn, N_EXPERTS, length), F32)],
        compiler_params=_params(("arbitrary", "arbitrary"), VMEM_LIMIT_BYTES),
        name="outproj",
    )(x, zs, yf, yb, gm, mod, d, w_glu, b_glu, w_out, ln1, w_rt)


def _route_kernel(aff_ref, key_ref, offs_ref, *, cap):
    aff = aff_ref[...]
    ne, nr, _ = aff.shape

    def count(mask):
        return jnp.sum(jnp.sum(mask.astype(F32), axis=2, keepdims=True), axis=1, keepdims=True)

    normal = count(aff >= F32_TINY) >= cap
    lo = jnp.where(normal, F32_TINY, 0.0)
    hi = jnp.where(normal, 2.0, F32_TINY)
    for it in range(BRACKET_GEO_STEPS + BRACKET_LIN_STEPS):
        mid = 0.5 * (lo + hi)
        if it < BRACKET_GEO_STEPS:
            mid = jnp.where(normal, jnp.sqrt(lo * hi), mid)
        ok = count(aff >= mid) >= cap
        lo = jnp.where(ok, mid, lo)
        hi = jnp.where(ok, hi, mid)
    thr = jnp.min(jnp.min(jnp.where(aff >= lo, aff, 2.0), axis=2, keepdims=True), axis=1, keepdims=True)
    above = aff > thr
    tied = aff == thr
    need = cap - count(above)

    kk = lax.broadcasted_iota(jnp.int32, (LANES, LANES), 0)
    ll = lax.broadcasted_iota(jnp.int32, (LANES, LANES), 1)
    upper = (kk <= ll).astype(BF16)
    ones = jnp.ones((LANES, LANES), BF16)
    ri = lax.broadcasted_iota(jnp.int32, (nr, nr), 0)
    rj = lax.broadcasted_iota(jnp.int32, (nr, nr), 1)
    lower = (rj < ri).astype(BF16)

    def exclusive_prefix(mask):
        m = mask.astype(BF16).reshape(ne * nr, LANES)
        in_row = jnp.dot(m, upper, preferred_element_type=F32).reshape(ne, nr, LANES)
        row_tot = jnp.dot(m, ones, preferred_element_type=F32).reshape(ne, nr, LANES)
        row_off = jnp.stack([jnp.dot(lower, row_tot[e].astype(BF16), preferred_element_type=F32)
                             for e in range(ne)])
        return in_row - mask.astype(F32) + row_off, row_off

    tie_rank, _ = exclusive_prefix(tied)
    sel = above | (tied & (tie_rank < need))
    pos, row_off = exclusive_prefix(sel)
    key_ref[...] = jnp.where(sel, pos, -1.0)
    offs_ref[...] = row_off


def _route(aff4, cap):
    bn, ne, nr, _ = aff4.shape
    spec = pl.BlockSpec((None, ne, nr, LANES), lambda b: (b, 0, 0, 0))
    return pl.pallas_call(
        functools.partial(_route_kernel, cap=cap),
        grid=(bn,),
        in_specs=[spec],
        out_specs=[spec, spec],
        out_shape=[jax.ShapeDtypeStruct(aff4.shape, F32)] * 2,
        compiler_params=_params(("arbitrary",), VMEM_LIMIT_BYTES),
        name="route",
    )(aff4)


def _compact_kernel(offs_sref, key_ref, aff_ref, wcast_ref, idx_ref, gate_ref, wcast_out_ref, acc_i, acc_g, *, cap):
    _cast_block(wcast_ref, wcast_out_ref)
    be = pl.program_id(0) * pl.num_programs(1) + pl.program_id(1)
    nr = key_ref.shape[0]
    acc_i[...] = jnp.zeros_like(acc_i)
    acc_g[...] = jnp.zeros_like(acc_g)
    slot = lax.broadcasted_iota(jnp.int32, (COMPACT_WIN, LANES), 0)
    lane = lax.broadcasted_iota(jnp.int32, (1, LANES), 1)

    def body(r, carry):
        off = offs_sref[be * nr + r]
        base = pl.multiple_of((off // SUBLANES) * SUBLANES, SUBLANES)
        win = pl.ds(base, COMPACT_WIN)
        hit = (slot + base) == key_ref[r].astype(jnp.int32)
        tok = (lane + r * LANES).astype(F32)
        acc_i[win, :] += jnp.where(hit, tok, 0.0)
        acc_g[win, :] += jnp.where(hit, aff_ref[r], 0.0)
        return carry

    lax.fori_loop(0, nr, body, 0)
    idx_ref[...] = jnp.sum(acc_i[0:cap, :], axis=1, keepdims=True).astype(jnp.int32)
    gate_ref[...] = jnp.sum(acc_g[0:cap, :], axis=1, keepdims=True)


def _compact(row_offs, key5, aff5, cap, w_cast):
    bn, ne, nr, _, _ = key5.shape
    spec_in = pl.BlockSpec((None, None, nr, 1, LANES), lambda b, e, offs: (b, e, 0, 0, 0))
    spec_out = pl.BlockSpec((None, None, cap, 1), lambda b, e, offs: (b, e, 0, 0))
    cast = _cast_spec(w_cast, bn * ne, lambda b, e, offs: b * ne + e)
    return pl.pallas_call(
        functools.partial(_compact_kernel, cap=cap),
        grid_spec=pltpu.PrefetchScalarGridSpec(
            num_scalar_prefetch=1,
            grid=(bn, ne),
            in_specs=[spec_in, spec_in, cast],
            out_specs=[spec_out, spec_out, cast],
            scratch_shapes=[pltpu.VMEM((cap + COMPACT_WIN, LANES), F32),
                            pltpu.VMEM((cap + COMPACT_WIN, LANES), F32)]),
        out_shape=[jax.ShapeDtypeStruct((bn, ne, cap, 1), jnp.int32),
                   jax.ShapeDtypeStruct((bn, ne, cap, 1), F32),
                   jax.ShapeDtypeStruct(w_cast.shape, BF16)],
        compiler_params=_params(("arbitrary", "arbitrary"), VMEM_LIMIT_BYTES),
        name="compact",
    )(row_offs, key5, aff5, w_cast)


GATHER_UNROLL = 8


def _gather_kernel(idx_sref, h_ref, o_ref, *, cap):
    be = pl.program_id(0) * pl.num_programs(1) + pl.program_id(1)

    def body(i, carry):
        for k in range(GATHER_UNROLL):
            c = i * GATHER_UNROLL + k
            o_ref[_row_tile(c), :] = h_ref[_row_tile(idx_sref[be * cap + c]), :]
        return carry

    lax.fori_loop(0, cap // GATHER_UNROLL, body, 0)


def _gather(idx_flat, h, ne, cap):
    bn, rows, _ = h.shape
    return pl.pallas_call(
        functools.partial(_gather_kernel, cap=cap),
        grid_spec=pltpu.PrefetchScalarGridSpec(
            num_scalar_prefetch=1,
            grid=(bn, ne),
            in_specs=[pl.BlockSpec((None, rows, LANES), lambda b, e, idx: (b, 0, 0),
                                   pipeline_mode=pl.Buffered(1))],
            out_specs=pl.BlockSpec((None, None, cap * ROW_CHUNKS, LANES), lambda b, e, idx: (b, e, 0, 0))),
        out_shape=jax.ShapeDtypeStruct((bn, ne, cap * ROW_CHUNKS, LANES), F32),
        compiler_params=_params(("arbitrary", "arbitrary"), VMEM_LIMIT_BYTES),
        name="gather",
    )(idx_flat, h)


def _experts_kernel(xg_ref, gate_ref, wg_ref, wu_ref, wd_ref, res_ref, *, cap):
    for m in range(cap // ROW_TILE):
        rows = slice(m * ROW_TILE, (m + 1) * ROW_TILE)
        xr = _load_row_tiles(xg_ref, rows.start, ROW_TILE).astype(BF16)
        hg = jnp.dot(xr, wg_ref[...], preferred_element_type=F32)
        hu = jnp.dot(xr, wu_ref[...], preferred_element_type=F32)
        hid = (hg * jax.nn.sigmoid(hg) * hu).astype(BF16)
        out = jnp.dot(hid, wd_ref[...], preferred_element_type=F32)
        _store_row_tiles(res_ref, rows.start, out * gate_ref[rows, :])


def _experts(xg, gate, wg, wu, wd):
    bn, ne, rows, _ = xg.shape
    cap = rows // ROW_CHUNKS
    return pl.pallas_call(
        functools.partial(_experts_kernel, cap=cap),
        grid=(bn, ne),
        in_specs=[pl.BlockSpec((None, None, rows, LANES), lambda b, e: (b, e, 0, 0)),
                  pl.BlockSpec((None, None, cap, 1), lambda b, e: (b, e, 0, 0)),
                  pl.BlockSpec((None, D_MODEL, D_FF), lambda b, e: (e, 0, 0)),
                  pl.BlockSpec((None, D_MODEL, D_FF), lambda b, e: (e, 0, 0)),
                  pl.BlockSpec((None, D_FF, D_MODEL), lambda b, e: (e, 0, 0))],
        out_specs=pl.BlockSpec((None, None, cap * ROW_CHUNKS, LANES), lambda b, e: (b, e, 0, 0)),
        out_shape=jax.ShapeDtypeStruct((bn, ne, cap * ROW_CHUNKS, LANES), F32),
        compiler_params=_params(("arbitrary", "arbitrary"), VMEM_LIMIT_BYTES),
        name="experts",
    )(xg, gate, wg, wu, wd)


SCATTER_UNROLL = 8


def _combine_kernel(idx_sref, res_ref, x1_ref, mod_ref, ln_ref, o_ref, acc_ref, *, cap, ne):
    b = pl.program_id(0)
    step = pl.program_id(1)

    @pl.when(step == 0)
    def _init():
        acc_ref[...] = jnp.zeros_like(acc_ref)

    @pl.when(step < ne)
    def _scatter():
        base = (b * ne + step) * cap

        def body(i, carry):
            first = i * SCATTER_UNROLL
            toks = [_row_tile(idx_sref[base + first + k]) for k in range(SCATTER_UNROLL)]
            sums = [acc_ref[toks[k], :] + res_ref[_row_tile(first + k), :] for k in range(SCATTER_UNROLL)]
            for k in range(SCATTER_UNROLL):
                acc_ref[toks[k], :] = sums[k]
            return carry

        lax.fori_loop(0, cap // SCATTER_UNROLL, body, 0)

    @pl.when(step >= ne)
    def _finish():
        moe = _load_row_tiles(acc_ref, (step - ne) * TOK_TILE, TOK_TILE)
        y = ALPHA * x1_ref[...] + mod_ref[5:6, :] * moe
        o_ref[...] = _layer_norm(y) * ln_ref[0:1, :] + ln_ref[1:2, :]


def _combine(idx_flat, res, x1, mod, ln2, cap):
    bn, length, _ = x1.shape
    ne = res.shape[1]
    assert cap % SCATTER_UNROLL == 0
    tok = lambda b, s, idx: (b, jnp.maximum(s - ne, 0), 0)
    return pl.pallas_call(
        functools.partial(_combine_kernel, cap=cap, ne=ne),
        grid_spec=pltpu.PrefetchScalarGridSpec(
            num_scalar_prefetch=1,
            grid=(bn, ne + length // TOK_TILE),
            in_specs=[pl.BlockSpec((None, None, cap * ROW_CHUNKS, LANES),
                                   lambda b, s, idx: (b, jnp.minimum(s, ne - 1), 0, 0)),
                      pl.BlockSpec((None, TOK_TILE, D_MODEL), tok),
                      pl.BlockSpec((None, 6, D_MODEL), lambda b, s, idx: (b, 0, 0)),
                      pl.BlockSpec((2, D_MODEL), lambda b, s, idx: (0, 0))],
            out_specs=pl.BlockSpec((None, TOK_TILE, D_MODEL), tok),
            scratch_shapes=[pltpu.VMEM((length * ROW_CHUNKS, LANES), F32)]),
        out_shape=jax.ShapeDtypeStruct((bn, length, D_MODEL), F32),
        compiler_params=_params(("arbitrary", "arbitrary"), VMEM_LIMIT_BYTES),
        name="combine",
    )(idx_flat, res, x1, mod, ln2)


def kernel(x, c, ctx, c_ctx, w_ada, b_ada, w_in, gm_ws, gm_bs, s5_a_re, s5_a_im, s5_log_step, s5_b_re, s5_b_im,
           s5_c_re, s5_c_im, s5_d, s5_w_glu, s5_b_glu, w_out, ln1_g, ln1_b, w_router, moe_w_gate, moe_w_up,
           moe_w_down, ln2_g, ln2_b):
    assert w_ada.shape[0] == DEPTH == 1
    bn, length, _ = x.shape
    assert bn == SCAN_HALF and ctx.shape[1] == SCAN_T and length % TOK_TILE == 0
    cap = CAPACITY_FACTOR * length // N_EXPERTS

    cc = jnp.zeros((SUBLANES, D_MODEL), F32).at[:bn].set(c).at[bn].set(c_ctx)
    mod = _adaln(cc, w_ada[0], b_ada[0][None, :])
    mod_lat = mod[:bn].reshape(bn, 6, D_MODEL)
    mod_ctx = mod[bn:bn + 1].reshape(1, 6, D_MODEL)

    w_in_b = w_in[0].astype(BF16)
    zs, gm, wg_b = _inproj(x, mod_lat, w_in_b, gm_ws[0].astype(BF16), gm_bs[0][:, :, None], moe_w_gate[0])
    zc = _ctxproj(ctx, mod_ctx, w_in_b)

    bmat, cmat, tab = _s5_tables(s5_a_re[0], s5_a_im[0], s5_log_step[0], s5_b_re[0], s5_b_im[0],
                                 s5_c_re[0], s5_c_im[0])
    yf, yb, wd_b = _s5scan(zs, zc, bmat, cmat, tab, moe_w_down[0])

    x1, hp, aff = _outproj(x, zs, yf, yb, gm, mod_lat, s5_d[0].reshape(1, S5_W), s5_w_glu[0].astype(BF16),
                           s5_b_glu[0][None, :], w_out[0].astype(BF16), jnp.stack([ln1_g[0], ln1_b[0]]),
                           w_router[0].T)

    aff4 = aff.reshape(bn, N_EXPERTS, length // LANES, LANES)
    key4, offs4 = _route(aff4, cap)
    row_offs = offs4[..., 0].astype(jnp.int32)
    rows5 = (bn, N_EXPERTS, length // LANES, 1, LANES)
    idx, gate, wu_b = _compact(row_offs.reshape(-1), key4.reshape(rows5), aff4.reshape(rows5), cap, moe_w_up[0])
    idx_flat = idx.reshape(-1)

    xg = _gather(idx_flat, hp, N_EXPERTS, cap)
    res = _experts(xg, gate, wg_b, wu_b, wd_b)

    return _combine(idx_flat, res, x1, mod_lat, jnp.stack([ln2_g[0], ln2_b[0]]), cap)
```

```python
import functools

import jax
import jax.numpy as jnp
from jax import lax
from jax.experimental import pallas as pl
from jax.experimental.pallas import tpu as pltpu

D_MODEL = 1024
CHUNK = 128
S5_W = D_MODEL // 4
GM_W = D_MODEL - S5_W
GM_HEAD_DIM = 128
GM_HEADS = GM_W // GM_HEAD_DIM
S5_GROUP = 16
S5_GROUPS = S5_W // S5_GROUP
S5_STATE = 64
S5_STATES = S5_GROUPS * S5_STATE
IN_COLS = S5_W + 2 * GM_W
N_EXPERTS = 16
CAPACITY_FACTOR = 2
D_FF = 2816
DEPTH = 1
ALPHA = (2.0 * DEPTH) ** 0.25
LN_EPS = 1e-6

F32 = jnp.float32
BF16 = jnp.bfloat16
HIGHEST = lax.Precision.HIGHEST

LANES = 128
SUBLANES = 8
ROW_CHUNKS = D_MODEL // LANES
VMEM_LIMIT_BYTES = 58 * 1024 * 1024

TOK_TILE = 1024
FIN_TILE = 512
SCAN_T = 256
SCAN_LANES = 512
SCAN_HALF = SUBLANES // 2
ROW_TILE = 256
COMPACT_WIN = CHUNK + SUBLANES
F32_TINY = 2.0 ** -126
BRACKET_GEO_STEPS = 8
BRACKET_LIN_STEPS = 26


def _layer_norm(x):
    mu = jnp.mean(x, axis=-1, keepdims=True)
    xc = x - mu
    var = jnp.mean(xc * xc, axis=-1, keepdims=True)
    return xc * lax.rsqrt(var + LN_EPS)


def _params(semantics, vmem=None):
    return pltpu.CompilerParams(dimension_semantics=semantics, vmem_limit_bytes=vmem)


def _store_row_tiles(ref, first_token, x):
    n = x.shape[0]
    for s in range(ROW_CHUNKS):
        ref[pl.ds(first_token * ROW_CHUNKS + s, n, stride=ROW_CHUNKS), :] = x[:, s * LANES:(s + 1) * LANES]


def _load_row_tiles(ref, first_token, n):
    return jnp.concatenate(
        [ref[pl.ds(first_token * ROW_CHUNKS + s, n, stride=ROW_CHUNKS), :] for s in range(ROW_CHUNKS)], axis=1)


def _row_tile(token):
    return pl.ds(pl.multiple_of(token * ROW_CHUNKS, ROW_CHUNKS), ROW_CHUNKS)


def _tile_at(first_row):
    return pl.ds(pl.multiple_of(first_row, ROW_CHUNKS), ROW_CHUNKS)


def _cast_spec(w, n_steps, step_of):
    ne, rows, cols = w.shape
    per_expert = n_steps // ne
    assert per_expert * ne == n_steps and rows % (per_expert * 2 * SUBLANES) == 0
    return pl.BlockSpec((None, rows // per_expert, cols),
                        lambda *ids: (step_of(*ids) // per_expert, step_of(*ids) % per_expert, 0))


def _cast_block(w_ref, o_ref):
    o_ref[...] = w_ref[...].astype(BF16)


def _adaln_kernel(c_ref, w_ref, b_ref, o_ref):
    c = c_ref[...]
    a = c * jax.nn.sigmoid(c)
    o_ref[...] = jnp.dot(a, w_ref[...], precision=HIGHEST, preferred_element_type=F32) + b_ref[...]


def _adaln(cc, w_ada, b_ada):
    n = w_ada.shape[1] // D_MODEL
    return pl.pallas_call(
        _adaln_kernel,
        grid=(n,),
        in_specs=[pl.BlockSpec((SUBLANES, D_MODEL), lambda j: (0, 0)),
                  pl.BlockSpec((D_MODEL, D_MODEL), lambda j: (0, j)),
                  pl.BlockSpec((1, D_MODEL), lambda j: (0, j))],
        out_specs=pl.BlockSpec((SUBLANES, D_MODEL), lambda j: (0, j)),
        out_shape=jax.ShapeDtypeStruct((SUBLANES, w_ada.shape[1]), F32),
        compiler_params=_params(("arbitrary",)),
        name="adaln",
    )(cc, w_ada, b_ada)


def _inproj_kernel(x_ref, mod_ref, w_ref, ws_ref, bs_ref, wcast_ref, zs_ref, gm_ref, wcast_out_ref):
    _cast_block(wcast_ref, wcast_out_ref)
    h = _layer_norm(x_ref[...]) * (1.0 + mod_ref[1:2, :]) + mod_ref[0:1, :]
    z = jnp.dot(h.astype(BF16), w_ref[...], preferred_element_type=F32)
    zs_ref[...] = z[:, :S5_W]
    for g in range(GM_HEADS):
        lo = S5_W + g * GM_HEAD_DIM
        u = jax.nn.gelu(z[:, lo:lo + GM_HEAD_DIM])
        v = jax.nn.gelu(z[:, lo + GM_W:lo + GM_W + GM_HEAD_DIM])
        vn = _layer_norm(v).astype(BF16)
        wsg = ws_ref[g]
        bsg = bs_ref[g]
        for c in range(TOK_TILE // CHUNK):
            rows = slice(c * CHUNK, (c + 1) * CHUNK)
            mixed = jnp.dot(wsg, vn[rows, :], preferred_element_type=F32) + bsg
            gm_ref[rows, g * GM_HEAD_DIM:(g + 1) * GM_HEAD_DIM] = (u[rows, :] * mixed).astype(BF16)


def _inproj(x, mod, w_in, gm_ws, gm_bs, w_cast):
    bn, length, _ = x.shape
    nt = length // TOK_TILE
    cast = _cast_spec(w_cast, bn * nt, lambda b, i: b * nt + i)
    return pl.pallas_call(
        _inproj_kernel,
        grid=(bn, nt),
        in_specs=[pl.BlockSpec((None, TOK_TILE, D_MODEL), lambda b, i: (b, i, 0)),
                  pl.BlockSpec((None, 6, D_MODEL), lambda b, i: (b, 0, 0)),
                  pl.BlockSpec((D_MODEL, IN_COLS), lambda b, i: (0, 0)),
                  pl.BlockSpec((GM_HEADS, CHUNK, CHUNK), lambda b, i: (0, 0, 0)),
                  pl.BlockSpec((GM_HEADS, CHUNK, 1), lambda b, i: (0, 0, 0)),
                  cast],
        out_specs=[pl.BlockSpec((TOK_TILE, S5_W), lambda b, i: (i, b)),
                   pl.BlockSpec((None, TOK_TILE, GM_W), lambda b, i: (b, i, 0)),
                   cast],
        out_shape=[jax.ShapeDtypeStruct((length, bn * S5_W), F32),
                   jax.ShapeDtypeStruct((bn, length, GM_W), BF16),
                   jax.ShapeDtypeStruct(w_cast.shape, BF16)],
        compiler_params=_params(("arbitrary", "arbitrary"), VMEM_LIMIT_BYTES),
        name="inproj",
    )(x, mod, w_in, gm_ws, gm_bs, w_cast)


def _ctxproj_kernel(x_ref, mod_ref, w_ref, o_ref):
    h = _layer_norm(x_ref[...]) * (1.0 + mod_ref[1:2, :]) + mod_ref[0:1, :]
    o_ref[...] = jnp.dot(h.astype(BF16), w_ref[...], preferred_element_type=F32)


def _ctxproj(ctx, mod_ctx, w_in):
    bn, clen, _ = ctx.shape
    return pl.pallas_call(
        _ctxproj_kernel,
        grid=(bn,),
        in_specs=[pl.BlockSpec((None, clen, D_MODEL), lambda b: (b, 0, 0)),
                  pl.BlockSpec((None, 6, D_MODEL), lambda b: (0, 0, 0)),
                  pl.BlockSpec((D_MODEL, S5_W), lambda b: (0, 0))],
        out_specs=pl.BlockSpec((clen, S5_W), lambda b: (0, b)),
        out_shape=jax.ShapeDtypeStruct((clen, bn * S5_W), F32),
        compiler_params=_params(("arbitrary",)),
        name="ctxproj",
    )(ctx, mod_ctx, w_in)


def _s5_tables(a_re, a_im, log_step, b_re, b_im, c_re, c_im):
    step = jnp.exp(log_step)[..., None]
    mag = jnp.exp(a_re * step)
    lb_re = mag * jnp.cos(a_im * step)
    lb_im = mag * jnp.sin(a_im * step)
    den = a_re * a_re + a_im * a_im
    q_re = ((lb_re - 1.0) * a_re + lb_im * a_im) / den
    q_im = (lb_im * a_re - (lb_re - 1.0) * a_im) / den
    bb_re = q_re[..., None] * b_re - q_im[..., None] * b_im
    bb_im = q_re[..., None] * b_im + q_im[..., None] * b_re
    eye = jnp.eye(S5_GROUPS, dtype=F32)

    def in_block(m):
        return jnp.einsum('dgph,gk->dghkp', m, eye).reshape(2, S5_W, S5_STATES)

    def out_block(m):
        return jnp.einsum('dghp,gk->dgpkh', m, eye).reshape(2, S5_STATES, S5_W)

    bmat = jnp.concatenate([in_block(bb_re), in_block(bb_im)], axis=2).astype(BF16)
    cmat = jnp.concatenate([out_block(c_re), out_block(-c_im)], axis=1).astype(BF16)
    lam = jnp.concatenate([lb_re.reshape(2, 1, S5_STATES), lb_im.reshape(2, 1, S5_STATES)], axis=2)
    tab = jnp.concatenate([jnp.repeat(lam[0], SCAN_HALF, axis=0), jnp.repeat(lam[1], SCAN_HALF, axis=0)], axis=0)
    return bmat, cmat, tab


def _scan_tiles(df_ref, db_ref, s_ref, tab_ref):
    n_tiles = df_ref.shape[1] // SUBLANES
    slabs_per_block = SCAN_LANES // LANES
    top = lax.broadcasted_iota(jnp.int32, (SUBLANES, SCAN_LANES), 0) < SCAN_HALF

    def advance(l_re, l_im, s_re, s_im, x_re, x_im):
        return x_re + l_re * s_re - l_im * s_im, x_im + l_re * s_im + l_im * s_re

    def swap(x):
        return pltpu.roll(x, SCAN_HALF, 0)

    for lb in range(S5_STATES // SCAN_LANES):
        re_cols = pl.ds(lb * SCAN_LANES, SCAN_LANES)
        im_cols = pl.ds(S5_STATES + lb * SCAN_LANES, SCAN_LANES)
        re_slabs = [lb * slabs_per_block + k for k in range(slabs_per_block)]
        im_slabs = [S5_STATES // LANES + s for s in re_slabs]
        l_re, l_im = tab_ref[:, re_cols], tab_ref[:, im_cols]

        def body(i, s, re_slabs=re_slabs, im_slabs=im_slabs, l_re=l_re, l_im=l_im):
            rows_f = pl.ds(pl.multiple_of(i * SUBLANES, SUBLANES), SUBLANES)
            rows_b = pl.ds(pl.multiple_of((n_tiles - 1 - i) * SUBLANES, SUBLANES), SUBLANES)
            out = []
            for slabs, s_part in ((re_slabs, 0), (im_slabs, 1)):
                f = jnp.concatenate([df_ref[k, rows_f, :] for k in slabs], axis=1)
                b = jnp.concatenate([db_ref[k, rows_b, :] for k in slabs], axis=1)
                out.append((jnp.where(top, f, b), swap(jnp.where(top, b, f))))
            (x0_re, x1_re), (x0_im, x1_im) = out
            y0_re, y0_im = advance(l_re, l_im, s[0], s[1], x0_re, x0_im)
            y1_re, y1_im = advance(l_re, l_im, y0_re, y0_im, x1_re, x1_im)
            for slabs, y0, y1 in ((re_slabs, y0_re, y1_re), (im_slabs, y0_im, y1_im)):
                r1 = swap(y1)
                f_new = jnp.where(top, y0, r1)
                b_new = jnp.where(top, r1, y0)
                for n, k in enumerate(slabs):
                    df_ref[k, rows_f, :] = f_new[:, n * LANES:(n + 1) * LANES]
                    db_ref[k, rows_b, :] = b_new[:, n * LANES:(n + 1) * LANES]
            return y1_re, y1_im

        s_re, s_im = lax.fori_loop(0, n_tiles, body, (s_ref[:, re_cols], s_ref[:, im_cols]))
        s_ref[:, re_cols] = s_re
        s_ref[:, im_cols] = s_im


def _s5scan_kernel(uf_ref, ub_ref, uc_ref, bmat_ref, cmat_ref, tab_ref, wcast_ref, yf_ref, yb_ref, wcast_out_ref,
                   df_ref, db_ref, s_ref):
    _cast_block(wcast_ref, wcast_out_ref)
    j = pl.program_id(0)
    n_slabs = 2 * S5_STATES // LANES

    def batch_rows(b):
        return pl.ds(b, SCAN_T, stride=SCAN_HALF)

    def drive(u_ref, d_ref, direction):
        u = jnp.concatenate([u_ref[:, b * S5_W:(b + 1) * S5_W] for b in range(SCAN_HALF)], axis=0)
        d = jnp.dot(u.astype(BF16), bmat_ref[direction], preferred_element_type=F32)
        for b in range(SCAN_HALF):
            for k in range(n_slabs):
                d_ref[k, batch_rows(b), :] = d[b * SCAN_T:(b + 1) * SCAN_T, k * LANES:(k + 1) * LANES]

    def readout(d_ref, y_ref, direction):
        s = jnp.concatenate(
            [jnp.concatenate([d_ref[k, batch_rows(b), :] for k in range(n_slabs)], axis=1)
             for b in range(SCAN_HALF)], axis=0)
        y = jnp.dot(s.astype(BF16), cmat_ref[direction], preferred_element_type=F32)
        for b in range(SCAN_HALF):
            y_ref[:, b * S5_W:(b + 1) * S5_W] = y[b * SCAN_T:(b + 1) * SCAN_T, :]

    @pl.when(j == 0)
    def _context():
        s_ref[...] = jnp.zeros_like(s_ref)
        drive(uc_ref, df_ref, 0)
        drive(uc_ref, db_ref, 1)
        _scan_tiles(df_ref, db_ref, s_ref, tab_ref)

    drive(uf_ref, df_ref, 0)
    drive(ub_ref, db_ref, 1)
    _scan_tiles(df_ref, db_ref, s_ref, tab_ref)
    readout(df_ref, yf_ref, 0)
    readout(db_ref, yb_ref, 1)


def _s5scan(u_lat, u_ctx, bmat, cmat, tab, w_cast):
    length, cols = u_lat.shape
    n = length // SCAN_T
    assert u_ctx.shape == (SCAN_T, cols) and cols == SCAN_HALF * S5_W
    slab = pltpu.VMEM((2 * S5_STATES // LANES, SCAN_T * SCAN_HALF, LANES), F32)
    cast = _cast_spec(w_cast, n, lambda j: j)
    return pl.pallas_call(
        _s5scan_kernel,
        grid=(n,),
        in_specs=[pl.BlockSpec((SCAN_T, cols), lambda j: (j, 0)),
                  pl.BlockSpec((SCAN_T, cols), lambda j: (n - 1 - j, 0)),
                  pl.BlockSpec((SCAN_T, cols), lambda j: (0, 0)),
                  pl.BlockSpec((2, S5_W, 2 * S5_STATES), lambda j: (0, 0, 0)),
                  pl.BlockSpec((2, 2 * S5_STATES, S5_W), lambda j: (0, 0, 0)),
                  pl.BlockSpec((SUBLANES, 2 * S5_STATES), lambda j: (0, 0)),
                  cast],
        out_specs=[pl.BlockSpec((SCAN_T, cols), lambda j: (j, 0)),
                   pl.BlockSpec((SCAN_T, cols), lambda j: (n - 1 - j, 0)),
                   cast],
        out_shape=[jax.ShapeDtypeStruct((length, cols), F32)] * 2 + [jax.ShapeDtypeStruct(w_cast.shape, BF16)],
        scratch_shapes=[slab, slab, pltpu.VMEM((SUBLANES, 2 * S5_STATES), F32)],
        compiler_params=_params(("arbitrary",), VMEM_LIMIT_BYTES),
        name="s5scan",
    )(u_lat, u_lat, u_ctx, bmat, cmat, tab, w_cast)


def _outproj_kernel(x_ref, zs_ref, yf_ref, yb_ref, gm_ref, mod_ref, d_ref, wglu_ref, bglu_ref, wout_ref,
                    ln_ref, wr_ref, x1_ref, hp_ref, aff_ref):
    y = d_ref[...] * zs_ref[...] + yf_ref[...] + yb_ref[...]
    g = jax.nn.gelu(y)
    s5 = g * jax.nn.sigmoid(jnp.dot(g.astype(BF16), wglu_ref[...], preferred_element_type=F32) + bglu_ref[...])
    mix = (jnp.dot(s5.astype(BF16), wout_ref[0:S5_W, :], preferred_element_type=F32)
           + jnp.dot(gm_ref[...], wout_ref[S5_W:, :], preferred_element_type=F32))
    x1 = _layer_norm(ALPHA * x_ref[...] + mod_ref[2:3, :] * mix) * ln_ref[0:1, :] + ln_ref[1:2, :]
    x1_ref[...] = x1
    h = _layer_norm(x1) * (1.0 + mod_ref[4:5, :]) + mod_ref[3:4, :]
    h_hi = h.astype(BF16)
    h_lo = (h - h_hi.astype(F32)).astype(BF16)
    contract_last = (((1,), (1,)), ((), ()))
    logits = (lax.dot_general(wr_ref[0], h_hi, contract_last, preferred_element_type=F32)
              + lax.dot_general(wr_ref[0], h_lo, contract_last, preferred_element_type=F32)
              + lax.dot_general(wr_ref[1], h_hi, contract_last, preferred_element_type=F32))
    ex = jnp.exp(logits - jnp.max(logits, axis=0, keepdims=True))
    aff_ref[...] = ex / jnp.sum(ex, axis=0, keepdims=True)
    _store_row_tiles(hp_ref, 0, h)


def _outproj(x, zs, yf, yb, gm, mod, d, w_glu, b_glu, w_out, ln1, w_rt):
    bn, length, _ = x.shape
    tok = lambda b, i: (b, i, 0)
    s5 = lambda b, i: (i, b)
    const2 = lambda b, i: (0, 0)
    return pl.pallas_call(
        _outproj_kernel,
        grid=(bn, length // TOK_TILE),
        in_specs=[pl.BlockSpec((None, TOK_TILE, D_MODEL), tok),
                  pl.BlockSpec((TOK_TILE, S5_W), s5),
                  pl.BlockSpec((TOK_TILE, S5_W), s5),
                  pl.BlockSpec((TOK_TILE, S5_W), s5),
                  pl.BlockSpec((None, TOK_TILE, GM_W), tok),
                  pl.BlockSpec((None, 6, D_MODEL), lambda b, i: (b, 0, 0)),
                  pl.BlockSpec((1, S5_W), const2),
                  pl.BlockSpec((S5_W, S5_W), const2),
                  pl.BlockSpec((1, S5_W), const2),
                  pl.BlockSpec((D_MODEL, D_MODEL), const2),
                  pl.BlockSpec((2, D_MODEL), const2),
                  pl.BlockSpec((2, N_EXPERTS, D_MODEL), lambda b, i: (0, 0, 0))],
        out_specs=[pl.BlockSpec((None, TOK_TILE, D_MODEL), tok),
                   pl.BlockSpec((None, TOK_TILE * ROW_CHUNKS, LANES), tok),
                   pl.BlockSpec((None, N_EXPERTS, TOK_TILE), lambda b, i: (b, 0, i))],
        out_shape=[jax.ShapeDtypeStruct((bn, length, D_MODEL), F32),
                   jax.ShapeDtypeStruct((bn, length * ROW_CHUNKS, LANES), F32),
                   jax.ShapeDtypeStruct((bn, N_EXPERTS, length), F32)],
        compiler_params=_params(("arbitrary", "arbitrary"), VMEM_LIMIT_BYTES),
        name="outproj",
    )(x, zs, yf, yb, gm, mod, d, w_glu, b_glu, w_out, ln1, w_rt)


def _route_kernel(aff_ref, key_ref, offs_ref, *, cap):
    aff = aff_ref[...]
    ne, nr, _ = aff.shape

    def count(mask):
        return jnp.sum(jnp.sum(mask.astype(F32), axis=2, keepdims=True), axis=1, keepdims=True)

    normal = count(aff >= F32_TINY) >= cap
    lo = jnp.where(normal, F32_TINY, 0.0)
    hi = jnp.where(normal, 2.0, F32_TINY)
    for it in range(BRACKET_GEO_STEPS + BRACKET_LIN_STEPS):
        mid = 0.5 * (lo + hi)
        if it < BRACKET_GEO_STEPS:
            mid = jnp.where(normal, jnp.sqrt(lo * hi), mid)
        ok = count(aff >= mid) >= cap
        lo = jnp.where(ok, mid, lo)
        hi = jnp.where(ok, hi, mid)
    thr = jnp.min(jnp.min(jnp.where(aff >= lo, aff, 2.0), axis=2, keepdims=True), axis=1, keepdims=True)
    above = aff > thr
    tied = aff == thr
    need = cap - count(above)

    kk = lax.broadcasted_iota(jnp.int32, (LANES, LANES), 0)
    ll = lax.broadcasted_iota(jnp.int32, (LANES, LANES), 1)
    upper = (kk <= ll).astype(BF16)
    ones = jnp.ones((LANES, LANES), BF16)
    ri = lax.broadcasted_iota(jnp.int32, (nr, nr), 0)
    rj = lax.broadcasted_iota(jnp.int32, (nr, nr), 1)
    lower = (rj < ri).astype(BF16)

    def exclusive_prefix(mask):
        m = mask.astype(BF16).reshape(ne * nr, LANES)
        in_row = jnp.dot(m, upper, preferred_element_type=F32).reshape(ne, nr, LANES)
        row_tot = jnp.dot(m, ones, preferred_element_type=F32).reshape(ne, nr, LANES)
        row_off = jnp.stack([jnp.dot(lower, row_tot[e].astype(BF16), preferred_element_type=F32)
                             for e in range(ne)])
        return in_row - mask.astype(F32) + row_off, row_off

    tie_rank, _ = exclusive_prefix(tied)
    sel = above | (tied & (tie_rank < need))
    pos, row_off = exclusive_prefix(sel)
    key_ref[...] = jnp.where(sel, pos, -1.0)
    offs_ref[...] = row_off


def _route(aff4, cap):
    bn, ne, nr, _ = aff4.shape
    spec = pl.BlockSpec((None, ne, nr, LANES), lambda b: (b, 0, 0, 0))
    return pl.pallas_call(
        functools.partial(_route_kernel, cap=cap),
        grid=(bn,),
        in_specs=[spec],
        out_specs=[spec, spec],
        out_shape=[jax.ShapeDtypeStruct(aff4.shape, F32)] * 2,
        compiler_params=_params(("arbitrary",), VMEM_LIMIT_BYTES),
        name="route",
    )(aff4)


def _compact_kernel(offs_sref, key_ref, aff_ref, wcast_ref, idx_ref, gate_ref, wcast_out_ref, acc_i, acc_g, *, cap):
    _cast_block(wcast_ref, wcast_out_ref)
    be = pl.program_id(0) * pl.num_programs(1) + pl.program_id(1)
    nr = key_ref.shape[0]
    acc_i[...] = jnp.zeros_like(acc_i)
    acc_g[...] = jnp.zeros_like(acc_g)
    slot = lax.broadcasted_iota(jnp.int32, (COMPACT_WIN, LANES), 0)
    lane = lax.broadcasted_iota(jnp.int32, (1, LANES), 1)

    def body(r, carry):
        off = offs_sref[be * nr + r]
        base = pl.multiple_of((off // SUBLANES) * SUBLANES, SUBLANES)
        win = pl.ds(base, COMPACT_WIN)
        hit = (slot + base) == key_ref[r].astype(jnp.int32)
        tok = (lane + r * LANES).astype(F32)
        acc_i[win, :] += jnp.where(hit, tok, 0.0)
        acc_g[win, :] += jnp.where(hit, aff_ref[r], 0.0)
        return carry

    lax.fori_loop(0, nr, body, 0)
    idx_ref[...] = jnp.sum(acc_i[0:cap, :], axis=1, keepdims=True).astype(jnp.int32)
    gate_ref[...] = jnp.sum(acc_g[0:cap, :], axis=1, keepdims=True)


def _compact(row_offs, key5, aff5, cap, w_cast):
    bn, ne, nr, _, _ = key5.shape
    spec_in = pl.BlockSpec((None, None, nr, 1, LANES), lambda b, e, offs: (b, e, 0, 0, 0))
    spec_out = pl.BlockSpec((None, None, cap, 1), lambda b, e, offs: (b, e, 0, 0))
    cast = _cast_spec(w_cast, bn * ne, lambda b, e, offs: b * ne + e)
    return pl.pallas_call(
        functools.partial(_compact_kernel, cap=cap),
        grid_spec=pltpu.PrefetchScalarGridSpec(
            num_scalar_prefetch=1,
            grid=(bn, ne),
            in_specs=[spec_in, spec_in, cast],
            out_specs=[spec_out, spec_out, cast],
            scratch_shapes=[pltpu.VMEM((cap + COMPACT_WIN, LANES), F32),
                            pltpu.VMEM((cap + COMPACT_WIN, LANES), F32)]),
        out_shape=[jax.ShapeDtypeStruct((bn, ne, cap, 1), jnp.int32),
                   jax.ShapeDtypeStruct((bn, ne, cap, 1), F32),
                   jax.ShapeDtypeStruct(w_cast.shape, BF16)],
        compiler_params=_params(("arbitrary", "arbitrary"), VMEM_LIMIT_BYTES),
        name="compact",
    )(row_offs, key5, aff5, w_cast)


GATHER_UNROLL = 8


def _gather_kernel(rows_sref, h_ref, o_ref, *, cap):
    be = pl.program_id(0) * pl.num_programs(1) + pl.program_id(1)

    def body(i, carry):
        for k in range(GATHER_UNROLL):
            c = i * GATHER_UNROLL + k
            o_ref[_row_tile(c), :] = h_ref[_tile_at(rows_sref[be * cap + c]), :]
        return carry

    lax.fori_loop(0, cap // GATHER_UNROLL, body, 0)


def _gather(tile_rows, h, ne, cap):
    bn, rows, _ = h.shape
    return pl.pallas_call(
        functools.partial(_gather_kernel, cap=cap),
        grid_spec=pltpu.PrefetchScalarGridSpec(
            num_scalar_prefetch=1,
            grid=(bn, ne),
            in_specs=[pl.BlockSpec((None, rows, LANES), lambda b, e, idx: (b, 0, 0),
                                   pipeline_mode=pl.Buffered(1))],
            out_specs=pl.BlockSpec((None, None, cap * ROW_CHUNKS, LANES), lambda b, e, idx: (b, e, 0, 0))),
        out_shape=jax.ShapeDtypeStruct((bn, ne, cap * ROW_CHUNKS, LANES), F32),
        compiler_params=_params(("arbitrary", "arbitrary"), VMEM_LIMIT_BYTES),
        name="gather",
    )(tile_rows, h)


def _experts_kernel(xg_ref, gate_ref, wg_ref, wu_ref, wd_ref, res_ref, *, cap):
    for m in range(cap // ROW_TILE):
        rows = slice(m * ROW_TILE, (m + 1) * ROW_TILE)
        xr = _load_row_tiles(xg_ref, rows.start, ROW_TILE).astype(BF16)
        hg = jnp.dot(xr, wg_ref[...], preferred_element_type=F32)
        hu = jnp.dot(xr, wu_ref[...], preferred_element_type=F32)
        hid = (hg * jax.nn.sigmoid(hg) * hu).astype(BF16)
        out = jnp.dot(hid, wd_ref[...], preferred_element_type=F32)
        _store_row_tiles(res_ref, rows.start, out * gate_ref[rows, :])


def _experts(xg, gate, wg, wu, wd):
    bn, ne, rows, _ = xg.shape
    cap = rows // ROW_CHUNKS
    return pl.pallas_call(
        functools.partial(_experts_kernel, cap=cap),
        grid=(bn, ne),
        in_specs=[pl.BlockSpec((None, None, rows, LANES), lambda b, e: (b, e, 0, 0)),
                  pl.BlockSpec((None, None, cap, 1), lambda b, e: (b, e, 0, 0)),
                  pl.BlockSpec((None, D_MODEL, D_FF), lambda b, e: (e, 0, 0)),
                  pl.BlockSpec((None, D_MODEL, D_FF), lambda b, e: (e, 0, 0)),
                  pl.BlockSpec((None, D_FF, D_MODEL), lambda b, e: (e, 0, 0))],
        out_specs=pl.BlockSpec((None, None, cap * ROW_CHUNKS, LANES), lambda b, e: (b, e, 0, 0)),
        out_shape=jax.ShapeDtypeStruct((bn, ne, cap * ROW_CHUNKS, LANES), F32),
        compiler_params=_params(("arbitrary", "arbitrary"), VMEM_LIMIT_BYTES),
        name="experts",
    )(xg, gate, wg, wu, wd)


SCATTER_UNROLL = 8


def _combine_kernel(rows_sref, res_ref, x1_ref, mod_ref, ln_ref, o_ref, acc_ref, *, cap, ne):
    b = pl.program_id(0)
    step = pl.program_id(1)

    @pl.when(step == 0)
    def _init():
        acc_ref[...] = jnp.zeros_like(acc_ref)

    @pl.when(step < ne)
    def _scatter():
        base = (b * ne + step) * cap

        def body(i, carry):
            first = i * SCATTER_UNROLL
            toks = [_tile_at(rows_sref[base + first + k]) for k in range(SCATTER_UNROLL)]
            sums = [acc_ref[toks[k], :] + res_ref[_row_tile(first + k), :] for k in range(SCATTER_UNROLL)]
            for k in range(SCATTER_UNROLL):
                acc_ref[toks[k], :] = sums[k]
            return carry

        lax.fori_loop(0, cap // SCATTER_UNROLL, body, 0)

    @pl.when(step >= ne)
    def _finish():
        moe = _load_row_tiles(acc_ref, (step - ne) * FIN_TILE, FIN_TILE)
        y = ALPHA * x1_ref[...] + mod_ref[5:6, :] * moe
        o_ref[...] = _layer_norm(y) * ln_ref[0:1, :] + ln_ref[1:2, :]


def _combine(tile_rows, res, x1, mod, ln2, cap):
    bn, length, _ = x1.shape
    ne = res.shape[1]
    assert cap % SCATTER_UNROLL == 0
    tok = lambda b, s, idx: (b, jnp.maximum(s - ne, 0), 0)
    return pl.pallas_call(
        functools.partial(_combine_kernel, cap=cap, ne=ne),
        grid_spec=pltpu.PrefetchScalarGridSpec(
            num_scalar_prefetch=1,
            grid=(bn, ne + length // FIN_TILE),
            in_specs=[pl.BlockSpec((None, None, cap * ROW_CHUNKS, LANES),
                                   lambda b, s, idx: (b, jnp.minimum(s, ne - 1), 0, 0)),
                      pl.BlockSpec((None, FIN_TILE, D_MODEL), tok),
                      pl.BlockSpec((None, 6, D_MODEL), lambda b, s, idx: (b, 0, 0)),
                      pl.BlockSpec((2, D_MODEL), lambda b, s, idx: (0, 0))],
            out_specs=pl.BlockSpec((None, FIN_TILE, D_MODEL), tok),
            scratch_shapes=[pltpu.VMEM((length * ROW_CHUNKS, LANES), F32)]),
        out_shape=jax.ShapeDtypeStruct((bn, length, D_MODEL), F32),
        compiler_params=_params(("arbitrary", "arbitrary"), VMEM_LIMIT_BYTES),
        name="combine",
    )(tile_rows, res, x1, mod, ln2)


def kernel(x, c, ctx, c_ctx, w_ada, b_ada, w_in, gm_ws, gm_bs, s5_a_re, s5_a_im, s5_log_step, s5_b_re, s5_b_im,
           s5_c_re, s5_c_im, s5_d, s5_w_glu, s5_b_glu, w_out, ln1_g, ln1_b, w_router, moe_w_gate, moe_w_up,
           moe_w_down, ln2_g, ln2_b):
    assert w_ada.shape[0] == DEPTH == 1
    bn, length, _ = x.shape
    assert bn == SCAN_HALF and ctx.shape[1] == SCAN_T and length % TOK_TILE == 0
    cap = CAPACITY_FACTOR * length // N_EXPERTS

    cc = jnp.zeros((SUBLANES, D_MODEL), F32).at[:bn].set(c).at[bn].set(c_ctx)
    mod = _adaln(cc, w_ada[0], b_ada[0][None, :])
    mod_lat = mod[:bn].reshape(bn, 6, D_MODEL)
    mod_ctx = mod[bn:bn + 1].reshape(1, 6, D_MODEL)

    w_in_b = w_in[0].astype(BF16)
    zs, gm, wg_b = _inproj(x, mod_lat, w_in_b, gm_ws[0].astype(BF16), gm_bs[0][:, :, None], moe_w_gate[0])
    zc = _ctxproj(ctx, mod_ctx, w_in_b)

    bmat, cmat, tab = _s5_tables(s5_a_re[0], s5_a_im[0], s5_log_step[0], s5_b_re[0], s5_b_im[0],
                                 s5_c_re[0], s5_c_im[0])
    yf, yb, wd_b = _s5scan(zs, zc, bmat, cmat, tab, moe_w_down[0])

    w_rt = w_router[0].T
    w_rt_hi = w_rt.astype(BF16)
    w_rt_parts = jnp.stack([w_rt_hi, (w_rt - w_rt_hi.astype(F32)).astype(BF16)])
    x1, hp, aff = _outproj(x, zs, yf, yb, gm, mod_lat, s5_d[0].reshape(1, S5_W), s5_w_glu[0].astype(BF16),
                           s5_b_glu[0][None, :], w_out[0].astype(BF16), jnp.stack([ln1_g[0], ln1_b[0]]),
                           w_rt_parts)

    aff4 = aff.reshape(bn, N_EXPERTS, length // LANES, LANES)
    key4, offs4 = _route(aff4, cap)
    row_offs = offs4[..., 0].astype(jnp.int32)
    rows5 = (bn, N_EXPERTS, length // LANES, 1, LANES)
    idx, gate, wu_b = _compact(row_offs.reshape(-1), key4.reshape(rows5), aff4.reshape(rows5), cap, moe_w_up[0])
    tile_rows = idx.reshape(-1) * ROW_CHUNKS

    xg = _gather(tile_rows, hp, N_EXPERTS, cap)
    res = _experts(xg, gate, wg_b, wu_b, wd_b)

    return _combine(tile_rows, res, x1, mod_lat, jnp.stack([ln2_g[0], ln2_b[0]]), cap)
```

```python
import functools

import jax
import jax.numpy as jnp
from jax import lax
from jax.experimental import pallas as pl
from jax.experimental.pallas import tpu as pltpu

D_MODEL = 1024
CHUNK = 128
S5_W = D_MODEL // 4
GM_W = D_MODEL - S5_W
GM_HEAD_DIM = 128
GM_HEADS = GM_W // GM_HEAD_DIM
S5_GROUP = 16
S5_GROUPS = S5_W // S5_GROUP
S5_STATE = 64
S5_STATES = S5_GROUPS * S5_STATE
IN_COLS = S5_W + 2 * GM_W
N_EXPERTS = 16
CAPACITY_FACTOR = 2
D_FF = 2816
DEPTH = 1
ALPHA = (2.0 * DEPTH) ** 0.25
LN_EPS = 1e-6

F32 = jnp.float32
BF16 = jnp.bfloat16
HIGHEST = lax.Precision.HIGHEST

LANES = 128
SUBLANES = 8
ROW_CHUNKS = D_MODEL // LANES
VMEM_LIMIT_BYTES = 58 * 1024 * 1024

TOK_TILE = 1024
FIN_TILE = 512
SCAN_T = 256
SCAN_LANES = 512
SCAN_HALF = SUBLANES // 2
ROW_TILE = 256
COMPACT_WIN = CHUNK + SUBLANES
F32_TINY = 2.0 ** -126
BRACKET_GEO_STEPS = 8
BRACKET_LIN_STEPS = 26


def _layer_norm(x):
    mu = jnp.mean(x, axis=-1, keepdims=True)
    xc = x - mu
    var = jnp.mean(xc * xc, axis=-1, keepdims=True)
    return xc * lax.rsqrt(var + LN_EPS)


def _params(semantics, vmem=None):
    return pltpu.CompilerParams(dimension_semantics=semantics, vmem_limit_bytes=vmem)


def _store_row_tiles(ref, first_token, x):
    n = x.shape[0]
    for s in range(ROW_CHUNKS):
        ref[pl.ds(first_token * ROW_CHUNKS + s, n, stride=ROW_CHUNKS), :] = x[:, s * LANES:(s + 1) * LANES]


def _load_row_tiles(ref, first_token, n):
    return jnp.concatenate(
        [ref[pl.ds(first_token * ROW_CHUNKS + s, n, stride=ROW_CHUNKS), :] for s in range(ROW_CHUNKS)], axis=1)


def _row_tile(token):
    return pl.ds(pl.multiple_of(token * ROW_CHUNKS, ROW_CHUNKS), ROW_CHUNKS)


def _tile_at(first_row):
    return pl.ds(pl.multiple_of(first_row, ROW_CHUNKS), ROW_CHUNKS)


def _cast_spec(w, n_steps, step_of):
    ne, rows, cols = w.shape
    per_expert = n_steps // ne
    assert per_expert * ne == n_steps and rows % (per_expert * 2 * SUBLANES) == 0
    return pl.BlockSpec((None, rows // per_expert, cols),
                        lambda *ids: (step_of(*ids) // per_expert, step_of(*ids) % per_expert, 0))


def _cast_block(w_ref, o_ref):
    o_ref[...] = w_ref[...].astype(BF16)


def _adaln_kernel(c_ref, w_ref, b_ref, o_ref):
    c = c_ref[...]
    a = c * jax.nn.sigmoid(c)
    o_ref[...] = jnp.dot(a, w_ref[...], precision=HIGHEST, preferred_element_type=F32) + b_ref[...]


def _adaln(cc, w_ada, b_ada):
    n = w_ada.shape[1] // D_MODEL
    return pl.pallas_call(
        _adaln_kernel,
        grid=(n,),
        in_specs=[pl.BlockSpec((SUBLANES, D_MODEL), lambda j: (0, 0)),
                  pl.BlockSpec((D_MODEL, D_MODEL), lambda j: (0, j)),
                  pl.BlockSpec((1, D_MODEL), lambda j: (0, j))],
        out_specs=pl.BlockSpec((SUBLANES, D_MODEL), lambda j: (0, j)),
        out_shape=jax.ShapeDtypeStruct((SUBLANES, w_ada.shape[1]), F32),
        compiler_params=_params(("arbitrary",)),
        name="adaln",
    )(cc, w_ada, b_ada)


def _inproj_kernel(x_ref, mod_ref, w_ref, ws_ref, bs_ref, wcast_ref, zs_ref, gm_ref, wcast_out_ref):
    _cast_block(wcast_ref, wcast_out_ref)
    h = _layer_norm(x_ref[...]) * (1.0 + mod_ref[1:2, :]) + mod_ref[0:1, :]
    z = jnp.dot(h.astype(BF16), w_ref[...], preferred_element_type=F32)
    zs_ref[...] = z[:, :S5_W]
    for g in range(GM_HEADS):
        lo = S5_W + g * GM_HEAD_DIM
        u = jax.nn.gelu(z[:, lo:lo + GM_HEAD_DIM])
        v = jax.nn.gelu(z[:, lo + GM_W:lo + GM_W + GM_HEAD_DIM])
        vn = _layer_norm(v).astype(BF16)
        wsg = ws_ref[g]
        bsg = bs_ref[g]
        for c in range(TOK_TILE // CHUNK):
            rows = slice(c * CHUNK, (c + 1) * CHUNK)
            mixed = jnp.dot(wsg, vn[rows, :], preferred_element_type=F32) + bsg
            gm_ref[rows, g * GM_HEAD_DIM:(g + 1) * GM_HEAD_DIM] = (u[rows, :] * mixed).astype(BF16)


def _inproj(x, mod, w_in, gm_ws, gm_bs, w_cast):
    bn, length, _ = x.shape
    nt = length // TOK_TILE
    cast = _cast_spec(w_cast, bn * nt, lambda b, i: b * nt + i)
    return pl.pallas_call(
        _inproj_kernel,
        grid=(bn, nt),
        in_specs=[pl.BlockSpec((None, TOK_TILE, D_MODEL), lambda b, i: (b, i, 0)),
                  pl.BlockSpec((None, 6, D_MODEL), lambda b, i: (b, 0, 0)),
                  pl.BlockSpec((D_MODEL, IN_COLS), lambda b, i: (0, 0)),
                  pl.BlockSpec((GM_HEADS, CHUNK, CHUNK), lambda b, i: (0, 0, 0)),
                  pl.BlockSpec((GM_HEADS, CHUNK, 1), lambda b, i: (0, 0, 0)),
                  cast],
        out_specs=[pl.BlockSpec((TOK_TILE, S5_W), lambda b, i: (i, b)),
                   pl.BlockSpec((None, TOK_TILE, GM_W), lambda b, i: (b, i, 0)),
                   cast],
        out_shape=[jax.ShapeDtypeStruct((length, bn * S5_W), F32),
                   jax.ShapeDtypeStruct((bn, length, GM_W), BF16),
                   jax.ShapeDtypeStruct(w_cast.shape, BF16)],
        compiler_params=_params(("arbitrary", "arbitrary"), VMEM_LIMIT_BYTES),
        name="inproj",
    )(x, mod, w_in, gm_ws, gm_bs, w_cast)


def _ctxproj_kernel(x_ref, mod_ref, w_ref, o_ref):
    h = _layer_norm(x_ref[...]) * (1.0 + mod_ref[1:2, :]) + mod_ref[0:1, :]
    o_ref[...] = jnp.dot(h.astype(BF16), w_ref[...], preferred_element_type=F32)


def _ctxproj(ctx, mod_ctx, w_in):
    bn, clen, _ = ctx.shape
    return pl.pallas_call(
        _ctxproj_kernel,
        grid=(bn,),
        in_specs=[pl.BlockSpec((None, clen, D_MODEL), lambda b: (b, 0, 0)),
                  pl.BlockSpec((None, 6, D_MODEL), lambda b: (0, 0, 0)),
                  pl.BlockSpec((D_MODEL, S5_W), lambda b: (0, 0))],
        out_specs=pl.BlockSpec((clen, S5_W), lambda b: (0, b)),
        out_shape=jax.ShapeDtypeStruct((clen, bn * S5_W), F32),
        compiler_params=_params(("arbitrary",)),
        name="ctxproj",
    )(ctx, mod_ctx, w_in)


def _s5_tables(a_re, a_im, log_step, b_re, b_im, c_re, c_im):
    step = jnp.exp(log_step)[..., None]
    mag = jnp.exp(a_re * step)
    lb_re = mag * jnp.cos(a_im * step)
    lb_im = mag * jnp.sin(a_im * step)
    den = a_re * a_re + a_im * a_im
    q_re = ((lb_re - 1.0) * a_re + lb_im * a_im) / den
    q_im = (lb_im * a_re - (lb_re - 1.0) * a_im) / den
    bb_re = q_re[..., None] * b_re - q_im[..., None] * b_im
    bb_im = q_re[..., None] * b_im + q_im[..., None] * b_re
    eye = jnp.eye(S5_GROUPS, dtype=F32)

    def in_block(m):
        return jnp.einsum('dgph,gk->dghkp', m, eye).reshape(2, S5_W, S5_STATES)

    def out_block(m):
        return jnp.einsum('dghp,gk->dgpkh', m, eye).reshape(2, S5_STATES, S5_W)

    bmat = jnp.concatenate([in_block(bb_re), in_block(bb_im)], axis=2).astype(BF16)
    cmat = jnp.concatenate([out_block(c_re), out_block(-c_im)], axis=1).astype(BF16)
    lam = jnp.concatenate([lb_re.reshape(2, 1, S5_STATES), lb_im.reshape(2, 1, S5_STATES)], axis=2)
    tab = jnp.concatenate([jnp.repeat(lam[0], SCAN_HALF, axis=0), jnp.repeat(lam[1], SCAN_HALF, axis=0)], axis=0)
    return bmat, cmat, tab


def _scan_tiles(df_ref, db_ref, s_ref, tab_ref):
    n_tiles = df_ref.shape[1] // SUBLANES
    slabs_per_block = SCAN_LANES // LANES
    top = lax.broadcasted_iota(jnp.int32, (SUBLANES, SCAN_LANES), 0) < SCAN_HALF

    def advance(l_re, l_im, s_re, s_im, x_re, x_im):
        return x_re + l_re * s_re - l_im * s_im, x_im + l_re * s_im + l_im * s_re

    def swap(x):
        return pltpu.roll(x, SCAN_HALF, 0)

    for lb in range(S5_STATES // SCAN_LANES):
        re_cols = pl.ds(lb * SCAN_LANES, SCAN_LANES)
        im_cols = pl.ds(S5_STATES + lb * SCAN_LANES, SCAN_LANES)
        re_slabs = [lb * slabs_per_block + k for k in range(slabs_per_block)]
        im_slabs = [S5_STATES // LANES + s for s in re_slabs]
        l_re, l_im = tab_ref[:, re_cols], tab_ref[:, im_cols]

        def body(i, s, re_slabs=re_slabs, im_slabs=im_slabs, l_re=l_re, l_im=l_im):
            rows_f = pl.ds(pl.multiple_of(i * SUBLANES, SUBLANES), SUBLANES)
            rows_b = pl.ds(pl.multiple_of((n_tiles - 1 - i) * SUBLANES, SUBLANES), SUBLANES)
            out = []
            for slabs, s_part in ((re_slabs, 0), (im_slabs, 1)):
                f = jnp.concatenate([df_ref[k, rows_f, :] for k in slabs], axis=1)
                b = jnp.concatenate([db_ref[k, rows_b, :] for k in slabs], axis=1)
                out.append((jnp.where(top, f, b), swap(jnp.where(top, b, f))))
            (x0_re, x1_re), (x0_im, x1_im) = out
            y0_re, y0_im = advance(l_re, l_im, s[0], s[1], x0_re, x0_im)
            y1_re, y1_im = advance(l_re, l_im, y0_re, y0_im, x1_re, x1_im)
            for slabs, y0, y1 in ((re_slabs, y0_re, y1_re), (im_slabs, y0_im, y1_im)):
                r1 = swap(y1)
                f_new = jnp.where(top, y0, r1)
                b_new = jnp.where(top, r1, y0)
                for n, k in enumerate(slabs):
                    df_ref[k, rows_f, :] = f_new[:, n * LANES:(n + 1) * LANES]
                    db_ref[k, rows_b, :] = b_new[:, n * LANES:(n + 1) * LANES]
            return y1_re, y1_im

        s_re, s_im = lax.fori_loop(0, n_tiles, body, (s_ref[:, re_cols], s_ref[:, im_cols]))
        s_ref[:, re_cols] = s_re
        s_ref[:, im_cols] = s_im


def _s5scan_kernel(uf_ref, ub_ref, uc_ref, bmat_ref, cmat_ref, tab_ref, wcast_ref, yf_ref, yb_ref, wcast_out_ref,
                   df_ref, db_ref, s_ref):
    _cast_block(wcast_ref, wcast_out_ref)
    j = pl.program_id(0)
    n_slabs = 2 * S5_STATES // LANES

    def batch_rows(b):
        return pl.ds(b, SCAN_T, stride=SCAN_HALF)

    def drive(u_ref, d_ref, direction):
        u = jnp.concatenate([u_ref[:, b * S5_W:(b + 1) * S5_W] for b in range(SCAN_HALF)], axis=0)
        d = jnp.dot(u.astype(BF16), bmat_ref[direction], preferred_element_type=F32)
        for b in range(SCAN_HALF):
            for k in range(n_slabs):
                d_ref[k, batch_rows(b), :] = d[b * SCAN_T:(b + 1) * SCAN_T, k * LANES:(k + 1) * LANES]

    def readout(d_ref, y_ref, direction):
        s = jnp.concatenate(
            [jnp.concatenate([d_ref[k, batch_rows(b), :] for k in range(n_slabs)], axis=1)
             for b in range(SCAN_HALF)], axis=0)
        y = jnp.dot(s.astype(BF16), cmat_ref[direction], preferred_element_type=F32)
        for b in range(SCAN_HALF):
            y_ref[:, b * S5_W:(b + 1) * S5_W] = y[b * SCAN_T:(b + 1) * SCAN_T, :]

    @pl.when(j == 0)
    def _context():
        s_ref[...] = jnp.zeros_like(s_ref)
        drive(uc_ref, df_ref, 0)
        drive(uc_ref, db_ref, 1)
        _scan_tiles(df_ref, db_ref, s_ref, tab_ref)

    drive(uf_ref, df_ref, 0)
    drive(ub_ref, db_ref, 1)
    _scan_tiles(df_ref, db_ref, s_ref, tab_ref)
    readout(df_ref, yf_ref, 0)
    readout(db_ref, yb_ref, 1)


def _s5scan(u_lat, u_ctx, bmat, cmat, tab, w_cast):
    length, cols = u_lat.shape
    n = length // SCAN_T
    assert u_ctx.shape == (SCAN_T, cols) and cols == SCAN_HALF * S5_W
    slab = pltpu.VMEM((2 * S5_STATES // LANES, SCAN_T * SCAN_HALF, LANES), F32)
    cast = _cast_spec(w_cast, n, lambda j: j)
    return pl.pallas_call(
        _s5scan_kernel,
        grid=(n,),
        in_specs=[pl.BlockSpec((SCAN_T, cols), lambda j: (j, 0)),
                  pl.BlockSpec((SCAN_T, cols), lambda j: (n - 1 - j, 0)),
                  pl.BlockSpec((SCAN_T, cols), lambda j: (0, 0)),
                  pl.BlockSpec((2, S5_W, 2 * S5_STATES), lambda j: (0, 0, 0)),
                  pl.BlockSpec((2, 2 * S5_STATES, S5_W), lambda j: (0, 0, 0)),
                  pl.BlockSpec((SUBLANES, 2 * S5_STATES), lambda j: (0, 0)),
                  cast],
        out_specs=[pl.BlockSpec((SCAN_T, cols), lambda j: (j, 0)),
                   pl.BlockSpec((SCAN_T, cols), lambda j: (n - 1 - j, 0)),
                   cast],
        out_shape=[jax.ShapeDtypeStruct((length, cols), F32)] * 2 + [jax.ShapeDtypeStruct(w_cast.shape, BF16)],
        scratch_shapes=[slab, slab, pltpu.VMEM((SUBLANES, 2 * S5_STATES), F32)],
        compiler_params=_params(("arbitrary",), VMEM_LIMIT_BYTES),
        name="s5scan",
    )(u_lat, u_lat, u_ctx, bmat, cmat, tab, w_cast)


def _outproj_kernel(x_ref, zs_ref, yf_ref, yb_ref, gm_ref, mod_ref, d_ref, wglu_ref, bglu_ref, wout_ref,
                    ln_ref, wr_ref, x1_ref, hp_ref, aff_ref):
    y = d_ref[...] * zs_ref[...] + yf_ref[...] + yb_ref[...]
    g = jax.nn.gelu(y)
    s5 = g * jax.nn.sigmoid(jnp.dot(g.astype(BF16), wglu_ref[...], preferred_element_type=F32) + bglu_ref[...])
    mix = (jnp.dot(s5.astype(BF16), wout_ref[0:S5_W, :], preferred_element_type=F32)
           + jnp.dot(gm_ref[...], wout_ref[S5_W:, :], preferred_element_type=F32))
    x1 = _layer_norm(ALPHA * x_ref[...] + mod_ref[2:3, :] * mix) * ln_ref[0:1, :] + ln_ref[1:2, :]
    x1_ref[...] = x1
    h = _layer_norm(x1) * (1.0 + mod_ref[4:5, :]) + mod_ref[3:4, :]
    h_hi = h.astype(BF16)
    h_lo = (h - h_hi.astype(F32)).astype(BF16)
    contract_last = (((1,), (1,)), ((), ()))
    logits = (lax.dot_general(wr_ref[0], h_hi, contract_last, preferred_element_type=F32)
              + lax.dot_general(wr_ref[0], h_lo, contract_last, preferred_element_type=F32)
              + lax.dot_general(wr_ref[1], h_hi, contract_last, preferred_element_type=F32))
    ex = jnp.exp(logits - jnp.max(logits, axis=0, keepdims=True))
    aff_ref[...] = ex / jnp.sum(ex, axis=0, keepdims=True)
    _store_row_tiles(hp_ref, 0, h)


def _outproj(x, zs, yf, yb, gm, mod, d, w_glu, b_glu, w_out, ln1, w_rt):
    bn, length, _ = x.shape
    tok = lambda b, i: (b, i, 0)
    s5 = lambda b, i: (i, b)
    const2 = lambda b, i: (0, 0)
    return pl.pallas_call(
        _outproj_kernel,
        grid=(bn, length // TOK_TILE),
        in_specs=[pl.BlockSpec((None, TOK_TILE, D_MODEL), tok),
                  pl.BlockSpec((TOK_TILE, S5_W), s5),
                  pl.BlockSpec((TOK_TILE, S5_W), s5),
                  pl.BlockSpec((TOK_TILE, S5_W), s5),
                  pl.BlockSpec((None, TOK_TILE, GM_W), tok),
                  pl.BlockSpec((None, 6, D_MODEL), lambda b, i: (b, 0, 0)),
                  pl.BlockSpec((1, S5_W), const2),
                  pl.BlockSpec((S5_W, S5_W), const2),
                  pl.BlockSpec((1, S5_W), const2),
                  pl.BlockSpec((D_MODEL, D_MODEL), const2),
                  pl.BlockSpec((2, D_MODEL), const2),
                  pl.BlockSpec((2, N_EXPERTS, D_MODEL), lambda b, i: (0, 0, 0))],
        out_specs=[pl.BlockSpec((None, TOK_TILE, D_MODEL), tok),
                   pl.BlockSpec((None, TOK_TILE * ROW_CHUNKS, LANES), tok),
                   pl.BlockSpec((None, N_EXPERTS, TOK_TILE), lambda b, i: (b, 0, i))],
        out_shape=[jax.ShapeDtypeStruct((bn, length, D_MODEL), F32),
                   jax.ShapeDtypeStruct((bn, length * ROW_CHUNKS, LANES), F32),
                   jax.ShapeDtypeStruct((bn, N_EXPERTS, length), F32)],
        compiler_params=_params(("arbitrary", "arbitrary"), VMEM_LIMIT_BYTES),
        name="outproj",
    )(x, zs, yf, yb, gm, mod, d, w_glu, b_glu, w_out, ln1, w_rt)


def _route_kernel(aff_ref, key_ref, offs_ref, *, cap):
    aff = aff_ref[...]
    ne, nr, _ = aff.shape

    def count(mask):
        return jnp.sum(jnp.sum(mask.astype(F32), axis=2, keepdims=True), axis=1, keepdims=True)

    normal = count(aff >= F32_TINY) >= cap
    lo = jnp.where(normal, F32_TINY, 0.0)
    hi = jnp.where(normal, 2.0, F32_TINY)
    for it in range(BRACKET_GEO_STEPS + BRACKET_LIN_STEPS):
        mid = 0.5 * (lo + hi)
        if it < BRACKET_GEO_STEPS:
            mid = jnp.where(normal, jnp.sqrt(lo * hi), mid)
        ok = count(aff >= mid) >= cap
        lo = jnp.where(ok, mid, lo)
        hi = jnp.where(ok, hi, mid)
    thr = jnp.min(jnp.min(jnp.where(aff >= lo, aff, 2.0), axis=2, keepdims=True), axis=1, keepdims=True)
    above = aff > thr
    tied = aff == thr
    need = cap - count(above)

    kk = lax.broadcasted_iota(jnp.int32, (LANES, LANES), 0)
    ll = lax.broadcasted_iota(jnp.int32, (LANES, LANES), 1)
    upper = (kk <= ll).astype(BF16)
    ones = jnp.ones((LANES, LANES), BF16)
    ri = lax.broadcasted_iota(jnp.int32, (nr, nr), 0)
    rj = lax.broadcasted_iota(jnp.int32, (nr, nr), 1)
    lower = (rj < ri).astype(BF16)

    def exclusive_prefix(mask):
        m = mask.astype(BF16).reshape(ne * nr, LANES)
        in_row = jnp.dot(m, upper, preferred_element_type=F32).reshape(ne, nr, LANES)
        row_tot = jnp.dot(m, ones, preferred_element_type=F32).reshape(ne, nr, LANES)
        row_off = jnp.stack([jnp.dot(lower, row_tot[e].astype(BF16), preferred_element_type=F32)
                             for e in range(ne)])
        return in_row - mask.astype(F32) + row_off, row_off

    tie_rank, _ = exclusive_prefix(tied)
    sel = above | (tied & (tie_rank < need))
    pos, row_off = exclusive_prefix(sel)
    key_ref[...] = jnp.where(sel, pos, -1.0)
    offs_ref[...] = row_off


def _route(aff4, cap):
    bn, ne, nr, _ = aff4.shape
    spec = pl.BlockSpec((None, ne, nr, LANES), lambda b: (b, 0, 0, 0))
    return pl.pallas_call(
        functools.partial(_route_kernel, cap=cap),
        grid=(bn,),
        in_specs=[spec],
        out_specs=[spec, spec],
        out_shape=[jax.ShapeDtypeStruct(aff4.shape, F32)] * 2,
        compiler_params=_params(("arbitrary",), VMEM_LIMIT_BYTES),
        name="route",
    )(aff4)


def _compact_kernel(offs_sref, key_ref, aff_ref, wcast_ref, idx_ref, gate_ref, wcast_out_ref, acc_i, acc_g, *, cap):
    _cast_block(wcast_ref, wcast_out_ref)
    be = pl.program_id(0) * pl.num_programs(1) + pl.program_id(1)
    nr = key_ref.shape[0]
    acc_i[...] = jnp.zeros_like(acc_i)
    acc_g[...] = jnp.zeros_like(acc_g)
    slot = lax.broadcasted_iota(jnp.int32, (COMPACT_WIN, LANES), 0)
    lane = lax.broadcasted_iota(jnp.int32, (1, LANES), 1)

    def body(r, carry):
        off = offs_sref[be * nr + r]
        base = pl.multiple_of((off // SUBLANES) * SUBLANES, SUBLANES)
        win = pl.ds(base, COMPACT_WIN)
        hit = (slot + base) == key_ref[r].astype(jnp.int32)
        tok = (lane + r * LANES).astype(F32)
        acc_i[win, :] += jnp.where(hit, tok, 0.0)
        acc_g[win, :] += jnp.where(hit, aff_ref[r], 0.0)
        return carry

    lax.fori_loop(0, nr, body, 0)
    idx_ref[...] = jnp.sum(acc_i[0:cap, :], axis=1, keepdims=True).astype(jnp.int32)
    gate_ref[...] = jnp.sum(acc_g[0:cap, :], axis=1, keepdims=True)


def _compact(row_offs, key5, aff5, cap, w_cast):
    bn, ne, nr, _, _ = key5.shape
    spec_in = pl.BlockSpec((None, None, nr, 1, LANES), lambda b, e, offs: (b, e, 0, 0, 0))
    spec_out = pl.BlockSpec((None, None, cap, 1), lambda b, e, offs: (b, e, 0, 0))
    cast = _cast_spec(w_cast, bn * ne, lambda b, e, offs: b * ne + e)
    return pl.pallas_call(
        functools.partial(_compact_kernel, cap=cap),
        grid_spec=pltpu.PrefetchScalarGridSpec(
            num_scalar_prefetch=1,
            grid=(bn, ne),
            in_specs=[spec_in, spec_in, cast],
            out_specs=[spec_out, spec_out, cast],
            scratch_shapes=[pltpu.VMEM((cap + COMPACT_WIN, LANES), F32),
                            pltpu.VMEM((cap + COMPACT_WIN, LANES), F32)]),
        out_shape=[jax.ShapeDtypeStruct((bn, ne, cap, 1), jnp.int32),
                   jax.ShapeDtypeStruct((bn, ne, cap, 1), F32),
                   jax.ShapeDtypeStruct(w_cast.shape, BF16)],
        compiler_params=_params(("arbitrary", "arbitrary"), VMEM_LIMIT_BYTES),
        name="compact",
    )(row_offs, key5, aff5, w_cast)


def _experts_kernel(rows_sref, h_hbm, gate_ref, wg_ref, wu_ref, wd_ref, res_ref, xg_ref, sem, *, cap):
    ne = pl.num_programs(1)
    step = pl.program_id(0) * ne + pl.program_id(1)
    last = pl.num_programs(0) * ne - 1
    slot = step % 2
    nxt = jnp.minimum(step + 1, last)

    def start_row(c, of_step, into_slot):
        pltpu.make_async_copy(h_hbm.at[of_step // ne, _tile_at(rows_sref[of_step * cap + c]), :],
                              xg_ref.at[into_slot, _row_tile(c), :], sem.at[into_slot]).start()

    def wait_rows(of_slot):
        pltpu.make_async_copy(h_hbm.at[0, pl.ds(0, cap * ROW_CHUNKS), :], xg_ref.at[of_slot],
                              sem.at[of_slot]).wait()

    @pl.when(step == 0)
    def _first_rows():
        def body(c, carry):
            start_row(c, step, slot)
            return carry

        lax.fori_loop(0, cap, body, 0)

    wait_rows(slot)
    for m in range(cap // ROW_TILE):
        rows = slice(m * ROW_TILE, (m + 1) * ROW_TILE)
        for c in range(rows.start, rows.stop):
            start_row(c, nxt, 1 - slot)
        xr = _load_row_tiles(xg_ref.at[slot], rows.start, ROW_TILE).astype(BF16)
        hg = jnp.dot(xr, wg_ref[...], preferred_element_type=F32)
        hu = jnp.dot(xr, wu_ref[...], preferred_element_type=F32)
        hid = (hg * jax.nn.sigmoid(hg) * hu).astype(BF16)
        out = jnp.dot(hid, wd_ref[...], preferred_element_type=F32)
        _store_row_tiles(res_ref, rows.start, out * gate_ref[rows, :])

    @pl.when(step == last)
    def _drain():
        wait_rows(1 - slot)


def _experts(tile_rows, h, gate, wg, wu, wd, cap):
    bn = h.shape[0]
    ne = wg.shape[0]
    return pl.pallas_call(
        functools.partial(_experts_kernel, cap=cap),
        grid_spec=pltpu.PrefetchScalarGridSpec(
            num_scalar_prefetch=1,
            grid=(bn, ne),
            in_specs=[pl.BlockSpec(memory_space=pl.ANY),
                      pl.BlockSpec((None, None, cap, 1), lambda b, e, rows: (b, e, 0, 0)),
                      pl.BlockSpec((None, D_MODEL, D_FF), lambda b, e, rows: (e, 0, 0)),
                      pl.BlockSpec((None, D_MODEL, D_FF), lambda b, e, rows: (e, 0, 0)),
                      pl.BlockSpec((None, D_FF, D_MODEL), lambda b, e, rows: (e, 0, 0))],
            out_specs=pl.BlockSpec((None, None, cap * ROW_CHUNKS, LANES), lambda b, e, rows: (b, e, 0, 0)),
            scratch_shapes=[pltpu.VMEM((2, cap * ROW_CHUNKS, LANES), F32),
                            pltpu.SemaphoreType.DMA((2,))]),
        out_shape=jax.ShapeDtypeStruct((bn, ne, cap * ROW_CHUNKS, LANES), F32),
        compiler_params=_params(("arbitrary", "arbitrary"), VMEM_LIMIT_BYTES),
        name="experts",
    )(tile_rows, h, gate, wg, wu, wd)


SCATTER_UNROLL = 8


def _combine_kernel(rows_sref, res_ref, x1_ref, mod_ref, ln_ref, o_ref, acc_ref, *, cap, ne):
    b = pl.program_id(0)
    step = pl.program_id(1)

    @pl.when(step == 0)
    def _init():
        acc_ref[...] = jnp.zeros_like(acc_ref)

    @pl.when(step < ne)
    def _scatter():
        base = (b * ne + step) * cap

        def body(i, carry):
            first = i * SCATTER_UNROLL
            toks = [_tile_at(rows_sref[base + first + k]) for k in range(SCATTER_UNROLL)]
            sums = [acc_ref[toks[k], :] + res_ref[_row_tile(first + k), :] for k in range(SCATTER_UNROLL)]
            for k in range(SCATTER_UNROLL):
                acc_ref[toks[k], :] = sums[k]
            return carry

        lax.fori_loop(0, cap // SCATTER_UNROLL, body, 0)

    @pl.when(step >= ne)
    def _finish():
        moe = _load_row_tiles(acc_ref, (step - ne) * FIN_TILE, FIN_TILE)
        y = ALPHA * x1_ref[...] + mod_ref[5:6, :] * moe
        o_ref[...] = _layer_norm(y) * ln_ref[0:1, :] + ln_ref[1:2, :]


def _combine(tile_rows, res, x1, mod, ln2, cap):
    bn, length, _ = x1.shape
    ne = res.shape[1]
    assert cap % SCATTER_UNROLL == 0
    tok = lambda b, s, idx: (b, jnp.maximum(s - ne, 0), 0)
    return pl.pallas_call(
        functools.partial(_combine_kernel, cap=cap, ne=ne),
        grid_spec=pltpu.PrefetchScalarGridSpec(
            num_scalar_prefetch=1,
            grid=(bn, ne + length // FIN_TILE),
            in_specs=[pl.BlockSpec((None, None, cap * ROW_CHUNKS, LANES),
                                   lambda b, s, idx: (b, jnp.minimum(s, ne - 1), 0, 0)),
                      pl.BlockSpec((None, FIN_TILE, D_MODEL), tok),
                      pl.BlockSpec((None, 6, D_MODEL), lambda b, s, idx: (b, 0, 0)),
                      pl.BlockSpec((2, D_MODEL), lambda b, s, idx: (0, 0))],
            out_specs=pl.BlockSpec((None, FIN_TILE, D_MODEL), tok),
            scratch_shapes=[pltpu.VMEM((length * ROW_CHUNKS, LANES), F32)]),
        out_shape=jax.ShapeDtypeStruct((bn, length, D_MODEL), F32),
        compiler_params=_params(("arbitrary", "arbitrary"), VMEM_LIMIT_BYTES),
        name="combine",
    )(tile_rows, res, x1, mod, ln2)


def kernel(x, c, ctx, c_ctx, w_ada, b_ada, w_in, gm_ws, gm_bs, s5_a_re, s5_a_im, s5_log_step, s5_b_re, s5_b_im,
           s5_c_re, s5_c_im, s5_d, s5_w_glu, s5_b_glu, w_out, ln1_g, ln1_b, w_router, moe_w_gate, moe_w_up,
           moe_w_down, ln2_g, ln2_b):
    assert w_ada.shape[0] == DEPTH == 1
    bn, length, _ = x.shape
    assert bn == SCAN_HALF and ctx.shape[1] == SCAN_T and length % TOK_TILE == 0
    cap = CAPACITY_FACTOR * length // N_EXPERTS

    cc = jnp.zeros((SUBLANES, D_MODEL), F32).at[:bn].set(c).at[bn].set(c_ctx)
    mod = _adaln(cc, w_ada[0], b_ada[0][None, :])
    mod_lat = mod[:bn].reshape(bn, 6, D_MODEL)
    mod_ctx = mod[bn:bn + 1].reshape(1, 6, D_MODEL)

    w_in_b = w_in[0].astype(BF16)
    zs, gm, wg_b = _inproj(x, mod_lat, w_in_b, gm_ws[0].astype(BF16), gm_bs[0][:, :, None], moe_w_gate[0])
    zc = _ctxproj(ctx, mod_ctx, w_in_b)

    bmat, cmat, tab = _s5_tables(s5_a_re[0], s5_a_im[0], s5_log_step[0], s5_b_re[0], s5_b_im[0],
                                 s5_c_re[0], s5_c_im[0])
    yf, yb, wd_b = _s5scan(zs, zc, bmat, cmat, tab, moe_w_down[0])

    w_rt = w_router[0].T
    w_rt_hi = w_rt.astype(BF16)
    w_rt_parts = jnp.stack([w_rt_hi, (w_rt - w_rt_hi.astype(F32)).astype(BF16)])
    x1, hp, aff = _outproj(x, zs, yf, yb, gm, mod_lat, s5_d[0].reshape(1, S5_W), s5_w_glu[0].astype(BF16),
                           s5_b_glu[0][None, :], w_out[0].astype(BF16), jnp.stack([ln1_g[0], ln1_b[0]]),
                           w_rt_parts)

    aff4 = aff.reshape(bn, N_EXPERTS, length // LANES, LANES)
    key4, offs4 = _route(aff4, cap)
    row_offs = offs4[..., 0].astype(jnp.int32)
    rows5 = (bn, N_EXPERTS, length // LANES, 1, LANES)
    idx, gate, wu_b = _compact(row_offs.reshape(-1), key4.reshape(rows5), aff4.reshape(rows5), cap, moe_w_up[0])
    tile_rows = idx.reshape(-1) * ROW_CHUNKS

    res = _experts(tile_rows, hp, gate, wg_b, wu_b, wd_b, cap)

    return _combine(tile_rows, res, x1, mod_lat, jnp.stack([ln2_g[0], ln2_b[0]]), cap)
```

```python
import functools

import jax
import jax.numpy as jnp
from jax import lax
from jax.experimental import pallas as pl
from jax.experimental.pallas import tpu as pltpu

D_MODEL = 1024
CHUNK = 128
S5_W = D_MODEL // 4
GM_W = D_MODEL - S5_W
GM_HEAD_DIM = 128
GM_HEADS = GM_W // GM_HEAD_DIM
S5_GROUP = 16
S5_GROUPS = S5_W // S5_GROUP
S5_STATE = 64
S5_STATES = S5_GROUPS * S5_STATE
IN_COLS = S5_W + 2 * GM_W
N_EXPERTS = 16
CAPACITY_FACTOR = 2
D_FF = 2816
DEPTH = 1
ALPHA = (2.0 * DEPTH) ** 0.25
LN_EPS = 1e-6

F32 = jnp.float32
BF16 = jnp.bfloat16
HIGHEST = lax.Precision.HIGHEST

LANES = 128
SUBLANES = 8
ROW_CHUNKS = D_MODEL // LANES
VMEM_LIMIT_BYTES = 58 * 1024 * 1024

TOK_TILE = 1024
FIN_TILE = 512
SCAN_T = 256
SCAN_LANES = 512
SCAN_HALF = SUBLANES // 2
ROW_TILE = 256
COMPACT_STEP = 32
F32_TINY = 2.0 ** -126
BRACKET_GEO_STEPS = 8
BRACKET_LIN_STEPS = 26


def _layer_norm(x):
    mu = jnp.mean(x, axis=-1, keepdims=True)
    xc = x - mu
    var = jnp.mean(xc * xc, axis=-1, keepdims=True)
    return xc * lax.rsqrt(var + LN_EPS)


def _params(semantics, vmem=None):
    return pltpu.CompilerParams(dimension_semantics=semantics, vmem_limit_bytes=vmem)


def _store_row_tiles(ref, first_token, x):
    n = x.shape[0]
    for s in range(ROW_CHUNKS):
        ref[pl.ds(first_token * ROW_CHUNKS + s, n, stride=ROW_CHUNKS), :] = x[:, s * LANES:(s + 1) * LANES]


def _load_row_tiles(ref, first_token, n):
    return jnp.concatenate(
        [ref[pl.ds(first_token * ROW_CHUNKS + s, n, stride=ROW_CHUNKS), :] for s in range(ROW_CHUNKS)], axis=1)


def _row_tile(token):
    return pl.ds(pl.multiple_of(token * ROW_CHUNKS, ROW_CHUNKS), ROW_CHUNKS)


def _tile_at(first_row):
    return pl.ds(pl.multiple_of(first_row, ROW_CHUNKS), ROW_CHUNKS)


def _cast_spec(w, n_steps, step_of):
    ne, rows, cols = w.shape
    per_expert = n_steps // ne
    assert per_expert * ne == n_steps and rows % (per_expert * 2 * SUBLANES) == 0
    return pl.BlockSpec((None, rows // per_expert, cols),
                        lambda *ids: (step_of(*ids) // per_expert, step_of(*ids) % per_expert, 0))


def _cast_block(w_ref, o_ref):
    o_ref[...] = w_ref[...].astype(BF16)


def _adaln_kernel(c_ref, w_ref, b_ref, o_ref):
    c = c_ref[...]
    a = c * jax.nn.sigmoid(c)
    o_ref[...] = jnp.dot(a, w_ref[...], precision=HIGHEST, preferred_element_type=F32) + b_ref[...]


def _adaln(cc, w_ada, b_ada):
    n = w_ada.shape[1] // D_MODEL
    return pl.pallas_call(
        _adaln_kernel,
        grid=(n,),
        in_specs=[pl.BlockSpec((SUBLANES, D_MODEL), lambda j: (0, 0)),
                  pl.BlockSpec((D_MODEL, D_MODEL), lambda j: (0, j)),
                  pl.BlockSpec((1, D_MODEL), lambda j: (0, j))],
        out_specs=pl.BlockSpec((SUBLANES, D_MODEL), lambda j: (0, j)),
        out_shape=jax.ShapeDtypeStruct((SUBLANES, w_ada.shape[1]), F32),
        compiler_params=_params(("arbitrary",)),
        name="adaln",
    )(cc, w_ada, b_ada)


def _inproj_kernel(x_ref, mod_ref, w_ref, ws_ref, bs_ref, wcast_ref, zs_ref, gm_ref, wcast_out_ref):
    _cast_block(wcast_ref, wcast_out_ref)
    h = _layer_norm(x_ref[...]) * (1.0 + mod_ref[1:2, :]) + mod_ref[0:1, :]
    z = jnp.dot(h.astype(BF16), w_ref[...], preferred_element_type=F32)
    zs_ref[...] = z[:, :S5_W]
    for g in range(GM_HEADS):
        lo = S5_W + g * GM_HEAD_DIM
        u = jax.nn.gelu(z[:, lo:lo + GM_HEAD_DIM])
        v = jax.nn.gelu(z[:, lo + GM_W:lo + GM_W + GM_HEAD_DIM])
        vn = _layer_norm(v).astype(BF16)
        wsg = ws_ref[g]
        bsg = bs_ref[g]
        for c in range(TOK_TILE // CHUNK):
            rows = slice(c * CHUNK, (c + 1) * CHUNK)
            mixed = jnp.dot(wsg, vn[rows, :], preferred_element_type=F32) + bsg
            gm_ref[rows, g * GM_HEAD_DIM:(g + 1) * GM_HEAD_DIM] = (u[rows, :] * mixed).astype(BF16)


def _inproj(x, mod, w_in, gm_ws, gm_bs, w_cast):
    bn, length, _ = x.shape
    nt = length // TOK_TILE
    cast = _cast_spec(w_cast, bn * nt, lambda b, i: b * nt + i)
    return pl.pallas_call(
        _inproj_kernel,
        grid=(bn, nt),
        in_specs=[pl.BlockSpec((None, TOK_TILE, D_MODEL), lambda b, i: (b, i, 0)),
                  pl.BlockSpec((None, 6, D_MODEL), lambda b, i: (b, 0, 0)),
                  pl.BlockSpec((D_MODEL, IN_COLS), lambda b, i: (0, 0)),
                  pl.BlockSpec((GM_HEADS, CHUNK, CHUNK), lambda b, i: (0, 0, 0)),
                  pl.BlockSpec((GM_HEADS, CHUNK, 1), lambda b, i: (0, 0, 0)),
                  cast],
        out_specs=[pl.BlockSpec((TOK_TILE, S5_W), lambda b, i: (i, b)),
                   pl.BlockSpec((None, TOK_TILE, GM_W), lambda b, i: (b, i, 0)),
                   cast],
        out_shape=[jax.ShapeDtypeStruct((length, bn * S5_W), F32),
                   jax.ShapeDtypeStruct((bn, length, GM_W), BF16),
                   jax.ShapeDtypeStruct(w_cast.shape, BF16)],
        compiler_params=_params(("arbitrary", "arbitrary"), VMEM_LIMIT_BYTES),
        name="inproj",
    )(x, mod, w_in, gm_ws, gm_bs, w_cast)


def _ctxproj_kernel(x_ref, mod_ref, w_ref, o_ref):
    h = _layer_norm(x_ref[...]) * (1.0 + mod_ref[1:2, :]) + mod_ref[0:1, :]
    o_ref[...] = jnp.dot(h.astype(BF16), w_ref[...], preferred_element_type=F32)


def _ctxproj(ctx, mod_ctx, w_in):
    bn, clen, _ = ctx.shape
    return pl.pallas_call(
        _ctxproj_kernel,
        grid=(bn,),
        in_specs=[pl.BlockSpec((None, clen, D_MODEL), lambda b: (b, 0, 0)),
                  pl.BlockSpec((None, 6, D_MODEL), lambda b: (0, 0, 0)),
                  pl.BlockSpec((D_MODEL, S5_W), lambda b: (0, 0))],
        out_specs=pl.BlockSpec((clen, S5_W), lambda b: (0, b)),
        out_shape=jax.ShapeDtypeStruct((clen, bn * S5_W), F32),
        compiler_params=_params(("arbitrary",)),
        name="ctxproj",
    )(ctx, mod_ctx, w_in)


def _s5_tables(a_re, a_im, log_step, b_re, b_im, c_re, c_im):
    step = jnp.exp(log_step)[..., None]
    mag = jnp.exp(a_re * step)
    lb_re = mag * jnp.cos(a_im * step)
    lb_im = mag * jnp.sin(a_im * step)
    den = a_re * a_re + a_im * a_im
    q_re = ((lb_re - 1.0) * a_re + lb_im * a_im) / den
    q_im = (lb_im * a_re - (lb_re - 1.0) * a_im) / den
    bb_re = q_re[..., None] * b_re - q_im[..., None] * b_im
    bb_im = q_re[..., None] * b_im + q_im[..., None] * b_re
    eye = jnp.eye(S5_GROUPS, dtype=F32)

    def in_block(m):
        return jnp.einsum('dgph,gk->dghkp', m, eye).reshape(2, S5_W, S5_STATES)

    def out_block(m):
        return jnp.einsum('dghp,gk->dgpkh', m, eye).reshape(2, S5_STATES, S5_W)

    bmat = jnp.concatenate([in_block(bb_re), in_block(bb_im)], axis=2).astype(BF16)
    cmat = jnp.concatenate([out_block(c_re), out_block(-c_im)], axis=1).astype(BF16)
    lam = jnp.concatenate([lb_re.reshape(2, 1, S5_STATES), lb_im.reshape(2, 1, S5_STATES)], axis=2)
    tab = jnp.concatenate([jnp.repeat(lam[0], SCAN_HALF, axis=0), jnp.repeat(lam[1], SCAN_HALF, axis=0)], axis=0)
    return bmat, cmat, tab


def _scan_tiles(df_ref, db_ref, s_ref, tab_ref):
    n_tiles = df_ref.shape[1] // SUBLANES
    slabs_per_block = SCAN_LANES // LANES
    top = lax.broadcasted_iota(jnp.int32, (SUBLANES, SCAN_LANES), 0) < SCAN_HALF

    def advance(l_re, l_im, s_re, s_im, x_re, x_im):
        return x_re + l_re * s_re - l_im * s_im, x_im + l_re * s_im + l_im * s_re

    def swap(x):
        return pltpu.roll(x, SCAN_HALF, 0)

    for lb in range(S5_STATES // SCAN_LANES):
        re_cols = pl.ds(lb * SCAN_LANES, SCAN_LANES)
        im_cols = pl.ds(S5_STATES + lb * SCAN_LANES, SCAN_LANES)
        re_slabs = [lb * slabs_per_block + k for k in range(slabs_per_block)]
        im_slabs = [S5_STATES // LANES + s for s in re_slabs]
        l_re, l_im = tab_ref[:, re_cols], tab_ref[:, im_cols]

        def body(i, s, re_slabs=re_slabs, im_slabs=im_slabs, l_re=l_re, l_im=l_im):
            rows_f = pl.ds(pl.multiple_of(i * SUBLANES, SUBLANES), SUBLANES)
            rows_b = pl.ds(pl.multiple_of((n_tiles - 1 - i) * SUBLANES, SUBLANES), SUBLANES)
            out = []
            for slabs, s_part in ((re_slabs, 0), (im_slabs, 1)):
                f = jnp.concatenate([df_ref[k, rows_f, :] for k in slabs], axis=1)
                b = jnp.concatenate([db_ref[k, rows_b, :] for k in slabs], axis=1)
                out.append((jnp.where(top, f, b), swap(jnp.where(top, b, f))))
            (x0_re, x1_re), (x0_im, x1_im) = out
            y0_re, y0_im = advance(l_re, l_im, s[0], s[1], x0_re, x0_im)
            y1_re, y1_im = advance(l_re, l_im, y0_re, y0_im, x1_re, x1_im)
            for slabs, y0, y1 in ((re_slabs, y0_re, y1_re), (im_slabs, y0_im, y1_im)):
                r1 = swap(y1)
                f_new = jnp.where(top, y0, r1)
                b_new = jnp.where(top, r1, y0)
                for n, k in enumerate(slabs):
                    df_ref[k, rows_f, :] = f_new[:, n * LANES:(n + 1) * LANES]
                    db_ref[k, rows_b, :] = b_new[:, n * LANES:(n + 1) * LANES]
            return y1_re, y1_im

        s_re, s_im = lax.fori_loop(0, n_tiles, body, (s_ref[:, re_cols], s_ref[:, im_cols]))
        s_ref[:, re_cols] = s_re
        s_ref[:, im_cols] = s_im


def _s5scan_kernel(uf_ref, ub_ref, uc_ref, bmat_ref, cmat_ref, tab_ref, wcast_ref, yf_ref, yb_ref, wcast_out_ref,
                   df_ref, db_ref, s_ref):
    _cast_block(wcast_ref, wcast_out_ref)
    j = pl.program_id(0)
    n_slabs = 2 * S5_STATES // LANES

    def batch_rows(b):
        return pl.ds(b, SCAN_T, stride=SCAN_HALF)

    def drive(u_ref, d_ref, direction):
        u = jnp.concatenate([u_ref[:, b * S5_W:(b + 1) * S5_W] for b in range(SCAN_HALF)], axis=0)
        d = jnp.dot(u.astype(BF16), bmat_ref[direction], preferred_element_type=F32)
        for b in range(SCAN_HALF):
            for k in range(n_slabs):
                d_ref[k, batch_rows(b), :] = d[b * SCAN_T:(b + 1) * SCAN_T, k * LANES:(k + 1) * LANES]

    def readout(d_ref, y_ref, direction):
        s = jnp.concatenate(
            [jnp.concatenate([d_ref[k, batch_rows(b), :] for k in range(n_slabs)], axis=1)
             for b in range(SCAN_HALF)], axis=0)
        y = jnp.dot(s.astype(BF16), cmat_ref[direction], preferred_element_type=F32)
        for b in range(SCAN_HALF):
            y_ref[:, b * S5_W:(b + 1) * S5_W] = y[b * SCAN_T:(b + 1) * SCAN_T, :]

    @pl.when(j == 0)
    def _context():
        s_ref[...] = jnp.zeros_like(s_ref)
        drive(uc_ref, df_ref, 0)
        drive(uc_ref, db_ref, 1)
        _scan_tiles(df_ref, db_ref, s_ref, tab_ref)

    drive(uf_ref, df_ref, 0)
    drive(ub_ref, db_ref, 1)
    _scan_tiles(df_ref, db_ref, s_ref, tab_ref)
    readout(df_ref, yf_ref, 0)
    readout(db_ref, yb_ref, 1)


def _s5scan(u_lat, u_ctx, bmat, cmat, tab, w_cast):
    length, cols = u_lat.shape
    n = length // SCAN_T
    assert u_ctx.shape == (SCAN_T, cols) and cols == SCAN_HALF * S5_W
    slab = pltpu.VMEM((2 * S5_STATES // LANES, SCAN_T * SCAN_HALF, LANES), F32)
    cast = _cast_spec(w_cast, n, lambda j: j)
    return pl.pallas_call(
        _s5scan_kernel,
        grid=(n,),
        in_specs=[pl.BlockSpec((SCAN_T, cols), lambda j: (j, 0)),
                  pl.BlockSpec((SCAN_T, cols), lambda j: (n - 1 - j, 0)),
                  pl.BlockSpec((SCAN_T, cols), lambda j: (0, 0)),
                  pl.BlockSpec((2, S5_W, 2 * S5_STATES), lambda j: (0, 0, 0)),
                  pl.BlockSpec((2, 2 * S5_STATES, S5_W), lambda j: (0, 0, 0)),
                  pl.BlockSpec((SUBLANES, 2 * S5_STATES), lambda j: (0, 0)),
                  cast],
        out_specs=[pl.BlockSpec((SCAN_T, cols), lambda j: (j, 0)),
                   pl.BlockSpec((SCAN_T, cols), lambda j: (n - 1 - j, 0)),
                   cast],
        out_shape=[jax.ShapeDtypeStruct((length, cols), F32)] * 2 + [jax.ShapeDtypeStruct(w_cast.shape, BF16)],
        scratch_shapes=[slab, slab, pltpu.VMEM((SUBLANES, 2 * S5_STATES), F32)],
        compiler_params=_params(("arbitrary",), VMEM_LIMIT_BYTES),
        name="s5scan",
    )(u_lat, u_lat, u_ctx, bmat, cmat, tab, w_cast)


def _outproj_kernel(x_ref, zs_ref, yf_ref, yb_ref, gm_ref, mod_ref, d_ref, wglu_ref, bglu_ref, wout_ref,
                    ln_ref, wr_ref, x1_ref, hp_ref, aff_ref):
    y = d_ref[...] * zs_ref[...] + yf_ref[...] + yb_ref[...]
    g = jax.nn.gelu(y)
    s5 = g * jax.nn.sigmoid(jnp.dot(g.astype(BF16), wglu_ref[...], preferred_element_type=F32) + bglu_ref[...])
    mix = (jnp.dot(s5.astype(BF16), wout_ref[0:S5_W, :], preferred_element_type=F32)
           + jnp.dot(gm_ref[...], wout_ref[S5_W:, :], preferred_element_type=F32))
    x1 = _layer_norm(ALPHA * x_ref[...] + mod_ref[2:3, :] * mix) * ln_ref[0:1, :] + ln_ref[1:2, :]
    x1_ref[...] = x1
    h = _layer_norm(x1) * (1.0 + mod_ref[4:5, :]) + mod_ref[3:4, :]
    h_hi = h.astype(BF16)
    h_lo = (h - h_hi.astype(F32)).astype(BF16)
    contract_last = (((1,), (1,)), ((), ()))
    logits = (lax.dot_general(wr_ref[0], h_hi, contract_last, preferred_element_type=F32)
              + lax.dot_general(wr_ref[0], h_lo, contract_last, preferred_element_type=F32)
              + lax.dot_general(wr_ref[1], h_hi, contract_last, preferred_element_type=F32))
    ex = jnp.exp(logits - jnp.max(logits, axis=0, keepdims=True))
    aff_ref[...] = ex / jnp.sum(ex, axis=0, keepdims=True)
    _store_row_tiles(hp_ref, 0, h)


def _outproj(x, zs, yf, yb, gm, mod, d, w_glu, b_glu, w_out, ln1, w_rt):
    bn, length, _ = x.shape
    tok = lambda b, i: (b, i, 0)
    s5 = lambda b, i: (i, b)
    const2 = lambda b, i: (0, 0)
    return pl.pallas_call(
        _outproj_kernel,
        grid=(bn, length // TOK_TILE),
        in_specs=[pl.BlockSpec((None, TOK_TILE, D_MODEL), tok),
                  pl.BlockSpec((TOK_TILE, S5_W), s5),
                  pl.BlockSpec((TOK_TILE, S5_W), s5),
                  pl.BlockSpec((TOK_TILE, S5_W), s5),
                  pl.BlockSpec((None, TOK_TILE, GM_W), tok),
                  pl.BlockSpec((None, 6, D_MODEL), lambda b, i: (b, 0, 0)),
                  pl.BlockSpec((1, S5_W), const2),
                  pl.BlockSpec((S5_W, S5_W), const2),
                  pl.BlockSpec((1, S5_W), const2),
                  pl.BlockSpec((D_MODEL, D_MODEL), const2),
                  pl.BlockSpec((2, D_MODEL), const2),
                  pl.BlockSpec((2, N_EXPERTS, D_MODEL), lambda b, i: (0, 0, 0))],
        out_specs=[pl.BlockSpec((None, TOK_TILE, D_MODEL), tok),
                   pl.BlockSpec((None, TOK_TILE * ROW_CHUNKS, LANES), tok),
                   pl.BlockSpec((None, N_EXPERTS, TOK_TILE), lambda b, i: (b, 0, i))],
        out_shape=[jax.ShapeDtypeStruct((bn, length, D_MODEL), F32),
                   jax.ShapeDtypeStruct((bn, length * ROW_CHUNKS, LANES), F32),
                   jax.ShapeDtypeStruct((bn, N_EXPERTS, length), F32)],
        compiler_params=_params(("arbitrary", "arbitrary"), VMEM_LIMIT_BYTES),
        name="outproj",
    )(x, zs, yf, yb, gm, mod, d, w_glu, b_glu, w_out, ln1, w_rt)


def _route_kernel(aff_ref, key_ref, offs_ref, *, cap):
    aff = aff_ref[...]
    ne, nr, _ = aff.shape

    def count(mask):
        return jnp.sum(jnp.sum(mask.astype(F32), axis=2, keepdims=True), axis=1, keepdims=True)

    normal = count(aff >= F32_TINY) >= cap
    lo = jnp.where(normal, F32_TINY, 0.0)
    hi = jnp.where(normal, 2.0, F32_TINY)
    for it in range(BRACKET_GEO_STEPS + BRACKET_LIN_STEPS):
        mid = 0.5 * (lo + hi)
        if it < BRACKET_GEO_STEPS:
            mid = jnp.where(normal, jnp.sqrt(lo * hi), mid)
        ok = count(aff >= mid) >= cap
        lo = jnp.where(ok, mid, lo)
        hi = jnp.where(ok, hi, mid)
    thr = jnp.min(jnp.min(jnp.where(aff >= lo, aff, 2.0), axis=2, keepdims=True), axis=1, keepdims=True)
    above = aff > thr
    tied = aff == thr
    need = cap - count(above)

    kk = lax.broadcasted_iota(jnp.int32, (LANES, LANES), 0)
    ll = lax.broadcasted_iota(jnp.int32, (LANES, LANES), 1)
    upper = (kk <= ll).astype(BF16)
    ones = jnp.ones((LANES, LANES), BF16)
    ri = lax.broadcasted_iota(jnp.int32, (nr, nr), 0)
    rj = lax.broadcasted_iota(jnp.int32, (nr, nr), 1)
    lower = (rj < ri).astype(BF16)

    def exclusive_prefix(mask):
        m = mask.astype(BF16).reshape(ne * nr, LANES)
        in_row = jnp.dot(m, upper, preferred_element_type=F32).reshape(ne, nr, LANES)
        row_tot = jnp.dot(m, ones, preferred_element_type=F32).reshape(ne, nr, LANES)
        row_off = jnp.stack([jnp.dot(lower, row_tot[e].astype(BF16), preferred_element_type=F32)
                             for e in range(ne)])
        return in_row - mask.astype(F32) + row_off, row_off

    tie_rank, _ = exclusive_prefix(tied)
    sel = above | (tied & (tie_rank < need))
    pos, row_off = exclusive_prefix(sel)
    key_ref[...] = jnp.where(sel, pos, -1.0)
    offs_ref[...] = row_off


def _route(aff4, cap):
    bn, ne, nr, _ = aff4.shape
    spec = pl.BlockSpec((None, ne, nr, LANES), lambda b: (b, 0, 0, 0))
    return pl.pallas_call(
        functools.partial(_route_kernel, cap=cap),
        grid=(bn,),
        in_specs=[spec],
        out_specs=[spec, spec],
        out_shape=[jax.ShapeDtypeStruct(aff4.shape, F32)] * 2,
        compiler_params=_params(("arbitrary",), VMEM_LIMIT_BYTES),
        name="route",
    )(aff4)


def _compact_kernel(offs_sref, key_ref, aff_ref, wcast_ref, idx_ref, gate_ref, wcast_out_ref, acc_i, acc_g, *, cap):
    _cast_block(wcast_ref, wcast_out_ref)
    be = pl.program_id(0) * pl.num_programs(1) + pl.program_id(1)
    nr = key_ref.shape[0]
    acc_i[...] = jnp.zeros_like(acc_i)
    acc_g[...] = jnp.zeros_like(acc_g)
    slot = lax.broadcasted_iota(jnp.int32, (COMPACT_STEP, LANES), 0)
    lane = lax.broadcasted_iota(jnp.int32, (1, LANES), 1)

    def body(r, carry):
        off = offs_sref[be * (nr + 1) + r]
        end = offs_sref[be * (nr + 1) + r + 1]
        base = (off // SUBLANES) * SUBLANES
        key = key_ref[r].astype(jnp.int32)
        tok = (lane + r * LANES).astype(F32)
        aff = aff_ref[r]

        def window(j, carry):
            first = pl.multiple_of(base + j * COMPACT_STEP, SUBLANES)
            win = pl.ds(first, COMPACT_STEP)
            hit = (slot + first) == key
            acc_i[win, :] += jnp.where(hit, tok, 0.0)
            acc_g[win, :] += jnp.where(hit, aff, 0.0)
            return carry

        return lax.fori_loop(0, (end - base + COMPACT_STEP - 1) // COMPACT_STEP, window, carry)

    lax.fori_loop(0, nr, body, 0)
    idx_ref[...] = jnp.sum(acc_i[0:cap, :], axis=1, keepdims=True).astype(jnp.int32)
    gate_ref[...] = jnp.sum(acc_g[0:cap, :], axis=1, keepdims=True)


def _compact(row_offs, key5, aff5, cap, w_cast):
    bn, ne, nr, _, _ = key5.shape
    spec_in = pl.BlockSpec((None, None, nr, 1, LANES), lambda b, e, offs: (b, e, 0, 0, 0))
    spec_out = pl.BlockSpec((None, None, cap, 1), lambda b, e, offs: (b, e, 0, 0))
    cast = _cast_spec(w_cast, bn * ne, lambda b, e, offs: b * ne + e)
    return pl.pallas_call(
        functools.partial(_compact_kernel, cap=cap),
        grid_spec=pltpu.PrefetchScalarGridSpec(
            num_scalar_prefetch=1,
            grid=(bn, ne),
            in_specs=[spec_in, spec_in, cast],
            out_specs=[spec_out, spec_out, cast],
            scratch_shapes=[pltpu.VMEM((cap + COMPACT_STEP, LANES), F32),
                            pltpu.VMEM((cap + COMPACT_STEP, LANES), F32)]),
        out_shape=[jax.ShapeDtypeStruct((bn, ne, cap, 1), jnp.int32),
                   jax.ShapeDtypeStruct((bn, ne, cap, 1), F32),
                   jax.ShapeDtypeStruct(w_cast.shape, BF16)],
        compiler_params=_params(("arbitrary", "arbitrary"), VMEM_LIMIT_BYTES),
        name="compact",
    )(row_offs, key5, aff5, w_cast)


def _experts_kernel(rows_sref, h_hbm, gate_ref, wg_ref, wu_ref, wd_ref, res_ref, xg_ref, sem, *, cap):
    ne = pl.num_programs(1)
    step = pl.program_id(0) * ne + pl.program_id(1)
    last = pl.num_programs(0) * ne - 1
    slot = step % 2
    nxt = jnp.minimum(step + 1, last)

    def start_row(c, of_step, into_slot):
        pltpu.make_async_copy(h_hbm.at[_tile_at(rows_sref[of_step * cap + c]), :],
                              xg_ref.at[into_slot, _row_tile(c), :], sem.at[into_slot]).start()

    def wait_rows(of_slot):
        pltpu.make_async_copy(h_hbm.at[pl.ds(0, cap * ROW_CHUNKS), :], xg_ref.at[of_slot],
                              sem.at[of_slot]).wait()

    @pl.when(step == 0)
    def _first_rows():
        def body(c, carry):
            start_row(c, step, slot)
            return carry

        lax.fori_loop(0, cap, body, 0)

    wait_rows(slot)
    for m in range(cap // ROW_TILE):
        rows = slice(m * ROW_TILE, (m + 1) * ROW_TILE)
        for c in range(rows.start, rows.stop):
            start_row(c, nxt, 1 - slot)
        xr = _load_row_tiles(xg_ref.at[slot], rows.start, ROW_TILE).astype(BF16)
        hg = jnp.dot(xr, wg_ref[...], preferred_element_type=F32)
        hu = jnp.dot(xr, wu_ref[...], preferred_element_type=F32)
        hid = (hg * jax.nn.sigmoid(hg) * hu).astype(BF16)
        out = jnp.dot(hid, wd_ref[...], preferred_element_type=F32)
        _store_row_tiles(res_ref, rows.start, out * gate_ref[rows, :])

    @pl.when(step == last)
    def _drain():
        wait_rows(1 - slot)


def _experts(tile_rows, h, gate, wg, wu, wd, cap):
    bn = gate.shape[0]
    ne = wg.shape[0]
    return pl.pallas_call(
        functools.partial(_experts_kernel, cap=cap),
        grid_spec=pltpu.PrefetchScalarGridSpec(
            num_scalar_prefetch=1,
            grid=(bn, ne),
            in_specs=[pl.BlockSpec(memory_space=pl.ANY),
                      pl.BlockSpec((None, None, cap, 1), lambda b, e, rows: (b, e, 0, 0)),
                      pl.BlockSpec((None, D_MODEL, D_FF), lambda b, e, rows: (e, 0, 0)),
                      pl.BlockSpec((None, D_MODEL, D_FF), lambda b, e, rows: (e, 0, 0)),
                      pl.BlockSpec((None, D_FF, D_MODEL), lambda b, e, rows: (e, 0, 0))],
            out_specs=pl.BlockSpec((None, None, cap * ROW_CHUNKS, LANES), lambda b, e, rows: (b, e, 0, 0)),
            scratch_shapes=[pltpu.VMEM((2, cap * ROW_CHUNKS, LANES), F32),
                            pltpu.SemaphoreType.DMA((2,))]),
        out_shape=jax.ShapeDtypeStruct((bn, ne, cap * ROW_CHUNKS, LANES), F32),
        compiler_params=_params(("arbitrary", "arbitrary"), VMEM_LIMIT_BYTES),
        name="experts",
    )(tile_rows, h, gate, wg, wu, wd)


SCATTER_UNROLL = 8


def _combine_kernel(rows_sref, res_ref, x1_ref, mod_ref, ln_ref, o_ref, acc_ref, *, cap, ne):
    b = pl.program_id(0)
    step = pl.program_id(1)

    @pl.when(step == 0)
    def _init():
        acc_ref[...] = jnp.zeros_like(acc_ref)

    @pl.when(step < ne)
    def _scatter():
        base = (b * ne + step) * cap

        def body(i, carry):
            first = i * SCATTER_UNROLL
            toks = [_tile_at(rows_sref[base + first + k]) for k in range(SCATTER_UNROLL)]
            sums = [acc_ref[toks[k], :] + res_ref[_row_tile(first + k), :] for k in range(SCATTER_UNROLL)]
            for k in range(SCATTER_UNROLL):
                acc_ref[toks[k], :] = sums[k]
            return carry

        lax.fori_loop(0, cap // SCATTER_UNROLL, body, 0)

    @pl.when(step >= ne)
    def _finish():
        moe = _load_row_tiles(acc_ref, (step - ne) * FIN_TILE, FIN_TILE)
        y = ALPHA * x1_ref[...] + mod_ref[5:6, :] * moe
        o_ref[...] = _layer_norm(y) * ln_ref[0:1, :] + ln_ref[1:2, :]


def _combine(tile_rows, res, x1, mod, ln2, cap):
    bn, length, _ = x1.shape
    ne = res.shape[1]
    assert cap % SCATTER_UNROLL == 0
    tok = lambda b, s, idx: (b, jnp.maximum(s - ne, 0), 0)
    return pl.pallas_call(
        functools.partial(_combine_kernel, cap=cap, ne=ne),
        grid_spec=pltpu.PrefetchScalarGridSpec(
            num_scalar_prefetch=1,
            grid=(bn, ne + length // FIN_TILE),
            in_specs=[pl.BlockSpec((None, None, cap * ROW_CHUNKS, LANES),
                                   lambda b, s, idx: (b, jnp.minimum(s, ne - 1), 0, 0)),
                      pl.BlockSpec((None, FIN_TILE, D_MODEL), tok),
                      pl.BlockSpec((None, 6, D_MODEL), lambda b, s, idx: (b, 0, 0)),
                      pl.BlockSpec((2, D_MODEL), lambda b, s, idx: (0, 0))],
            out_specs=pl.BlockSpec((None, FIN_TILE, D_MODEL), tok),
            scratch_shapes=[pltpu.VMEM((length * ROW_CHUNKS, LANES), F32)]),
        out_shape=jax.ShapeDtypeStruct((bn, length, D_MODEL), F32),
        compiler_params=_params(("arbitrary", "arbitrary"), VMEM_LIMIT_BYTES),
        name="combine",
    )(tile_rows, res, x1, mod, ln2)


def kernel(x, c, ctx, c_ctx, w_ada, b_ada, w_in, gm_ws, gm_bs, s5_a_re, s5_a_im, s5_log_step, s5_b_re, s5_b_im,
           s5_c_re, s5_c_im, s5_d, s5_w_glu, s5_b_glu, w_out, ln1_g, ln1_b, w_router, moe_w_gate, moe_w_up,
           moe_w_down, ln2_g, ln2_b):
    assert w_ada.shape[0] == DEPTH == 1
    bn, length, _ = x.shape
    assert bn == SCAN_HALF and ctx.shape[1] == SCAN_T and length % TOK_TILE == 0
    cap = CAPACITY_FACTOR * length // N_EXPERTS

    cc = jnp.zeros((SUBLANES, D_MODEL), F32).at[:bn].set(c).at[bn].set(c_ctx)
    mod = _adaln(cc, w_ada[0], b_ada[0][None, :])
    mod_lat = mod[:bn].reshape(bn, 6, D_MODEL)
    mod_ctx = mod[bn:bn + 1].reshape(1, 6, D_MODEL)

    w_in_b = w_in[0].astype(BF16)
    zs, gm, wg_b = _inproj(x, mod_lat, w_in_b, gm_ws[0].astype(BF16), gm_bs[0][:, :, None], moe_w_gate[0])
    zc = _ctxproj(ctx, mod_ctx, w_in_b)

    bmat, cmat, tab = _s5_tables(s5_a_re[0], s5_a_im[0], s5_log_step[0], s5_b_re[0], s5_b_im[0],
                                 s5_c_re[0], s5_c_im[0])
    yf, yb, wd_b = _s5scan(zs, zc, bmat, cmat, tab, moe_w_down[0])

    w_rt = w_router[0].T
    w_rt_hi = w_rt.astype(BF16)
    w_rt_parts = jnp.stack([w_rt_hi, (w_rt - w_rt_hi.astype(F32)).astype(BF16)])
    x1, hp, aff = _outproj(x, zs, yf, yb, gm, mod_lat, s5_d[0].reshape(1, S5_W), s5_w_glu[0].astype(BF16),
                           s5_b_glu[0][None, :], w_out[0].astype(BF16), jnp.stack([ln1_g[0], ln1_b[0]]),
                           w_rt_parts)

    aff4 = aff.reshape(bn, N_EXPERTS, length // LANES, LANES)
    key4, offs4 = _route(aff4, cap)
    row_offs = jnp.concatenate([offs4[..., 0].astype(jnp.int32), jnp.full((bn, N_EXPERTS, 1), cap, jnp.int32)],
                               axis=-1)
    rows5 = (bn, N_EXPERTS, length // LANES, 1, LANES)
    idx, gate, wu_b = _compact(row_offs.reshape(-1), key4.reshape(rows5), aff4.reshape(rows5), cap, moe_w_up[0])
    tile_rows = idx.reshape(bn, -1) * ROW_CHUNKS
    batch_rows = jnp.arange(bn, dtype=jnp.int32)[:, None] * (length * ROW_CHUNKS)

    res = _experts((tile_rows + batch_rows).reshape(-1), hp.reshape(bn * length * ROW_CHUNKS, LANES), gate,
                   wg_b, wu_b, wd_b, cap)

    return _combine(tile_rows.reshape(-1), res, x1, mod_lat, jnp.stack([ln2_g[0], ln2_b[0]]), cap)
```

```python
import functools

import jax
import jax.numpy as jnp
from jax import lax
from jax.experimental import pallas as pl
from jax.experimental.pallas import tpu as pltpu

D_MODEL = 1024
CHUNK = 128
S5_W = D_MODEL // 4
GM_W = D_MODEL - S5_W
GM_HEAD_DIM = 128
GM_HEADS = GM_W // GM_HEAD_DIM
S5_GROUP = 16
S5_GROUPS = S5_W // S5_GROUP
S5_STATE = 64
S5_STATES = S5_GROUPS * S5_STATE
IN_COLS = S5_W + 2 * GM_W
N_EXPERTS = 16
CAPACITY_FACTOR = 2
D_FF = 2816
DEPTH = 1
ALPHA = (2.0 * DEPTH) ** 0.25
LN_EPS = 1e-6

F32 = jnp.float32
BF16 = jnp.bfloat16
HIGHEST = lax.Precision.HIGHEST

LANES = 128
SUBLANES = 8
ROW_CHUNKS = D_MODEL // LANES
VMEM_LIMIT_BYTES = 58 * 1024 * 1024

TOK_TILE = 1024
FIN_TILE = 512
SCAN_T = 256
SCAN_LANES = 512
SCAN_HALF = SUBLANES // 2
ROW_TILE = 256
COMPACT_WIN = CHUNK + SUBLANES
F32_TINY = 2.0 ** -126
BRACKET_GEO_STEPS = 8
BRACKET_LIN_STEPS = 26


def _layer_norm(x):
    mu = jnp.mean(x, axis=-1, keepdims=True)
    xc = x - mu
    var = jnp.mean(xc * xc, axis=-1, keepdims=True)
    return xc * lax.rsqrt(var + LN_EPS)


def _params(semantics, vmem=None):
    return pltpu.CompilerParams(dimension_semantics=semantics, vmem_limit_bytes=vmem)


def _store_row_tiles(ref, first_token, x):
    n = x.shape[0]
    for s in range(ROW_CHUNKS):
        ref[pl.ds(first_token * ROW_CHUNKS + s, n, stride=ROW_CHUNKS), :] = x[:, s * LANES:(s + 1) * LANES]


def _load_row_tiles(ref, first_token, n):
    return jnp.concatenate(
        [ref[pl.ds(first_token * ROW_CHUNKS + s, n, stride=ROW_CHUNKS), :] for s in range(ROW_CHUNKS)], axis=1)


def _row_tile(token):
    return pl.ds(pl.multiple_of(token * ROW_CHUNKS, ROW_CHUNKS), ROW_CHUNKS)


def _tile_at(first_row):
    return pl.ds(pl.multiple_of(first_row, ROW_CHUNKS), ROW_CHUNKS)


def _cast_spec(w, n_steps, step_of):
    ne, rows, cols = w.shape
    per_expert = n_steps // ne
    assert per_expert * ne == n_steps and rows % (per_expert * 2 * SUBLANES) == 0
    return pl.BlockSpec((None, rows // per_expert, cols),
                        lambda *ids: (step_of(*ids) // per_expert, step_of(*ids) % per_expert, 0))


def _cast_block(w_ref, o_ref):
    o_ref[...] = w_ref[...].astype(BF16)


def _adaln_kernel(c_ref, w_ref, b_ref, o_ref):
    c = c_ref[...]
    a = c * jax.nn.sigmoid(c)
    o_ref[...] = jnp.dot(a, w_ref[...], precision=HIGHEST, preferred_element_type=F32) + b_ref[...]


def _adaln(cc, w_ada, b_ada):
    n = w_ada.shape[1] // D_MODEL
    return pl.pallas_call(
        _adaln_kernel,
        grid=(n,),
        in_specs=[pl.BlockSpec((SUBLANES, D_MODEL), lambda j: (0, 0)),
                  pl.BlockSpec((D_MODEL, D_MODEL), lambda j: (0, j)),
                  pl.BlockSpec((1, D_MODEL), lambda j: (0, j))],
        out_specs=pl.BlockSpec((SUBLANES, D_MODEL), lambda j: (0, j)),
        out_shape=jax.ShapeDtypeStruct((SUBLANES, w_ada.shape[1]), F32),
        compiler_params=_params(("arbitrary",)),
        name="adaln",
    )(cc, w_ada, b_ada)


def _inproj_kernel(x_ref, mod_ref, w_ref, ws_ref, bs_ref, wcast_ref, zs_ref, gm_ref, wcast_out_ref):
    _cast_block(wcast_ref, wcast_out_ref)
    h = _layer_norm(x_ref[...]) * (1.0 + mod_ref[1:2, :]) + mod_ref[0:1, :]
    z = jnp.dot(h.astype(BF16), w_ref[...], preferred_element_type=F32)
    zs_ref[...] = z[:, :S5_W]
    for g in range(GM_HEADS):
        lo = S5_W + g * GM_HEAD_DIM
        u = jax.nn.gelu(z[:, lo:lo + GM_HEAD_DIM])
        v = jax.nn.gelu(z[:, lo + GM_W:lo + GM_W + GM_HEAD_DIM])
        vn = _layer_norm(v).astype(BF16)
        wsg = ws_ref[g]
        bsg = bs_ref[g]
        for c in range(TOK_TILE // CHUNK):
            rows = slice(c * CHUNK, (c + 1) * CHUNK)
            mixed = jnp.dot(wsg, vn[rows, :], preferred_element_type=F32) + bsg
            gm_ref[rows, g * GM_HEAD_DIM:(g + 1) * GM_HEAD_DIM] = (u[rows, :] * mixed).astype(BF16)


def _inproj(x, mod, w_in, gm_ws, gm_bs, w_cast):
    bn, length, _ = x.shape
    nt = length // TOK_TILE
    cast = _cast_spec(w_cast, bn * nt, lambda b, i: b * nt + i)
    return pl.pallas_call(
        _inproj_kernel,
        grid=(bn, nt),
        in_specs=[pl.BlockSpec((None, TOK_TILE, D_MODEL), lambda b, i: (b, i, 0)),
                  pl.BlockSpec((None, 6, D_MODEL), lambda b, i: (b, 0, 0)),
                  pl.BlockSpec((D_MODEL, IN_COLS), lambda b, i: (0, 0)),
                  pl.BlockSpec((GM_HEADS, CHUNK, CHUNK), lambda b, i: (0, 0, 0)),
                  pl.BlockSpec((GM_HEADS, CHUNK, 1), lambda b, i: (0, 0, 0)),
                  cast],
        out_specs=[pl.BlockSpec((TOK_TILE, S5_W), lambda b, i: (i, b)),
                   pl.BlockSpec((None, TOK_TILE, GM_W), lambda b, i: (b, i, 0)),
                   cast],
        out_shape=[jax.ShapeDtypeStruct((length, bn * S5_W), F32),
                   jax.ShapeDtypeStruct((bn, length, GM_W), BF16),
                   jax.ShapeDtypeStruct(w_cast.shape, BF16)],
        compiler_params=_params(("arbitrary", "arbitrary"), VMEM_LIMIT_BYTES),
        name="inproj",
    )(x, mod, w_in, gm_ws, gm_bs, w_cast)


def _ctxproj_kernel(x_ref, mod_ref, w_ref, o_ref):
    h = _layer_norm(x_ref[...]) * (1.0 + mod_ref[1:2, :]) + mod_ref[0:1, :]
    o_ref[...] = jnp.dot(h.astype(BF16), w_ref[...], preferred_element_type=F32)


def _ctxproj(ctx, mod_ctx, w_in):
    bn, clen, _ = ctx.shape
    return pl.pallas_call(
        _ctxproj_kernel,
        grid=(bn,),
        in_specs=[pl.BlockSpec((None, clen, D_MODEL), lambda b: (b, 0, 0)),
                  pl.BlockSpec((None, 6, D_MODEL), lambda b: (0, 0, 0)),
                  pl.BlockSpec((D_MODEL, S5_W), lambda b: (0, 0))],
        out_specs=pl.BlockSpec((clen, S5_W), lambda b: (0, b)),
        out_shape=jax.ShapeDtypeStruct((clen, bn * S5_W), F32),
        compiler_params=_params(("arbitrary",)),
        name="ctxproj",
    )(ctx, mod_ctx, w_in)


def _s5_tables(a_re, a_im, log_step, b_re, b_im, c_re, c_im):
    step = jnp.exp(log_step)[..., None]
    mag = jnp.exp(a_re * step)
    lb_re = mag * jnp.cos(a_im * step)
    lb_im = mag * jnp.sin(a_im * step)
    den = a_re * a_re + a_im * a_im
    q_re = ((lb_re - 1.0) * a_re + lb_im * a_im) / den
    q_im = (lb_im * a_re - (lb_re - 1.0) * a_im) / den
    bb_re = q_re[..., None] * b_re - q_im[..., None] * b_im
    bb_im = q_re[..., None] * b_im + q_im[..., None] * b_re
    eye = jnp.eye(S5_GROUPS, dtype=F32)

    def in_block(m):
        return jnp.einsum('dgph,gk->dghkp', m, eye).reshape(2, S5_W, S5_STATES)

    def out_block(m):
        return jnp.einsum('dghp,gk->dgpkh', m, eye).reshape(2, S5_STATES, S5_W)

    bmat = jnp.concatenate([in_block(bb_re), in_block(bb_im)], axis=2).astype(BF16)
    cmat = jnp.concatenate([out_block(c_re), out_block(-c_im)], axis=1).astype(BF16)
    lam = jnp.concatenate([lb_re.reshape(2, 1, S5_STATES), lb_im.reshape(2, 1, S5_STATES)], axis=2)
    tab = jnp.concatenate([jnp.repeat(lam[0], SCAN_HALF, axis=0), jnp.repeat(lam[1], SCAN_HALF, axis=0)], axis=0)
    return bmat, cmat, tab


def _scan_tiles(df_ref, db_ref, s_ref, tab_ref):
    n_tiles = df_ref.shape[1] // SUBLANES
    slabs_per_block = SCAN_LANES // LANES
    top = lax.broadcasted_iota(jnp.int32, (SUBLANES, SCAN_LANES), 0) < SCAN_HALF

    def advance(l_re, l_im, s_re, s_im, x_re, x_im):
        return x_re + l_re * s_re - l_im * s_im, x_im + l_re * s_im + l_im * s_re

    def swap(x):
        return pltpu.roll(x, SCAN_HALF, 0)

    for lb in range(S5_STATES // SCAN_LANES):
        re_cols = pl.ds(lb * SCAN_LANES, SCAN_LANES)
        im_cols = pl.ds(S5_STATES + lb * SCAN_LANES, SCAN_LANES)
        re_slabs = [lb * slabs_per_block + k for k in range(slabs_per_block)]
        im_slabs = [S5_STATES // LANES + s for s in re_slabs]
        l_re, l_im = tab_ref[:, re_cols], tab_ref[:, im_cols]

        def body(i, s, re_slabs=re_slabs, im_slabs=im_slabs, l_re=l_re, l_im=l_im):
            rows_f = pl.ds(pl.multiple_of(i * SUBLANES, SUBLANES), SUBLANES)
            rows_b = pl.ds(pl.multiple_of((n_tiles - 1 - i) * SUBLANES, SUBLANES), SUBLANES)
            out = []
            for slabs, s_part in ((re_slabs, 0), (im_slabs, 1)):
                f = jnp.concatenate([df_ref[k, rows_f, :] for k in slabs], axis=1)
                b = jnp.concatenate([db_ref[k, rows_b, :] for k in slabs], axis=1)
                out.append((jnp.where(top, f, b), swap(jnp.where(top, b, f))))
            (x0_re, x1_re), (x0_im, x1_im) = out
            y0_re, y0_im = advance(l_re, l_im, s[0], s[1], x0_re, x0_im)
            y1_re, y1_im = advance(l_re, l_im, y0_re, y0_im, x1_re, x1_im)
            for slabs, y0, y1 in ((re_slabs, y0_re, y1_re), (im_slabs, y0_im, y1_im)):
                r1 = swap(y1)
                f_new = jnp.where(top, y0, r1)
                b_new = jnp.where(top, r1, y0)
                for n, k in enumerate(slabs):
                    df_ref[k, rows_f, :] = f_new[:, n * LANES:(n + 1) * LANES]
                    db_ref[k, rows_b, :] = b_new[:, n * LANES:(n + 1) * LANES]
            return y1_re, y1_im

        s_re, s_im = lax.fori_loop(0, n_tiles, body, (s_ref[:, re_cols], s_ref[:, im_cols]))
        s_ref[:, re_cols] = s_re
        s_ref[:, im_cols] = s_im


def _s5scan_kernel(uf_ref, ub_ref, uc_ref, bmat_ref, cmat_ref, tab_ref, wcast_ref, yf_ref, yb_ref, wcast_out_ref,
                   df_ref, db_ref, s_ref):
    _cast_block(wcast_ref, wcast_out_ref)
    j = pl.program_id(0)
    n_slabs = 2 * S5_STATES // LANES

    def batch_rows(b):
        return pl.ds(b, SCAN_T, stride=SCAN_HALF)

    def drive(u_ref, d_ref, direction):
        u = jnp.concatenate([u_ref[:, b * S5_W:(b + 1) * S5_W] for b in range(SCAN_HALF)], axis=0)
        d = jnp.dot(u.astype(BF16), bmat_ref[direction], preferred_element_type=F32)
        for b in range(SCAN_HALF):
            for k in range(n_slabs):
                d_ref[k, batch_rows(b), :] = d[b * SCAN_T:(b + 1) * SCAN_T, k * LANES:(k + 1) * LANES]

    def readout(d_ref, y_ref, direction):
        s = jnp.concatenate(
            [jnp.concatenate([d_ref[k, batch_rows(b), :] for k in range(n_slabs)], axis=1)
             for b in range(SCAN_HALF)], axis=0)
        y = jnp.dot(s.astype(BF16), cmat_ref[direction], preferred_element_type=F32)
        for b in range(SCAN_HALF):
            y_ref[:, b * S5_W:(b + 1) * S5_W] = y[b * SCAN_T:(b + 1) * SCAN_T, :]

    @pl.when(j == 0)
    def _context():
        s_ref[...] = jnp.zeros_like(s_ref)
        drive(uc_ref, df_ref, 0)
        drive(uc_ref, db_ref, 1)
        _scan_tiles(df_ref, db_ref, s_ref, tab_ref)

    drive(uf_ref, df_ref, 0)
    drive(ub_ref, db_ref, 1)
    _scan_tiles(df_ref, db_ref, s_ref, tab_ref)
    readout(df_ref, yf_ref, 0)
    readout(db_ref, yb_ref, 1)


def _s5scan(u_lat, u_ctx, bmat, cmat, tab, w_cast):
    length, cols = u_lat.shape
    n = length // SCAN_T
    assert u_ctx.shape == (SCAN_T, cols) and cols == SCAN_HALF * S5_W
    slab = pltpu.VMEM((2 * S5_STATES // LANES, SCAN_T * SCAN_HALF, LANES), F32)
    cast = _cast_spec(w_cast, n, lambda j: j)
    return pl.pallas_call(
        _s5scan_kernel,
        grid=(n,),
        in_specs=[pl.BlockSpec((SCAN_T, cols), lambda j: (j, 0)),
                  pl.BlockSpec((SCAN_T, cols), lambda j: (n - 1 - j, 0)),
                  pl.BlockSpec((SCAN_T, cols), lambda j: (0, 0)),
                  pl.BlockSpec((2, S5_W, 2 * S5_STATES), lambda j: (0, 0, 0)),
                  pl.BlockSpec((2, 2 * S5_STATES, S5_W), lambda j: (0, 0, 0)),
                  pl.BlockSpec((SUBLANES, 2 * S5_STATES), lambda j: (0, 0)),
                  cast],
        out_specs=[pl.BlockSpec((SCAN_T, cols), lambda j: (j, 0)),
                   pl.BlockSpec((SCAN_T, cols), lambda j: (n - 1 - j, 0)),
                   cast],
        out_shape=[jax.ShapeDtypeStruct((length, cols), F32)] * 2 + [jax.ShapeDtypeStruct(w_cast.shape, BF16)],
        scratch_shapes=[slab, slab, pltpu.VMEM((SUBLANES, 2 * S5_STATES), F32)],
        compiler_params=_params(("arbitrary",), VMEM_LIMIT_BYTES),
        name="s5scan",
    )(u_lat, u_lat, u_ctx, bmat, cmat, tab, w_cast)


def _outproj_kernel(x_ref, zs_ref, yf_ref, yb_ref, gm_ref, mod_ref, d_ref, wglu_ref, bglu_ref, wout_ref,
                    ln_ref, wr_ref, x1_ref, hp_ref, aff_ref):
    y = d_ref[...] * zs_ref[...] + yf_ref[...] + yb_ref[...]
    g = jax.nn.gelu(y)
    s5 = g * jax.nn.sigmoid(jnp.dot(g.astype(BF16), wglu_ref[...], preferred_element_type=F32) + bglu_ref[...])
    mix = (jnp.dot(s5.astype(BF16), wout_ref[0:S5_W, :], preferred_element_type=F32)
           + jnp.dot(gm_ref[...], wout_ref[S5_W:, :], preferred_element_type=F32))
    x1 = _layer_norm(ALPHA * x_ref[...] + mod_ref[2:3, :] * mix) * ln_ref[0:1, :] + ln_ref[1:2, :]
    x1_ref[...] = x1
    h = _layer_norm(x1) * (1.0 + mod_ref[4:5, :]) + mod_ref[3:4, :]
    h_hi = h.astype(BF16)
    h_lo = (h - h_hi.astype(F32)).astype(BF16)
    contract_last = (((1,), (1,)), ((), ()))
    logits = (lax.dot_general(wr_ref[0], h_hi, contract_last, preferred_element_type=F32)
              + lax.dot_general(wr_ref[0], h_lo, contract_last, preferred_element_type=F32)
              + lax.dot_general(wr_ref[1], h_hi, contract_last, preferred_element_type=F32))
    ex = jnp.exp(logits - jnp.max(logits, axis=0, keepdims=True))
    aff_ref[...] = ex / jnp.sum(ex, axis=0, keepdims=True)
    _store_row_tiles(hp_ref, 0, h)


def _outproj(x, zs, yf, yb, gm, mod, d, w_glu, b_glu, w_out, ln1, w_rt):
    bn, length, _ = x.shape
    tok = lambda b, i: (b, i, 0)
    s5 = lambda b, i: (i, b)
    const2 = lambda b, i: (0, 0)
    return pl.pallas_call(
        _outproj_kernel,
        grid=(bn, length // TOK_TILE),
        in_specs=[pl.BlockSpec((None, TOK_TILE, D_MODEL), tok),
                  pl.BlockSpec((TOK_TILE, S5_W), s5),
                  pl.BlockSpec((TOK_TILE, S5_W), s5),
                  pl.BlockSpec((TOK_TILE, S5_W), s5),
                  pl.BlockSpec((None, TOK_TILE, GM_W), tok),
                  pl.BlockSpec((None, 6, D_MODEL), lambda b, i: (b, 0, 0)),
                  pl.BlockSpec((1, S5_W), const2),
                  pl.BlockSpec((S5_W, S5_W), const2),
                  pl.BlockSpec((1, S5_W), const2),
                  pl.BlockSpec((D_MODEL, D_MODEL), const2),
                  pl.BlockSpec((2, D_MODEL), const2),
                  pl.BlockSpec((2, N_EXPERTS, D_MODEL), lambda b, i: (0, 0, 0))],
        out_specs=[pl.BlockSpec((None, TOK_TILE, D_MODEL), tok),
                   pl.BlockSpec((None, TOK_TILE * ROW_CHUNKS, LANES), tok),
                   pl.BlockSpec((None, N_EXPERTS, TOK_TILE), lambda b, i: (b, 0, i))],
        out_shape=[jax.ShapeDtypeStruct((bn, length, D_MODEL), F32),
                   jax.ShapeDtypeStruct((bn, length * ROW_CHUNKS, LANES), F32),
                   jax.ShapeDtypeStruct((bn, N_EXPERTS, length), F32)],
        compiler_params=_params(("arbitrary", "arbitrary"), VMEM_LIMIT_BYTES),
        name="outproj",
    )(x, zs, yf, yb, gm, mod, d, w_glu, b_glu, w_out, ln1, w_rt)


def _route_kernel(aff_ref, key_ref, offs_ref, *, cap):
    aff = aff_ref[...]
    ne, nr, _ = aff.shape

    def count(mask):
        return jnp.sum(jnp.sum(mask.astype(F32), axis=2, keepdims=True), axis=1, keepdims=True)

    normal = count(aff >= F32_TINY) >= cap
    lo = jnp.where(normal, F32_TINY, 0.0)
    hi = jnp.where(normal, 2.0, F32_TINY)
    for it in range(BRACKET_GEO_STEPS + BRACKET_LIN_STEPS):
        mid = 0.5 * (lo + hi)
        if it < BRACKET_GEO_STEPS:
            mid = jnp.where(normal, jnp.sqrt(lo * hi), mid)
        ok = count(aff >= mid) >= cap
        lo = jnp.where(ok, mid, lo)
        hi = jnp.where(ok, hi, mid)
    thr = jnp.min(jnp.min(jnp.where(aff >= lo, aff, 2.0), axis=2, keepdims=True), axis=1, keepdims=True)
    above = aff > thr
    tied = aff == thr
    need = cap - count(above)

    kk = lax.broadcasted_iota(jnp.int32, (LANES, LANES), 0)
    ll = lax.broadcasted_iota(jnp.int32, (LANES, LANES), 1)
    upper = (kk <= ll).astype(BF16)
    ones = jnp.ones((LANES, LANES), BF16)
    ri = lax.broadcasted_iota(jnp.int32, (nr, nr), 0)
    rj = lax.broadcasted_iota(jnp.int32, (nr, nr), 1)
    lower = (rj < ri).astype(BF16)

    def exclusive_prefix(mask):
        m = mask.astype(BF16).reshape(ne * nr, LANES)
        in_row = jnp.dot(m, upper, preferred_element_type=F32).reshape(ne, nr, LANES)
        row_tot = jnp.dot(m, ones, preferred_element_type=F32).reshape(ne, nr, LANES)
        row_off = jnp.stack([jnp.dot(lower, row_tot[e].astype(BF16), preferred_element_type=F32)
                             for e in range(ne)])
        return in_row - mask.astype(F32) + row_off, row_off

    tie_rank, _ = exclusive_prefix(tied)
    sel = above | (tied & (tie_rank < need))
    pos, row_off = exclusive_prefix(sel)
    key_ref[...] = jnp.where(sel, pos, -1.0)
    offs_ref[...] = row_off


def _route(aff4, cap):
    bn, ne, nr, _ = aff4.shape
    spec = pl.BlockSpec((None, ne, nr, LANES), lambda b: (b, 0, 0, 0))
    return pl.pallas_call(
        functools.partial(_route_kernel, cap=cap),
        grid=(bn,),
        in_specs=[spec],
        out_specs=[spec, spec],
        out_shape=[jax.ShapeDtypeStruct(aff4.shape, F32)] * 2,
        compiler_params=_params(("arbitrary",), VMEM_LIMIT_BYTES),
        name="route",
    )(aff4)


def _compact_kernel(offs_sref, key_ref, aff_ref, wcast_ref, idx_ref, gate_ref, wcast_out_ref, acc_i, acc_g, *, cap):
    _cast_block(wcast_ref, wcast_out_ref)
    be = pl.program_id(0) * pl.num_programs(1) + pl.program_id(1)
    nr = key_ref.shape[0]
    acc_i[...] = jnp.zeros_like(acc_i)
    acc_g[...] = jnp.zeros_like(acc_g)
    slot = lax.broadcasted_iota(jnp.int32, (COMPACT_WIN, LANES), 0)
    lane = lax.broadcasted_iota(jnp.int32, (1, LANES), 1)

    def body(r, carry):
        off = offs_sref[be * nr + r]
        base = pl.multiple_of((off // SUBLANES) * SUBLANES, SUBLANES)
        win = pl.ds(base, COMPACT_WIN)
        hit = (slot + base) == key_ref[r].astype(jnp.int32)
        tok = (lane + r * LANES).astype(F32)
        acc_i[win, :] += jnp.where(hit, tok, 0.0)
        acc_g[win, :] += jnp.where(hit, aff_ref[r], 0.0)
        return carry

    lax.fori_loop(0, nr, body, 0)
    idx_ref[...] = jnp.sum(acc_i[0:cap, :], axis=1, keepdims=True).astype(jnp.int32)
    gate_ref[...] = jnp.sum(acc_g[0:cap, :], axis=1, keepdims=True)


def _compact(row_offs, key5, aff5, cap, w_cast):
    bn, ne, nr, _, _ = key5.shape
    spec_in = pl.BlockSpec((None, None, nr, 1, LANES), lambda b, e, offs: (b, e, 0, 0, 0))
    spec_out = pl.BlockSpec((None, None, cap, 1), lambda b, e, offs: (b, e, 0, 0))
    cast = _cast_spec(w_cast, bn * ne, lambda b, e, offs: b * ne + e)
    return pl.pallas_call(
        functools.partial(_compact_kernel, cap=cap),
        grid_spec=pltpu.PrefetchScalarGridSpec(
            num_scalar_prefetch=1,
            grid=(bn, ne),
            in_specs=[spec_in, spec_in, cast],
            out_specs=[spec_out, spec_out, cast],
            scratch_shapes=[pltpu.VMEM((cap + COMPACT_WIN, LANES), F32),
                            pltpu.VMEM((cap + COMPACT_WIN, LANES), F32)]),
        out_shape=[jax.ShapeDtypeStruct((bn, ne, cap, 1), jnp.int32),
                   jax.ShapeDtypeStruct((bn, ne, cap, 1), F32),
                   jax.ShapeDtypeStruct(w_cast.shape, BF16)],
        compiler_params=_params(("arbitrary", "arbitrary"), VMEM_LIMIT_BYTES),
        name="compact",
    )(row_offs, key5, aff5, w_cast)


def _experts_kernel(rows_sref, h_hbm, gate_ref, wg_ref, wu_ref, wd_ref, res_ref, xg_ref, sem, *, cap):
    ne = pl.num_programs(1)
    step = pl.program_id(0) * ne + pl.program_id(1)
    last = pl.num_programs(0) * ne - 1
    slot = step % 2
    nxt = jnp.minimum(step + 1, last)

    def start_row(c, of_step, into_slot):
        pltpu.make_async_copy(h_hbm.at[of_step // ne, _tile_at(rows_sref[of_step * cap + c]), :],
                              xg_ref.at[into_slot, _row_tile(c), :], sem.at[into_slot]).start()

    def wait_rows(of_slot):
        pltpu.make_async_copy(h_hbm.at[0, pl.ds(0, cap * ROW_CHUNKS), :], xg_ref.at[of_slot],
                              sem.at[of_slot]).wait()

    @pl.when(step == 0)
    def _first_rows():
        def body(c, carry):
            start_row(c, step, slot)
            return carry

        lax.fori_loop(0, cap, body, 0)

    wait_rows(slot)
    for m in range(cap // ROW_TILE):
        rows = slice(m * ROW_TILE, (m + 1) * ROW_TILE)

        def start_quarter(q, first=rows.start):
            for c in range(first + q * ROW_TILE // 4, first + (q + 1) * ROW_TILE // 4):
                start_row(c, nxt, 1 - slot)

        start_quarter(0)
        xr = _load_row_tiles(xg_ref.at[slot], rows.start, ROW_TILE).astype(BF16)
        hg = jnp.dot(xr, wg_ref[...], preferred_element_type=F32)
        start_quarter(1)
        hu = jnp.dot(xr, wu_ref[...], preferred_element_type=F32)
        start_quarter(2)
        hid = (hg * jax.nn.sigmoid(hg) * hu).astype(BF16)
        start_quarter(3)
        out = jnp.dot(hid, wd_ref[...], preferred_element_type=F32)
        _store_row_tiles(res_ref, rows.start, out * gate_ref[rows, :])

    @pl.when(step == last)
    def _drain():
        wait_rows(1 - slot)


def _experts(tile_rows, h, gate, wg, wu, wd, cap):
    bn = h.shape[0]
    ne = wg.shape[0]
    return pl.pallas_call(
        functools.partial(_experts_kernel, cap=cap),
        grid_spec=pltpu.PrefetchScalarGridSpec(
            num_scalar_prefetch=1,
            grid=(bn, ne),
            in_specs=[pl.BlockSpec(memory_space=pl.ANY),
                      pl.BlockSpec((None, None, cap, 1), lambda b, e, rows: (b, e, 0, 0)),
                      pl.BlockSpec((None, D_MODEL, D_FF), lambda b, e, rows: (e, 0, 0)),
                      pl.BlockSpec((None, D_MODEL, D_FF), lambda b, e, rows: (e, 0, 0)),
                      pl.BlockSpec((None, D_FF, D_MODEL), lambda b, e, rows: (e, 0, 0))],
            out_specs=pl.BlockSpec((None, None, cap * ROW_CHUNKS, LANES), lambda b, e, rows: (b, e, 0, 0)),
            scratch_shapes=[pltpu.VMEM((2, cap * ROW_CHUNKS, LANES), F32),
                            pltpu.SemaphoreType.DMA((2,))]),
        out_shape=jax.ShapeDtypeStruct((bn, ne, cap * ROW_CHUNKS, LANES), F32),
        compiler_params=_params(("arbitrary", "arbitrary"), VMEM_LIMIT_BYTES),
        name="experts",
    )(tile_rows, h, gate, wg, wu, wd)


SCATTER_UNROLL = 8


def _combine_kernel(rows_sref, res_ref, x1_ref, mod_ref, ln_ref, o_ref, acc_ref, *, cap, ne):
    b = pl.program_id(0)
    step = pl.program_id(1)

    @pl.when(step == 0)
    def _init():
        acc_ref[...] = jnp.zeros_like(acc_ref)

    @pl.when(step < ne)
    def _scatter():
        base = (b * ne + step) * cap

        def body(i, carry):
            first = i * SCATTER_UNROLL
            toks = [_tile_at(rows_sref[base + first + k]) for k in range(SCATTER_UNROLL)]
            sums = [acc_ref[toks[k], :] + res_ref[_row_tile(first + k), :] for k in range(SCATTER_UNROLL)]
            for k in range(SCATTER_UNROLL):
                acc_ref[toks[k], :] = sums[k]
            return carry

        lax.fori_loop(0, cap // SCATTER_UNROLL, body, 0)

    @pl.when(step >= ne)
    def _finish():
        moe = _load_row_tiles(acc_ref, (step - ne) * FIN_TILE, FIN_TILE)
        y = ALPHA * x1_ref[...] + mod_ref[5:6, :] * moe
        o_ref[...] = _layer_norm(y) * ln_ref[0:1, :] + ln_ref[1:2, :]


def _combine(tile_rows, res, x1, mod, ln2, cap):
    bn, length, _ = x1.shape
    ne = res.shape[1]
    assert cap % SCATTER_UNROLL == 0
    tok = lambda b, s, idx: (b, jnp.maximum(s - ne, 0), 0)
    return pl.pallas_call(
        functools.partial(_combine_kernel, cap=cap, ne=ne),
        grid_spec=pltpu.PrefetchScalarGridSpec(
            num_scalar_prefetch=1,
            grid=(bn, ne + length // FIN_TILE),
            in_specs=[pl.BlockSpec((None, None, cap * ROW_CHUNKS, LANES),
                                   lambda b, s, idx: (b, jnp.minimum(s, ne - 1), 0, 0)),
                      pl.BlockSpec((None, FIN_TILE, D_MODEL), tok),
                      pl.BlockSpec((None, 6, D_MODEL), lambda b, s, idx: (b, 0, 0)),
                      pl.BlockSpec((2, D_MODEL), lambda b, s, idx: (0, 0))],
            out_specs=pl.BlockSpec((None, FIN_TILE, D_MODEL), tok),
            scratch_shapes=[pltpu.VMEM((length * ROW_CHUNKS, LANES), F32)]),
        out_shape=jax.ShapeDtypeStruct((bn, length, D_MODEL), F32),
        compiler_params=_params(("arbitrary", "arbitrary"), VMEM_LIMIT_BYTES),
        name="combine",
    )(tile_rows, res, x1, mod, ln2)


def kernel(x, c, ctx, c_ctx, w_ada, b_ada, w_in, gm_ws, gm_bs, s5_a_re, s5_a_im, s5_log_step, s5_b_re, s5_b_im,
           s5_c_re, s5_c_im, s5_d, s5_w_glu, s5_b_glu, w_out, ln1_g, ln1_b, w_router, moe_w_gate, moe_w_up,
           moe_w_down, ln2_g, ln2_b):
    assert w_ada.shape[0] == DEPTH == 1
    bn, length, _ = x.shape
    assert bn == SCAN_HALF and ctx.shape[1] == SCAN_T and length % TOK_TILE == 0
    cap = CAPACITY_FACTOR * length // N_EXPERTS

    cc = jnp.zeros((SUBLANES, D_MODEL), F32).at[:bn].set(c).at[bn].set(c_ctx)
    mod = _adaln(cc, w_ada[0], b_ada[0][None, :])
    mod_lat = mod[:bn].reshape(bn, 6, D_MODEL)
    mod_ctx = mod[bn:bn + 1].reshape(1, 6, D_MODEL)

    w_in_b = w_in[0].astype(BF16)
    zs, gm, wg_b = _inproj(x, mod_lat, w_in_b, gm_ws[0].astype(BF16), gm_bs[0][:, :, None], moe_w_gate[0])
    zc = _ctxproj(ctx, mod_ctx, w_in_b)

    bmat, cmat, tab = _s5_tables(s5_a_re[0], s5_a_im[0], s5_log_step[0], s5_b_re[0], s5_b_im[0],
                                 s5_c_re[0], s5_c_im[0])
    yf, yb, wd_b = _s5scan(zs, zc, bmat, cmat, tab, moe_w_down[0])

    w_rt = w_router[0].T
    w_rt_hi = w_rt.astype(BF16)
    w_rt_parts = jnp.stack([w_rt_hi, (w_rt - w_rt_hi.astype(F32)).astype(BF16)])
    x1, hp, aff = _outproj(x, zs, yf, yb, gm, mod_lat, s5_d[0].reshape(1, S5_W), s5_w_glu[0].astype(BF16),
                           s5_b_glu[0][None, :], w_out[0].astype(BF16), jnp.stack([ln1_g[0], ln1_b[0]]),
                           w_rt_parts)

    aff4 = aff.reshape(bn, N_EXPERTS, length // LANES, LANES)
    key4, offs4 = _route(aff4, cap)
    row_offs = offs4[..., 0].astype(jnp.int32)
    rows5 = (bn, N_EXPERTS, length // LANES, 1, LANES)
    idx, gate, wu_b = _compact(row_offs.reshape(-1), key4.reshape(rows5), aff4.reshape(rows5), cap, moe_w_up[0])
    tile_rows = idx.reshape(-1) * ROW_CHUNKS

    res = _experts(tile_rows, hp, gate, wg_b, wu_b, wd_b, cap)

    return _combine(tile_rows, res, x1, mod_lat, jnp.stack([ln2_g[0], ln2_b[0]]), cap)
```

```python
import functools

import jax
import jax.numpy as jnp
from jax import lax
from jax.experimental import pallas as pl
from jax.experimental.pallas import tpu as pltpu

D_MODEL = 1024
CHUNK = 128
S5_W = D_MODEL // 4
GM_W = D_MODEL - S5_W
GM_HEAD_DIM = 128
GM_HEADS = GM_W // GM_HEAD_DIM
S5_GROUP = 16
S5_GROUPS = S5_W // S5_GROUP
S5_STATE = 64
S5_STATES = S5_GROUPS * S5_STATE
IN_COLS = S5_W + 2 * GM_W
N_EXPERTS = 16
CAPACITY_FACTOR = 2
D_FF = 2816
DEPTH = 1
ALPHA = (2.0 * DEPTH) ** 0.25
LN_EPS = 1e-6

F32 = jnp.float32
BF16 = jnp.bfloat16
HIGHEST = lax.Precision.HIGHEST

LANES = 128
SUBLANES = 8
ROW_CHUNKS = D_MODEL // LANES
VMEM_LIMIT_BYTES = 58 * 1024 * 1024

TOK_TILE = 1024
FIN_TILE = 512
SCAN_T = 256
SCAN_LANES = 512
SCAN_HALF = SUBLANES // 2
ROW_TILE = 256
COMPACT_WIN = CHUNK + SUBLANES
F32_TINY = 2.0 ** -126
BRACKET_GEO_STEPS = 8
BRACKET_LIN_STEPS = 26


def _layer_norm(x):
    mu = jnp.mean(x, axis=-1, keepdims=True)
    xc = x - mu
    var = jnp.mean(xc * xc, axis=-1, keepdims=True)
    return xc * lax.rsqrt(var + LN_EPS)


def _params(semantics, vmem=None):
    return pltpu.CompilerParams(dimension_semantics=semantics, vmem_limit_bytes=vmem)


def _store_row_tiles(ref, first_token, x):
    n = x.shape[0]
    for s in range(ROW_CHUNKS):
        ref[pl.ds(first_token * ROW_CHUNKS + s, n, stride=ROW_CHUNKS), :] = x[:, s * LANES:(s + 1) * LANES]


def _load_row_tiles(ref, first_token, n):
    return jnp.concatenate(
        [ref[pl.ds(first_token * ROW_CHUNKS + s, n, stride=ROW_CHUNKS), :] for s in range(ROW_CHUNKS)], axis=1)


def _row_tile(token):
    return pl.ds(pl.multiple_of(token * ROW_CHUNKS, ROW_CHUNKS), ROW_CHUNKS)


def _tile_at(first_row):
    return pl.ds(pl.multiple_of(first_row, ROW_CHUNKS), ROW_CHUNKS)


def _cast_spec(w, n_steps, step_of):
    ne, rows, cols = w.shape
    per_expert = n_steps // ne
    assert per_expert * ne == n_steps and rows % (per_expert * 2 * SUBLANES) == 0
    return pl.BlockSpec((None, rows // per_expert, cols),
                        lambda *ids: (step_of(*ids) // per_expert, step_of(*ids) % per_expert, 0))


def _cast_block(w_ref, o_ref):
    o_ref[...] = w_ref[...].astype(BF16)


def _adaln_kernel(c_ref, w_ref, b_ref, o_ref):
    c = c_ref[...]
    a = c * jax.nn.sigmoid(c)
    o_ref[...] = jnp.dot(a, w_ref[...], precision=HIGHEST, preferred_element_type=F32) + b_ref[...]


def _adaln(cc, w_ada, b_ada):
    n = w_ada.shape[1] // D_MODEL
    return pl.pallas_call(
        _adaln_kernel,
        grid=(n,),
        in_specs=[pl.BlockSpec((SUBLANES, D_MODEL), lambda j: (0, 0)),
                  pl.BlockSpec((D_MODEL, D_MODEL), lambda j: (0, j)),
                  pl.BlockSpec((1, D_MODEL), lambda j: (0, j))],
        out_specs=pl.BlockSpec((SUBLANES, D_MODEL), lambda j: (0, j)),
        out_shape=jax.ShapeDtypeStruct((SUBLANES, w_ada.shape[1]), F32),
        compiler_params=_params(("arbitrary",)),
        name="adaln",
    )(cc, w_ada, b_ada)


def _inproj_kernel(x_ref, mod_ref, w_ref, ws_ref, bs_ref, wcast_ref, zs_ref, gm_ref, wcast_out_ref):
    _cast_block(wcast_ref, wcast_out_ref)
    h = _layer_norm(x_ref[...]) * (1.0 + mod_ref[1:2, :]) + mod_ref[0:1, :]
    z = jnp.dot(h.astype(BF16), w_ref[...], preferred_element_type=F32)
    zs_ref[...] = z[:, :S5_W]
    for g in range(GM_HEADS):
        lo = S5_W + g * GM_HEAD_DIM
        u = jax.nn.gelu(z[:, lo:lo + GM_HEAD_DIM])
        v = jax.nn.gelu(z[:, lo + GM_W:lo + GM_W + GM_HEAD_DIM])
        vn = _layer_norm(v).astype(BF16)
        wsg = ws_ref[g]
        bsg = bs_ref[g]
        for c in range(TOK_TILE // CHUNK):
            rows = slice(c * CHUNK, (c + 1) * CHUNK)
            mixed = jnp.dot(wsg, vn[rows, :], preferred_element_type=F32) + bsg
            gm_ref[rows, g * GM_HEAD_DIM:(g + 1) * GM_HEAD_DIM] = (u[rows, :] * mixed).astype(BF16)


def _inproj(x, mod, w_in, gm_ws, gm_bs, w_cast):
    bn, length, _ = x.shape
    nt = length // TOK_TILE
    cast = _cast_spec(w_cast, bn * nt, lambda b, i: b * nt + i)
    return pl.pallas_call(
        _inproj_kernel,
        grid=(bn, nt),
        in_specs=[pl.BlockSpec((None, TOK_TILE, D_MODEL), lambda b, i: (b, i, 0)),
                  pl.BlockSpec((None, 6, D_MODEL), lambda b, i: (b, 0, 0)),
                  pl.BlockSpec((D_MODEL, IN_COLS), lambda b, i: (0, 0)),
                  pl.BlockSpec((GM_HEADS, CHUNK, CHUNK), lambda b, i: (0, 0, 0)),
                  pl.BlockSpec((GM_HEADS, CHUNK, 1), lambda b, i: (0, 0, 0)),
                  cast],
        out_specs=[pl.BlockSpec((TOK_TILE, S5_W), lambda b, i: (i, b)),
                   pl.BlockSpec((None, TOK_TILE, GM_W), lambda b, i: (b, i, 0)),
                   cast],
        out_shape=[jax.ShapeDtypeStruct((length, bn * S5_W), F32),
                   jax.ShapeDtypeStruct((bn, length, GM_W), BF16),
                   jax.ShapeDtypeStruct(w_cast.shape, BF16)],
        compiler_params=_params(("arbitrary", "arbitrary"), VMEM_LIMIT_BYTES),
        name="inproj",
    )(x, mod, w_in, gm_ws, gm_bs, w_cast)


def _ctxproj_kernel(x_ref, mod_ref, w_ref, o_ref):
    h = _layer_norm(x_ref[...]) * (1.0 + mod_ref[1:2, :]) + mod_ref[0:1, :]
    o_ref[...] = jnp.dot(h.astype(BF16), w_ref[...], preferred_element_type=F32)


def _ctxproj(ctx, mod_ctx, w_in):
    bn, clen, _ = ctx.shape
    return pl.pallas_call(
        _ctxproj_kernel,
        grid=(bn,),
        in_specs=[pl.BlockSpec((None, clen, D_MODEL), lambda b: (b, 0, 0)),
                  pl.BlockSpec((None, 6, D_MODEL), lambda b: (0, 0, 0)),
                  pl.BlockSpec((D_MODEL, S5_W), lambda b: (0, 0))],
        out_specs=pl.BlockSpec((clen, S5_W), lambda b: (0, b)),
        out_shape=jax.ShapeDtypeStruct((clen, bn * S5_W), F32),
        compiler_params=_params(("arbitrary",)),
        name="ctxproj",
    )(ctx, mod_ctx, w_in)


def _s5_tables(a_re, a_im, log_step, b_re, b_im, c_re, c_im):
    step = jnp.exp(log_step)[..., None]
    mag = jnp.exp(a_re * step)
    lb_re = mag * jnp.cos(a_im * step)
    lb_im = mag * jnp.sin(a_im * step)
    den = a_re * a_re + a_im * a_im
    q_re = ((lb_re - 1.0) * a_re + lb_im * a_im) / den
    q_im = (lb_im * a_re - (lb_re - 1.0) * a_im) / den
    bb_re = q_re[..., None] * b_re - q_im[..., None] * b_im
    bb_im = q_re[..., None] * b_im + q_im[..., None] * b_re
    eye = jnp.eye(S5_GROUPS, dtype=F32)

    def in_block(m):
        return jnp.einsum('dgph,gk->dghkp', m, eye).reshape(2, S5_W, S5_STATES)

    def out_block(m):
        return jnp.einsum('dghp,gk->dgpkh', m, eye).reshape(2, S5_STATES, S5_W)

    bmat = jnp.concatenate([in_block(bb_re), in_block(bb_im)], axis=2).astype(BF16)
    cmat = jnp.concatenate([out_block(c_re), out_block(-c_im)], axis=1).astype(BF16)
    lam = jnp.concatenate([lb_re.reshape(2, 1, S5_STATES), lb_im.reshape(2, 1, S5_STATES)], axis=2)
    tab = jnp.concatenate([jnp.repeat(lam[0], SCAN_HALF, axis=0), jnp.repeat(lam[1], SCAN_HALF, axis=0)], axis=0)
    return bmat, cmat, tab


def _scan_tiles(df_ref, db_ref, s_ref, tab_ref):
    n_tiles = df_ref.shape[1] // SUBLANES
    slabs_per_block = SCAN_LANES // LANES
    top = lax.broadcasted_iota(jnp.int32, (SUBLANES, SCAN_LANES), 0) < SCAN_HALF

    def advance(l_re, l_im, s_re, s_im, x_re, x_im):
        return x_re + l_re * s_re - l_im * s_im, x_im + l_re * s_im + l_im * s_re

    def swap(x):
        return pltpu.roll(x, SCAN_HALF, 0)

    for lb in range(S5_STATES // SCAN_LANES):
        re_cols = pl.ds(lb * SCAN_LANES, SCAN_LANES)
        im_cols = pl.ds(S5_STATES + lb * SCAN_LANES, SCAN_LANES)
        re_slabs = [lb * slabs_per_block + k for k in range(slabs_per_block)]
        im_slabs = [S5_STATES // LANES + s for s in re_slabs]
        l_re, l_im = tab_ref[:, re_cols], tab_ref[:, im_cols]

        def body(i, s, re_slabs=re_slabs, im_slabs=im_slabs, l_re=l_re, l_im=l_im):
            rows_f = pl.ds(pl.multiple_of(i * SUBLANES, SUBLANES), SUBLANES)
            rows_b = pl.ds(pl.multiple_of((n_tiles - 1 - i) * SUBLANES, SUBLANES), SUBLANES)
            out = []
            for slabs, s_part in ((re_slabs, 0), (im_slabs, 1)):
                f = jnp.concatenate([df_ref[k, rows_f, :] for k in slabs], axis=1)
                b = jnp.concatenate([db_ref[k, rows_b, :] for k in slabs], axis=1)
                out.append((jnp.where(top, f, b), swap(jnp.where(top, b, f))))
            (x0_re, x1_re), (x0_im, x1_im) = out
            y0_re, y0_im = advance(l_re, l_im, s[0], s[1], x0_re, x0_im)
            y1_re, y1_im = advance(l_re, l_im, y0_re, y0_im, x1_re, x1_im)
            for slabs, y0, y1 in ((re_slabs, y0_re, y1_re), (im_slabs, y0_im, y1_im)):
                r1 = swap(y1)
                f_new = jnp.where(top, y0, r1)
                b_new = jnp.where(top, r1, y0)
                for n, k in enumerate(slabs):
                    df_ref[k, rows_f, :] = f_new[:, n * LANES:(n + 1) * LANES]
                    db_ref[k, rows_b, :] = b_new[:, n * LANES:(n + 1) * LANES]
            return y1_re, y1_im

        s_re, s_im = lax.fori_loop(0, n_tiles, body, (s_ref[:, re_cols], s_ref[:, im_cols]))
        s_ref[:, re_cols] = s_re
        s_ref[:, im_cols] = s_im


def _s5scan_kernel(uf_ref, ub_ref, uc_ref, bmat_ref, cmat_ref, tab_ref, wcast_ref, yf_ref, yb_ref, wcast_out_ref,
                   df_ref, db_ref, s_ref):
    _cast_block(wcast_ref, wcast_out_ref)
    j = pl.program_id(0)
    n_slabs = 2 * S5_STATES // LANES

    def batch_rows(b):
        return pl.ds(b, SCAN_T, stride=SCAN_HALF)

    def drive(u_ref, d_ref, direction):
        u = jnp.concatenate([u_ref[:, b * S5_W:(b + 1) * S5_W] for b in range(SCAN_HALF)], axis=0)
        d = jnp.dot(u.astype(BF16), bmat_ref[direction], preferred_element_type=F32)
        for b in range(SCAN_HALF):
            for k in range(n_slabs):
                d_ref[k, batch_rows(b), :] = d[b * SCAN_T:(b + 1) * SCAN_T, k * LANES:(k + 1) * LANES]

    def readout(d_ref, y_ref, direction):
        s = jnp.concatenate(
            [jnp.concatenate([d_ref[k, batch_rows(b), :] for k in range(n_slabs)], axis=1)
             for b in range(SCAN_HALF)], axis=0)
        y = jnp.dot(s.astype(BF16), cmat_ref[direction], preferred_element_type=F32)
        for b in range(SCAN_HALF):
            y_ref[:, b * S5_W:(b + 1) * S5_W] = y[b * SCAN_T:(b + 1) * SCAN_T, :]

    @pl.when(j == 0)
    def _context():
        s_ref[...] = jnp.zeros_like(s_ref)
        drive(uc_ref, df_ref, 0)
        drive(uc_ref, db_ref, 1)
        _scan_tiles(df_ref, db_ref, s_ref, tab_ref)

    drive(uf_ref, df_ref, 0)
    drive(ub_ref, db_ref, 1)
    _scan_tiles(df_ref, db_ref, s_ref, tab_ref)
    readout(df_ref, yf_ref, 0)
    readout(db_ref, yb_ref, 1)


def _s5scan(u_lat, u_ctx, bmat, cmat, tab, w_cast):
    length, cols = u_lat.shape
    n = length // SCAN_T
    assert u_ctx.shape == (SCAN_T, cols) and cols == SCAN_HALF * S5_W
    slab = pltpu.VMEM((2 * S5_STATES // LANES, SCAN_T * SCAN_HALF, LANES), F32)
    cast = _cast_spec(w_cast, n, lambda j: j)
    return pl.pallas_call(
        _s5scan_kernel,
        grid=(n,),
        in_specs=[pl.BlockSpec((SCAN_T, cols), lambda j: (j, 0)),
                  pl.BlockSpec((SCAN_T, cols), lambda j: (n - 1 - j, 0)),
                  pl.BlockSpec((SCAN_T, cols), lambda j: (0, 0)),
                  pl.BlockSpec((2, S5_W, 2 * S5_STATES), lambda j: (0, 0, 0)),
                  pl.BlockSpec((2, 2 * S5_STATES, S5_W), lambda j: (0, 0, 0)),
                  pl.BlockSpec((SUBLANES, 2 * S5_STATES), lambda j: (0, 0)),
                  cast],
        out_specs=[pl.BlockSpec((SCAN_T, cols), lambda j: (j, 0)),
                   pl.BlockSpec((SCAN_T, cols), lambda j: (n - 1 - j, 0)),
                   cast],
        out_shape=[jax.ShapeDtypeStruct((length, cols), F32)] * 2 + [jax.ShapeDtypeStruct(w_cast.shape, BF16)],
        scratch_shapes=[slab, slab, pltpu.VMEM((SUBLANES, 2 * S5_STATES), F32)],
        compiler_params=_params(("arbitrary",), VMEM_LIMIT_BYTES),
        name="s5scan",
    )(u_lat, u_lat, u_ctx, bmat, cmat, tab, w_cast)


def _outproj_kernel(x_ref, zs_ref, yf_ref, yb_ref, gm_ref, mod_ref, d_ref, wglu_ref, bglu_ref, wout_ref,
                    ln_ref, wr_ref, x1_ref, hp_ref, aff_ref):
    y = d_ref[...] * zs_ref[...] + yf_ref[...] + yb_ref[...]
    g = jax.nn.gelu(y)
    s5 = g * jax.nn.sigmoid(jnp.dot(g.astype(BF16), wglu_ref[...], preferred_element_type=F32) + bglu_ref[...])
    mix = (jnp.dot(s5.astype(BF16), wout_ref[0:S5_W, :], preferred_element_type=F32)
           + jnp.dot(gm_ref[...], wout_ref[S5_W:, :], preferred_element_type=F32))
    x1 = _layer_norm(ALPHA * x_ref[...] + mod_ref[2:3, :] * mix) * ln_ref[0:1, :] + ln_ref[1:2, :]
    x1_ref[...] = x1
    h = _layer_norm(x1) * (1.0 + mod_ref[4:5, :]) + mod_ref[3:4, :]
    h_hi = h.astype(BF16)
    h_lo = (h - h_hi.astype(F32)).astype(BF16)
    contract_last = (((1,), (1,)), ((), ()))
    logits = (lax.dot_general(wr_ref[0], h_hi, contract_last, preferred_element_type=F32)
              + lax.dot_general(wr_ref[0], h_lo, contract_last, preferred_element_type=F32)
              + lax.dot_general(wr_ref[1], h_hi, contract_last, preferred_element_type=F32))
    ex = jnp.exp(logits - jnp.max(logits, axis=0, keepdims=True))
    aff_ref[...] = ex / jnp.sum(ex, axis=0, keepdims=True)
    _store_row_tiles(hp_ref, 0, h)


def _outproj(x, zs, yf, yb, gm, mod, d, w_glu, b_glu, w_out, ln1, w_rt):
    bn, length, _ = x.shape
    tok = lambda b, i: (b, i, 0)
    s5 = lambda b, i: (i, b)
    const2 = lambda b, i: (0, 0)
    return pl.pallas_call(
        _outproj_kernel,
        grid=(bn, length // TOK_TILE),
        in_specs=[pl.BlockSpec((None, TOK_TILE, D_MODEL), tok),
                  pl.BlockSpec((TOK_TILE, S5_W), s5),
                  pl.BlockSpec((TOK_TILE, S5_W), s5),
                  pl.BlockSpec((TOK_TILE, S5_W), s5),
                  pl.BlockSpec((None, TOK_TILE, GM_W), tok),
                  pl.BlockSpec((None, 6, D_MODEL), lambda b, i: (b, 0, 0)),
                  pl.BlockSpec((1, S5_W), const2),
                  pl.BlockSpec((S5_W, S5_W), const2),
                  pl.BlockSpec((1, S5_W), const2),
                  pl.BlockSpec((D_MODEL, D_MODEL), const2),
                  pl.BlockSpec((2, D_MODEL), const2),
                  pl.BlockSpec((2, N_EXPERTS, D_MODEL), lambda b, i: (0, 0, 0))],
        out_specs=[pl.BlockSpec((None, TOK_TILE, D_MODEL), tok),
                   pl.BlockSpec((None, TOK_TILE * ROW_CHUNKS, LANES), tok),
                   pl.BlockSpec((None, N_EXPERTS, TOK_TILE), lambda b, i: (b, 0, i))],
        out_shape=[jax.ShapeDtypeStruct((bn, length, D_MODEL), F32),
                   jax.ShapeDtypeStruct((bn, length * ROW_CHUNKS, LANES), F32),
                   jax.ShapeDtypeStruct((bn, N_EXPERTS, length), F32)],
        compiler_params=_params(("arbitrary", "arbitrary"), VMEM_LIMIT_BYTES),
        name="outproj",
    )(x, zs, yf, yb, gm, mod, d, w_glu, b_glu, w_out, ln1, w_rt)


def _route_kernel(aff_ref, key_ref, offs_ref, *, cap):
    aff = aff_ref[...]
    ne, nr, _ = aff.shape

    def count(mask):
        return jnp.sum(jnp.sum(mask.astype(F32), axis=2, keepdims=True), axis=1, keepdims=True)

    normal = count(aff >= F32_TINY) >= cap
    lo = jnp.where(normal, F32_TINY, 0.0)
    hi = jnp.where(normal, 2.0, F32_TINY)
    for it in range(BRACKET_GEO_STEPS + BRACKET_LIN_STEPS):
        mid = 0.5 * (lo + hi)
        if it < BRACKET_GEO_STEPS:
            mid = jnp.where(normal, jnp.sqrt(lo * hi), mid)
        ok = count(aff >= mid) >= cap
        lo = jnp.where(ok, mid, lo)
        hi = jnp.where(ok, hi, mid)
    thr = jnp.min(jnp.min(jnp.where(aff >= lo, aff, 2.0), axis=2, keepdims=True), axis=1, keepdims=True)
    above = aff > thr
    tied = aff == thr
    need = cap - count(above)

    kk = lax.broadcasted_iota(jnp.int32, (LANES, LANES), 0)
    ll = lax.broadcasted_iota(jnp.int32, (LANES, LANES), 1)
    upper = (kk <= ll).astype(BF16)
    ones = jnp.ones((LANES, LANES), BF16)
    ri = lax.broadcasted_iota(jnp.int32, (nr, nr), 0)
    rj = lax.broadcasted_iota(jnp.int32, (nr, nr), 1)
    lower = (rj < ri).astype(BF16)

    def exclusive_prefix(mask):
        m = mask.astype(BF16).reshape(ne * nr, LANES)
        in_row = jnp.dot(m, upper, preferred_element_type=F32).reshape(ne, nr, LANES)
        row_tot = jnp.dot(m, ones, preferred_element_type=F32).reshape(ne, nr, LANES)
        row_off = jnp.stack([jnp.dot(lower, row_tot[e].astype(BF16), preferred_element_type=F32)
                             for e in range(ne)])
        return in_row - mask.astype(F32) + row_off, row_off

    tie_rank, _ = exclusive_prefix(tied)
    sel = above | (tied & (tie_rank < need))
    pos, row_off = exclusive_prefix(sel)
    key_ref[...] = jnp.where(sel, pos, -1.0)
    offs_ref[...] = row_off


def _route(aff4, cap):
    bn, ne, nr, _ = aff4.shape
    spec = pl.BlockSpec((None, ne, nr, LANES), lambda b: (b, 0, 0, 0))
    return pl.pallas_call(
        functools.partial(_route_kernel, cap=cap),
        grid=(bn,),
        in_specs=[spec],
        out_specs=[spec, spec],
        out_shape=[jax.ShapeDtypeStruct(aff4.shape, F32)] * 2,
        compiler_params=_params(("arbitrary",), VMEM_LIMIT_BYTES),
        name="route",
    )(aff4)


def _compact_kernel(offs_sref, key_ref, aff_ref, wcast_ref, idx_ref, gate_ref, wcast_out_ref, acc_i, acc_g, *, cap):
    _cast_block(wcast_ref, wcast_out_ref)
    be = pl.program_id(0) * pl.num_programs(1) + pl.program_id(1)
    nr = key_ref.shape[0]
    acc_i[...] = jnp.zeros_like(acc_i)
    acc_g[...] = jnp.zeros_like(acc_g)
    slot = lax.broadcasted_iota(jnp.int32, (COMPACT_WIN, LANES), 0)
    lane = lax.broadcasted_iota(jnp.int32, (1, LANES), 1)

    def body(r, carry):
        off = offs_sref[be * nr + r]
        base = pl.multiple_of((off // SUBLANES) * SUBLANES, SUBLANES)
        win = pl.ds(base, COMPACT_WIN)
        hit = (slot + base) == key_ref[r].astype(jnp.int32)
        tok = (lane + r * LANES).astype(F32)
        acc_i[win, :] += jnp.where(hit, tok, 0.0)
        acc_g[win, :] += jnp.where(hit, aff_ref[r], 0.0)
        return carry

    lax.fori_loop(0, nr, body, 0)
    ones = jnp.ones((SUBLANES, LANES), F32)
    idx_rows = lax.dot_general(ones, acc_i[0:cap, :], (((1,), (1,)), ((), ())), precision=HIGHEST,
                               preferred_element_type=F32)
    idx_ref[...] = idx_rows[0:1, :].astype(jnp.int32)
    gate_ref[...] = jnp.sum(acc_g[0:cap, :], axis=1, keepdims=True)


def _compact(row_offs, key5, aff5, cap, w_cast):
    bn, ne, nr, _, _ = key5.shape
    spec_in = pl.BlockSpec((None, None, nr, 1, LANES), lambda b, e, offs: (b, e, 0, 0, 0))
    spec_out = pl.BlockSpec((None, None, cap, 1), lambda b, e, offs: (b, e, 0, 0))
    cast = _cast_spec(w_cast, bn * ne, lambda b, e, offs: b * ne + e)
    return pl.pallas_call(
        functools.partial(_compact_kernel, cap=cap),
        grid_spec=pltpu.PrefetchScalarGridSpec(
            num_scalar_prefetch=1,
            grid=(bn, ne),
            in_specs=[spec_in, spec_in, cast],
            out_specs=[pl.BlockSpec((None, None, 1, cap), lambda b, e, offs: (b, e, 0, 0)), spec_out, cast],
            scratch_shapes=[pltpu.VMEM((cap + COMPACT_WIN, LANES), F32),
                            pltpu.VMEM((cap + COMPACT_WIN, LANES), F32)]),
        out_shape=[jax.ShapeDtypeStruct((bn, ne, 1, cap), jnp.int32),
                   jax.ShapeDtypeStruct((bn, ne, cap, 1), F32),
                   jax.ShapeDtypeStruct(w_cast.shape, BF16)],
        compiler_params=_params(("arbitrary", "arbitrary"), VMEM_LIMIT_BYTES),
        name="compact",
    )(row_offs, key5, aff5, w_cast)


def _experts_kernel(rows_sref, h_hbm, gate_ref, wg_ref, wu_ref, wd_ref, res_ref, xg_ref, sem, *, cap):
    ne = pl.num_programs(1)
    step = pl.program_id(0) * ne + pl.program_id(1)
    last = pl.num_programs(0) * ne - 1
    slot = step % 2
    nxt = jnp.minimum(step + 1, last)

    def start_row(c, of_step, into_slot):
        pltpu.make_async_copy(h_hbm.at[of_step // ne, _tile_at(rows_sref[of_step * cap + c]), :],
                              xg_ref.at[into_slot, _row_tile(c), :], sem.at[into_slot]).start()

    def wait_rows(of_slot):
        pltpu.make_async_copy(h_hbm.at[0, pl.ds(0, cap * ROW_CHUNKS), :], xg_ref.at[of_slot],
                              sem.at[of_slot]).wait()

    @pl.when(step == 0)
    def _first_rows():
        def body(c, carry):
            start_row(c, step, slot)
            return carry

        lax.fori_loop(0, cap, body, 0)

    wait_rows(slot)
    for m in range(cap // ROW_TILE):
        rows = slice(m * ROW_TILE, (m + 1) * ROW_TILE)
        for c in range(rows.start, rows.stop):
            start_row(c, nxt, 1 - slot)
        xr = _load_row_tiles(xg_ref.at[slot], rows.start, ROW_TILE).astype(BF16)
        hg = jnp.dot(xr, wg_ref[...], preferred_element_type=F32)
        hu = jnp.dot(xr, wu_ref[...], preferred_element_type=F32)
        hid = (hg * jax.nn.sigmoid(hg) * hu).astype(BF16)
        out = jnp.dot(hid, wd_ref[...], preferred_element_type=F32)
        _store_row_tiles(res_ref, rows.start, out * gate_ref[rows, :])

    @pl.when(step == last)
    def _drain():
        wait_rows(1 - slot)


def _experts(tile_rows, h, gate, wg, wu, wd, cap):
    bn = h.shape[0]
    ne = wg.shape[0]
    return pl.pallas_call(
        functools.partial(_experts_kernel, cap=cap),
        grid_spec=pltpu.PrefetchScalarGridSpec(
            num_scalar_prefetch=1,
            grid=(bn, ne),
            in_specs=[pl.BlockSpec(memory_space=pl.ANY),
                      pl.BlockSpec((None, None, cap, 1), lambda b, e, rows: (b, e, 0, 0)),
                      pl.BlockSpec((None, D_MODEL, D_FF), lambda b, e, rows: (e, 0, 0)),
                      pl.BlockSpec((None, D_MODEL, D_FF), lambda b, e, rows: (e, 0, 0)),
                      pl.BlockSpec((None, D_FF, D_MODEL), lambda b, e, rows: (e, 0, 0))],
            out_specs=pl.BlockSpec((None, None, cap * ROW_CHUNKS, LANES), lambda b, e, rows: (b, e, 0, 0)),
            scratch_shapes=[pltpu.VMEM((2, cap * ROW_CHUNKS, LANES), F32),
                            pltpu.SemaphoreType.DMA((2,))]),
        out_shape=jax.ShapeDtypeStruct((bn, ne, cap * ROW_CHUNKS, LANES), F32),
        compiler_params=_params(("arbitrary", "arbitrary"), VMEM_LIMIT_BYTES),
        name="experts",
    )(tile_rows, h, gate, wg, wu, wd)


SCATTER_UNROLL = 16


def _combine_kernel(rows_sref, res_ref, x1_ref, mod_ref, ln_ref, o_ref, acc_ref, *, cap, ne):
    b = pl.program_id(0)
    step = pl.program_id(1)

    @pl.when(step == 0)
    def _init():
        acc_ref[...] = jnp.zeros_like(acc_ref)

    @pl.when(step < ne)
    def _scatter():
        base = (b * ne + step) * cap

        def body(i, carry):
            first = i * SCATTER_UNROLL
            toks = [_tile_at(rows_sref[base + first + k]) for k in range(SCATTER_UNROLL)]
            sums = [acc_ref[toks[k], :] + res_ref[_row_tile(first + k), :] for k in range(SCATTER_UNROLL)]
            for k in range(SCATTER_UNROLL):
                acc_ref[toks[k], :] = sums[k]
            return carry

        lax.fori_loop(0, cap // SCATTER_UNROLL, body, 0)

    @pl.when(step >= ne)
    def _finish():
        moe = _load_row_tiles(acc_ref, (step - ne) * FIN_TILE, FIN_TILE)
        y = ALPHA * x1_ref[...] + mod_ref[5:6, :] * moe
        o_ref[...] = _layer_norm(y) * ln_ref[0:1, :] + ln_ref[1:2, :]


def _combine(tile_rows, res, x1, mod, ln2, cap):
    bn, length, _ = x1.shape
    ne = res.shape[1]
    assert cap % SCATTER_UNROLL == 0
    tok = lambda b, s, idx: (b, jnp.maximum(s - ne, 0), 0)
    return pl.pallas_call(
        functools.partial(_combine_kernel, cap=cap, ne=ne),
        grid_spec=pltpu.PrefetchScalarGridSpec(
            num_scalar_prefetch=1,
            grid=(bn, ne + length // FIN_TILE),
            in_specs=[pl.BlockSpec((None, None, cap * ROW_CHUNKS, LANES),
                                   lambda b, s, idx: (b, jnp.minimum(s, ne - 1), 0, 0)),
                      pl.BlockSpec((None, FIN_TILE, D_MODEL), tok),
                      pl.BlockSpec((None, 6, D_MODEL), lambda b, s, idx: (b, 0, 0)),
                      pl.BlockSpec((2, D_MODEL), lambda b, s, idx: (0, 0))],
            out_specs=pl.BlockSpec((None, FIN_TILE, D_MODEL), tok),
            scratch_shapes=[pltpu.VMEM((length * ROW_CHUNKS, LANES), F32)]),
        out_shape=jax.ShapeDtypeStruct((bn, length, D_MODEL), F32),
        compiler_params=_params(("arbitrary", "arbitrary"), VMEM_LIMIT_BYTES),
        name="combine",
    )(tile_rows, res, x1, mod, ln2)


def kernel(x, c, ctx, c_ctx, w_ada, b_ada, w_in, gm_ws, gm_bs, s5_a_re, s5_a_im, s5_log_step, s5_b_re, s5_b_im,
           s5_c_re, s5_c_im, s5_d, s5_w_glu, s5_b_glu, w_out, ln1_g, ln1_b, w_router, moe_w_gate, moe_w_up,
           moe_w_down, ln2_g, ln2_b):
    assert w_ada.shape[0] == DEPTH == 1
    bn, length, _ = x.shape
    assert bn == SCAN_HALF and ctx.shape[1] == SCAN_T and length % TOK_TILE == 0
    cap = CAPACITY_FACTOR * length // N_EXPERTS

    cc = jnp.zeros((SUBLANES, D_MODEL), F32).at[:bn].set(c).at[bn].set(c_ctx)
    mod = _adaln(cc, w_ada[0], b_ada[0][None, :])
    mod_lat = mod[:bn].reshape(bn, 6, D_MODEL)
    mod_ctx = mod[bn:bn + 1].reshape(1, 6, D_MODEL)

    w_in_b = w_in[0].astype(BF16)
    zs, gm, wg_b = _inproj(x, mod_lat, w_in_b, gm_ws[0].astype(BF16), gm_bs[0][:, :, None], moe_w_gate[0])
    zc = _ctxproj(ctx, mod_ctx, w_in_b)

    bmat, cmat, tab = _s5_tables(s5_a_re[0], s5_a_im[0], s5_log_step[0], s5_b_re[0], s5_b_im[0],
                                 s5_c_re[0], s5_c_im[0])
    yf, yb, wd_b = _s5scan(zs, zc, bmat, cmat, tab, moe_w_down[0])

    w_rt = w_router[0].T
    w_rt_hi = w_rt.astype(BF16)
    w_rt_parts = jnp.stack([w_rt_hi, (w_rt - w_rt_hi.astype(F32)).astype(BF16)])
    x1, hp, aff = _outproj(x, zs, yf, yb, gm, mod_lat, s5_d[0].reshape(1, S5_W), s5_w_glu[0].astype(BF16),
                           s5_b_glu[0][None, :], w_out[0].astype(BF16), jnp.stack([ln1_g[0], ln1_b[0]]),
                           w_rt_parts)

    aff4 = aff.reshape(bn, N_EXPERTS, length // LANES, LANES)
    key4, offs4 = _route(aff4, cap)
    row_offs = offs4[..., 0].astype(jnp.int32)
    rows5 = (bn, N_EXPERTS, length // LANES, 1, LANES)
    idx, gate, wu_b = _compact(row_offs.reshape(-1), key4.reshape(rows5), aff4.reshape(rows5), cap, moe_w_up[0])
    tile_rows = idx.reshape(-1) * ROW_CHUNKS

    res = _experts(tile_rows, hp, gate, wg_b, wu_b, wd_b, cap)

    return _combine(tile_rows, res, x1, mod_lat, jnp.stack([ln2_g[0], ln2_b[0]]), cap)
```

```python
import functools

import jax
import jax.numpy as jnp
from jax import lax
from jax.experimental import pallas as pl
from jax.experimental.pallas import tpu as pltpu

D_MODEL = 1024
CHUNK = 128
S5_W = D_MODEL // 4
GM_W = D_MODEL - S5_W
GM_HEAD_DIM = 128
GM_HEADS = GM_W // GM_HEAD_DIM
S5_GROUP = 16
S5_GROUPS = S5_W // S5_GROUP
S5_STATE = 64
S5_STATES = S5_GROUPS * S5_STATE
IN_COLS = S5_W + 2 * GM_W
N_EXPERTS = 16
CAPACITY_FACTOR = 2
D_FF = 2816
DEPTH = 1
ALPHA = (2.0 * DEPTH) ** 0.25
LN_EPS = 1e-6

F32 = jnp.float32
BF16 = jnp.bfloat16
HIGHEST = lax.Precision.HIGHEST

LANES = 128
SUBLANES = 8
ROW_CHUNKS = D_MODEL // LANES
VMEM_LIMIT_BYTES = 58 * 1024 * 1024

TOK_TILE = 1024
FIN_TILE = 512
SCAN_T = 256
SCAN_LANES = 1024
SCAN_HALF = SUBLANES // 2
ROW_TILE = 256
COMPACT_WIN = CHUNK + SUBLANES
F32_TINY = 2.0 ** -126
BRACKET_GEO_STEPS = 8
BRACKET_LIN_STEPS = 26


def _layer_norm(x):
    mu = jnp.mean(x, axis=-1, keepdims=True)
    xc = x - mu
    var = jnp.mean(xc * xc, axis=-1, keepdims=True)
    return xc * lax.rsqrt(var + LN_EPS)


def _params(semantics, vmem=None):
    return pltpu.CompilerParams(dimension_semantics=semantics, vmem_limit_bytes=vmem)


def _store_row_tiles(ref, first_token, x):
    n = x.shape[0]
    for s in range(ROW_CHUNKS):
        ref[pl.ds(first_token * ROW_CHUNKS + s, n, stride=ROW_CHUNKS), :] = x[:, s * LANES:(s + 1) * LANES]


def _load_row_tiles(ref, first_token, n):
    return jnp.concatenate(
        [ref[pl.ds(first_token * ROW_CHUNKS + s, n, stride=ROW_CHUNKS), :] for s in range(ROW_CHUNKS)], axis=1)


def _row_tile(token):
    return pl.ds(pl.multiple_of(token * ROW_CHUNKS, ROW_CHUNKS), ROW_CHUNKS)


def _tile_at(first_row):
    return pl.ds(pl.multiple_of(first_row, ROW_CHUNKS), ROW_CHUNKS)


def _cast_spec(w, n_steps, step_of):
    ne, rows, cols = w.shape
    per_expert = n_steps // ne
    assert per_expert * ne == n_steps and rows % (per_expert * 2 * SUBLANES) == 0
    return pl.BlockSpec((None, rows // per_expert, cols),
                        lambda *ids: (step_of(*ids) // per_expert, step_of(*ids) % per_expert, 0))


def _cast_block(w_ref, o_ref):
    o_ref[...] = w_ref[...].astype(BF16)


def _adaln_kernel(c_ref, w_ref, b_ref, o_ref):
    c = c_ref[...]
    a = c * jax.nn.sigmoid(c)
    o_ref[...] = jnp.dot(a, w_ref[...], precision=HIGHEST, preferred_element_type=F32) + b_ref[...]


def _adaln(cc, w_ada, b_ada):
    n = w_ada.shape[1] // D_MODEL
    return pl.pallas_call(
        _adaln_kernel,
        grid=(n,),
        in_specs=[pl.BlockSpec((SUBLANES, D_MODEL), lambda j: (0, 0)),
                  pl.BlockSpec((D_MODEL, D_MODEL), lambda j: (0, j)),
                  pl.BlockSpec((1, D_MODEL), lambda j: (0, j))],
        out_specs=pl.BlockSpec((SUBLANES, D_MODEL), lambda j: (0, j)),
        out_shape=jax.ShapeDtypeStruct((SUBLANES, w_ada.shape[1]), F32),
        compiler_params=_params(("arbitrary",)),
        name="adaln",
    )(cc, w_ada, b_ada)


def _inproj_kernel(x_ref, mod_ref, w_ref, ws_ref, bs_ref, wcast_ref, zs_ref, gm_ref, wcast_out_ref):
    _cast_block(wcast_ref, wcast_out_ref)
    h = _layer_norm(x_ref[...]) * (1.0 + mod_ref[1:2, :]) + mod_ref[0:1, :]
    z = jnp.dot(h.astype(BF16), w_ref[...], preferred_element_type=F32)
    zs_ref[...] = z[:, :S5_W]
    for g in range(GM_HEADS):
        lo = S5_W + g * GM_HEAD_DIM
        u = jax.nn.gelu(z[:, lo:lo + GM_HEAD_DIM])
        v = jax.nn.gelu(z[:, lo + GM_W:lo + GM_W + GM_HEAD_DIM])
        vn = _layer_norm(v).astype(BF16)
        wsg = ws_ref[g]
        bsg = bs_ref[g]
        for c in range(TOK_TILE // CHUNK):
            rows = slice(c * CHUNK, (c + 1) * CHUNK)
            mixed = jnp.dot(wsg, vn[rows, :], preferred_element_type=F32) + bsg
            gm_ref[rows, g * GM_HEAD_DIM:(g + 1) * GM_HEAD_DIM] = (u[rows, :] * mixed).astype(BF16)


def _inproj(x, mod, w_in, gm_ws, gm_bs, w_cast):
    bn, length, _ = x.shape
    nt = length // TOK_TILE
    cast = _cast_spec(w_cast, bn * nt, lambda b, i: b * nt + i)
    return pl.pallas_call(
        _inproj_kernel,
        grid=(bn, nt),
        in_specs=[pl.BlockSpec((None, TOK_TILE, D_MODEL), lambda b, i: (b, i, 0)),
                  pl.BlockSpec((None, 6, D_MODEL), lambda b, i: (b, 0, 0)),
                  pl.BlockSpec((D_MODEL, IN_COLS), lambda b, i: (0, 0)),
                  pl.BlockSpec((GM_HEADS, CHUNK, CHUNK), lambda b, i: (0, 0, 0)),
                  pl.BlockSpec((GM_HEADS, CHUNK, 1), lambda b, i: (0, 0, 0)),
                  cast],
        out_specs=[pl.BlockSpec((TOK_TILE, S5_W), lambda b, i: (i, b)),
                   pl.BlockSpec((None, TOK_TILE, GM_W), lambda b, i: (b, i, 0)),
                   cast],
        out_shape=[jax.ShapeDtypeStruct((length, bn * S5_W), F32),
                   jax.ShapeDtypeStruct((bn, length, GM_W), BF16),
                   jax.ShapeDtypeStruct(w_cast.shape, BF16)],
        compiler_params=_params(("arbitrary", "arbitrary"), VMEM_LIMIT_BYTES),
        name="inproj",
    )(x, mod, w_in, gm_ws, gm_bs, w_cast)


def _ctxproj_kernel(x_ref, mod_ref, w_ref, o_ref):
    h = _layer_norm(x_ref[...]) * (1.0 + mod_ref[1:2, :]) + mod_ref[0:1, :]
    o_ref[...] = jnp.dot(h.astype(BF16), w_ref[...], preferred_element_type=F32)


def _ctxproj(ctx, mod_ctx, w_in):
    bn, clen, _ = ctx.shape
    return pl.pallas_call(
        _ctxproj_kernel,
        grid=(bn,),
        in_specs=[pl.BlockSpec((None, clen, D_MODEL), lambda b: (b, 0, 0)),
                  pl.BlockSpec((None, 6, D_MODEL), lambda b: (0, 0, 0)),
                  pl.BlockSpec((D_MODEL, S5_W), lambda b: (0, 0))],
        out_specs=pl.BlockSpec((clen, S5_W), lambda b: (0, b)),
        out_shape=jax.ShapeDtypeStruct((clen, bn * S5_W), F32),
        compiler_params=_params(("arbitrary",)),
        name="ctxproj",
    )(ctx, mod_ctx, w_in)


def _s5_tables(a_re, a_im, log_step, b_re, b_im, c_re, c_im):
    step = jnp.exp(log_step)[..., None]
    mag = jnp.exp(a_re * step)
    lb_re = mag * jnp.cos(a_im * step)
    lb_im = mag * jnp.sin(a_im * step)
    den = a_re * a_re + a_im * a_im
    q_re = ((lb_re - 1.0) * a_re + lb_im * a_im) / den
    q_im = (lb_im * a_re - (lb_re - 1.0) * a_im) / den
    bb_re = q_re[..., None] * b_re - q_im[..., None] * b_im
    bb_im = q_re[..., None] * b_im + q_im[..., None] * b_re
    eye = jnp.eye(S5_GROUPS, dtype=F32)

    def in_block(m):
        return jnp.einsum('dgph,gk->dghkp', m, eye).reshape(2, S5_W, S5_STATES)

    def out_block(m):
        return jnp.einsum('dghp,gk->dgpkh', m, eye).reshape(2, S5_STATES, S5_W)

    bmat = jnp.concatenate([in_block(bb_re), in_block(bb_im)], axis=2).astype(BF16)
    cmat = jnp.concatenate([out_block(c_re), out_block(-c_im)], axis=1).astype(BF16)
    lam = jnp.concatenate([lb_re.reshape(2, 1, S5_STATES), lb_im.reshape(2, 1, S5_STATES)], axis=2)
    tab = jnp.concatenate([jnp.repeat(lam[0], SCAN_HALF, axis=0), jnp.repeat(lam[1], SCAN_HALF, axis=0)], axis=0)
    return bmat, cmat, tab


def _scan_tiles(df_ref, db_ref, s_ref, tab_ref):
    n_tiles = df_ref.shape[1] // SUBLANES
    slabs_per_block = SCAN_LANES // LANES
    top = lax.broadcasted_iota(jnp.int32, (SUBLANES, SCAN_LANES), 0) < SCAN_HALF

    def advance(l_re, l_im, s_re, s_im, x_re, x_im):
        return x_re + l_re * s_re - l_im * s_im, x_im + l_re * s_im + l_im * s_re

    def swap(x):
        return pltpu.roll(x, SCAN_HALF, 0)

    for lb in range(S5_STATES // SCAN_LANES):
        re_cols = pl.ds(lb * SCAN_LANES, SCAN_LANES)
        im_cols = pl.ds(S5_STATES + lb * SCAN_LANES, SCAN_LANES)
        re_slabs = [lb * slabs_per_block + k for k in range(slabs_per_block)]
        im_slabs = [S5_STATES // LANES + s for s in re_slabs]
        l_re, l_im = tab_ref[:, re_cols], tab_ref[:, im_cols]

        def body(i, s, re_slabs=re_slabs, im_slabs=im_slabs, l_re=l_re, l_im=l_im):
            rows_f = pl.ds(pl.multiple_of(i * SUBLANES, SUBLANES), SUBLANES)
            rows_b = pl.ds(pl.multiple_of((n_tiles - 1 - i) * SUBLANES, SUBLANES), SUBLANES)
            out = []
            for slabs, s_part in ((re_slabs, 0), (im_slabs, 1)):
                f = jnp.concatenate([df_ref[k, rows_f, :] for k in slabs], axis=1)
                b = jnp.concatenate([db_ref[k, rows_b, :] for k in slabs], axis=1)
                out.append((jnp.where(top, f, b), swap(jnp.where(top, b, f))))
            (x0_re, x1_re), (x0_im, x1_im) = out
            y0_re, y0_im = advance(l_re, l_im, s[0], s[1], x0_re, x0_im)
            y1_re, y1_im = advance(l_re, l_im, y0_re, y0_im, x1_re, x1_im)
            for slabs, y0, y1 in ((re_slabs, y0_re, y1_re), (im_slabs, y0_im, y1_im)):
                r1 = swap(y1)
                f_new = jnp.where(top, y0, r1)
                b_new = jnp.where(top, r1, y0)
                for n, k in enumerate(slabs):
                    df_ref[k, rows_f, :] = f_new[:, n * LANES:(n + 1) * LANES]
                    db_ref[k, rows_b, :] = b_new[:, n * LANES:(n + 1) * LANES]
            return y1_re, y1_im

        s_re, s_im = lax.fori_loop(0, n_tiles, body, (s_ref[:, re_cols], s_ref[:, im_cols]))
        s_ref[:, re_cols] = s_re
        s_ref[:, im_cols] = s_im


def _s5scan_kernel(uf_ref, ub_ref, uc_ref, bmat_ref, cmat_ref, tab_ref, wcast_ref, yf_ref, yb_ref, wcast_out_ref,
                   df_ref, db_ref, s_ref):
    _cast_block(wcast_ref, wcast_out_ref)
    j = pl.program_id(0)
    n_slabs = 2 * S5_STATES // LANES

    def batch_rows(b):
        return pl.ds(b, SCAN_T, stride=SCAN_HALF)

    def drive(u_ref, d_ref, direction):
        u = jnp.concatenate([u_ref[:, b * S5_W:(b + 1) * S5_W] for b in range(SCAN_HALF)], axis=0)
        d = jnp.dot(u.astype(BF16), bmat_ref[direction], preferred_element_type=F32)
        for b in range(SCAN_HALF):
            for k in range(n_slabs):
                d_ref[k, batch_rows(b), :] = d[b * SCAN_T:(b + 1) * SCAN_T, k * LANES:(k + 1) * LANES]

    def readout(d_ref, y_ref, direction):
        s = jnp.concatenate(
            [jnp.concatenate([d_ref[k, batch_rows(b), :] for k in range(n_slabs)], axis=1)
             for b in range(SCAN_HALF)], axis=0)
        y = jnp.dot(s.astype(BF16), cmat_ref[direction], preferred_element_type=F32)
        for b in range(SCAN_HALF):
            y_ref[:, b * S5_W:(b + 1) * S5_W] = y[b * SCAN_T:(b + 1) * SCAN_T, :]

    @pl.when(j == 0)
    def _context():
        s_ref[...] = jnp.zeros_like(s_ref)
        drive(uc_ref, df_ref, 0)
        drive(uc_ref, db_ref, 1)
        _scan_tiles(df_ref, db_ref, s_ref, tab_ref)

    drive(uf_ref, df_ref, 0)
    drive(ub_ref, db_ref, 1)
    _scan_tiles(df_ref, db_ref, s_ref, tab_ref)
    readout(df_ref, yf_ref, 0)
    readout(db_ref, yb_ref, 1)


def _s5scan(u_lat, u_ctx, bmat, cmat, tab, w_cast):
    length, cols = u_lat.shape
    n = length // SCAN_T
    assert u_ctx.shape == (SCAN_T, cols) and cols == SCAN_HALF * S5_W
    slab = pltpu.VMEM((2 * S5_STATES // LANES, SCAN_T * SCAN_HALF, LANES), F32)
    cast = _cast_spec(w_cast, n, lambda j: j)
    return pl.pallas_call(
        _s5scan_kernel,
        grid=(n,),
        in_specs=[pl.BlockSpec((SCAN_T, cols), lambda j: (j, 0)),
                  pl.BlockSpec((SCAN_T, cols), lambda j: (n - 1 - j, 0)),
                  pl.BlockSpec((SCAN_T, cols), lambda j: (0, 0)),
                  pl.BlockSpec((2, S5_W, 2 * S5_STATES), lambda j: (0, 0, 0)),
                  pl.BlockSpec((2, 2 * S5_STATES, S5_W), lambda j: (0, 0, 0)),
                  pl.BlockSpec((SUBLANES, 2 * S5_STATES), lambda j: (0, 0)),
                  cast],
        out_specs=[pl.BlockSpec((SCAN_T, cols), lambda j: (j, 0)),
                   pl.BlockSpec((SCAN_T, cols), lambda j: (n - 1 - j, 0)),
                   cast],
        out_shape=[jax.ShapeDtypeStruct((length, cols), F32)] * 2 + [jax.ShapeDtypeStruct(w_cast.shape, BF16)],
        scratch_shapes=[slab, slab, pltpu.VMEM((SUBLANES, 2 * S5_STATES), F32)],
        compiler_params=_params(("arbitrary",), VMEM_LIMIT_BYTES),
        name="s5scan",
    )(u_lat, u_lat, u_ctx, bmat, cmat, tab, w_cast)


def _outproj_kernel(x_ref, zs_ref, yf_ref, yb_ref, gm_ref, mod_ref, d_ref, wglu_ref, bglu_ref, wout_ref,
                    ln_ref, wr_ref, x1_ref, hp_ref, aff_ref):
    y = d_ref[...] * zs_ref[...] + yf_ref[...] + yb_ref[...]
    g = jax.nn.gelu(y)
    s5 = g * jax.nn.sigmoid(jnp.dot(g.astype(BF16), wglu_ref[...], preferred_element_type=F32) + bglu_ref[...])
    mix = (jnp.dot(s5.astype(BF16), wout_ref[0:S5_W, :], preferred_element_type=F32)
           + jnp.dot(gm_ref[...], wout_ref[S5_W:, :], preferred_element_type=F32))
    x1 = _layer_norm(ALPHA * x_ref[...] + mod_ref[2:3, :] * mix) * ln_ref[0:1, :] + ln_ref[1:2, :]
    x1_ref[...] = x1
    h = _layer_norm(x1) * (1.0 + mod_ref[4:5, :]) + mod_ref[3:4, :]
    h_hi = h.astype(BF16)
    h_lo = (h - h_hi.astype(F32)).astype(BF16)
    contract_last = (((1,), (1,)), ((), ()))
    logits = (lax.dot_general(wr_ref[0], h_hi, contract_last, preferred_element_type=F32)
              + lax.dot_general(wr_ref[0], h_lo, contract_last, preferred_element_type=F32)
              + lax.dot_general(wr_ref[1], h_hi, contract_last, preferred_element_type=F32))
    ex = jnp.exp(logits - jnp.max(logits, axis=0, keepdims=True))
    aff_ref[...] = ex / jnp.sum(ex, axis=0, keepdims=True)
    _store_row_tiles(hp_ref, 0, h)


def _outproj(x, zs, yf, yb, gm, mod, d, w_glu, b_glu, w_out, ln1, w_rt):
    bn, length, _ = x.shape
    tok = lambda b, i: (b, i, 0)
    s5 = lambda b, i: (i, b)
    const2 = lambda b, i: (0, 0)
    return pl.pallas_call(
        _outproj_kernel,
        grid=(bn, length // TOK_TILE),
        in_specs=[pl.BlockSpec((None, TOK_TILE, D_MODEL), tok),
                  pl.BlockSpec((TOK_TILE, S5_W), s5),
                  pl.BlockSpec((TOK_TILE, S5_W), s5),
                  pl.BlockSpec((TOK_TILE, S5_W), s5),
                  pl.BlockSpec((None, TOK_TILE, GM_W), tok),
                  pl.BlockSpec((None, 6, D_MODEL), lambda b, i: (b, 0, 0)),
                  pl.BlockSpec((1, S5_W), const2),
                  pl.BlockSpec((S5_W, S5_W), const2),
                  pl.BlockSpec((1, S5_W), const2),
                  pl.BlockSpec((D_MODEL, D_MODEL), const2),
                  pl.BlockSpec((2, D_MODEL), const2),
                  pl.BlockSpec((2, N_EXPERTS, D_MODEL), lambda b, i: (0, 0, 0))],
        out_specs=[pl.BlockSpec((None, TOK_TILE, D_MODEL), tok),
                   pl.BlockSpec((None, TOK_TILE * ROW_CHUNKS, LANES), tok),
                   pl.BlockSpec((None, N_EXPERTS, TOK_TILE), lambda b, i: (b, 0, i))],
        out_shape=[jax.ShapeDtypeStruct((bn, length, D_MODEL), F32),
                   jax.ShapeDtypeStruct((bn, length * ROW_CHUNKS, LANES), F32),
                   jax.ShapeDtypeStruct((bn, N_EXPERTS, length), F32)],
        compiler_params=_params(("arbitrary", "arbitrary"), VMEM_LIMIT_BYTES),
        name="outproj",
    )(x, zs, yf, yb, gm, mod, d, w_glu, b_glu, w_out, ln1, w_rt)


def _route_kernel(aff_ref, key_ref, offs_ref, *, cap):
    aff = aff_ref[...]
    ne, nr, _ = aff.shape

    def count(mask):
        return jnp.sum(jnp.sum(mask.astype(F32), axis=2, keepdims=True), axis=1, keepdims=True)

    normal = count(aff >= F32_TINY) >= cap
    lo = jnp.where(normal, F32_TINY, 0.0)
    hi = jnp.where(normal, 2.0, F32_TINY)
    for it in range(BRACKET_GEO_STEPS + BRACKET_LIN_STEPS):
        mid = 0.5 * (lo + hi)
        if it < BRACKET_GEO_STEPS:
            mid = jnp.where(normal, jnp.sqrt(lo * hi), mid)
        ok = count(aff >= mid) >= cap
        lo = jnp.where(ok, mid, lo)
        hi = jnp.where(ok, hi, mid)
    thr = jnp.min(jnp.min(jnp.where(aff >= lo, aff, 2.0), axis=2, keepdims=True), axis=1, keepdims=True)
    above = aff > thr
    tied = aff == thr
    need = cap - count(above)

    kk = lax.broadcasted_iota(jnp.int32, (LANES, LANES), 0)
    ll = lax.broadcasted_iota(jnp.int32, (LANES, LANES), 1)
    upper = (kk <= ll).astype(BF16)
    ones = jnp.ones((LANES, LANES), BF16)
    ri = lax.broadcasted_iota(jnp.int32, (nr, nr), 0)
    rj = lax.broadcasted_iota(jnp.int32, (nr, nr), 1)
    lower = (rj < ri).astype(BF16)

    def exclusive_prefix(mask):
        m = mask.astype(BF16).reshape(ne * nr, LANES)
        in_row = jnp.dot(m, upper, preferred_element_type=F32).reshape(ne, nr, LANES)
        row_tot = jnp.dot(m, ones, preferred_element_type=F32).reshape(ne, nr, LANES)
        row_off = jnp.stack([jnp.dot(lower, row_tot[e].astype(BF16), preferred_element_type=F32)
                             for e in range(ne)])
        return in_row - mask.astype(F32) + row_off, row_off

    tie_rank, _ = exclusive_prefix(tied)
    sel = above | (tied & (tie_rank < need))
    pos, row_off = exclusive_prefix(sel)
    key_ref[...] = jnp.where(sel, pos, -1.0)
    offs_ref[...] = row_off


def _route(aff4, cap):
    bn, ne, nr, _ = aff4.shape
    spec = pl.BlockSpec((None, ne, nr, LANES), lambda b: (b, 0, 0, 0))
    return pl.pallas_call(
        functools.partial(_route_kernel, cap=cap),
        grid=(bn,),
        in_specs=[spec],
        out_specs=[spec, spec],
        out_shape=[jax.ShapeDtypeStruct(aff4.shape, F32)] * 2,
        compiler_params=_params(("arbitrary",), VMEM_LIMIT_BYTES),
        name="route",
    )(aff4)


def _compact_kernel(offs_sref, key_ref, aff_ref, wcast_ref, idx_ref, gate_ref, wcast_out_ref, acc_i, acc_g, *, cap):
    _cast_block(wcast_ref, wcast_out_ref)
    be = pl.program_id(0) * pl.num_programs(1) + pl.program_id(1)
    nr = key_ref.shape[0]
    acc_i[...] = jnp.zeros_like(acc_i)
    acc_g[...] = jnp.zeros_like(acc_g)
    slot = lax.broadcasted_iota(jnp.int32, (COMPACT_WIN, LANES), 0)
    lane = lax.broadcasted_iota(jnp.int32, (1, LANES), 1)

    def body(r, carry):
        off = offs_sref[be * nr + r]
        base = pl.multiple_of((off // SUBLANES) * SUBLANES, SUBLANES)
        win = pl.ds(base, COMPACT_WIN)
        hit = (slot + base) == key_ref[r].astype(jnp.int32)
        tok = (lane + r * LANES).astype(F32)
        acc_i[win, :] += jnp.where(hit, tok, 0.0)
        acc_g[win, :] += jnp.where(hit, aff_ref[r], 0.0)
        return carry

    lax.fori_loop(0, nr, body, 0)
    ones = jnp.ones((SUBLANES, LANES), F32)
    idx_rows = lax.dot_general(ones, acc_i[0:cap, :], (((1,), (1,)), ((), ())), precision=HIGHEST,
                               preferred_element_type=F32)
    idx_ref[...] = idx_rows[0:1, :].astype(jnp.int32)
    gate_ref[...] = jnp.sum(acc_g[0:cap, :], axis=1, keepdims=True)


def _compact(row_offs, key5, aff5, cap, w_cast):
    bn, ne, nr, _, _ = key5.shape
    spec_in = pl.BlockSpec((None, None, nr, 1, LANES), lambda b, e, offs: (b, e, 0, 0, 0))
    spec_out = pl.BlockSpec((None, None, cap, 1), lambda b, e, offs: (b, e, 0, 0))
    cast = _cast_spec(w_cast, bn * ne, lambda b, e, offs: b * ne + e)
    return pl.pallas_call(
        functools.partial(_compact_kernel, cap=cap),
        grid_spec=pltpu.PrefetchScalarGridSpec(
            num_scalar_prefetch=1,
            grid=(bn, ne),
            in_specs=[spec_in, spec_in, cast],
            out_specs=[pl.BlockSpec((None, None, 1, cap), lambda b, e, offs: (b, e, 0, 0)), spec_out, cast],
            scratch_shapes=[pltpu.VMEM((cap + COMPACT_WIN, LANES), F32),
                            pltpu.VMEM((cap + COMPACT_WIN, LANES), F32)]),
        out_shape=[jax.ShapeDtypeStruct((bn, ne, 1, cap), jnp.int32),
                   jax.ShapeDtypeStruct((bn, ne, cap, 1), F32),
                   jax.ShapeDtypeStruct(w_cast.shape, BF16)],
        compiler_params=_params(("arbitrary", "arbitrary"), VMEM_LIMIT_BYTES),
        name="compact",
    )(row_offs, key5, aff5, w_cast)


def _experts_kernel(rows_sref, h_hbm, gate_ref, wg_ref, wu_ref, wd_ref, res_ref, xg_ref, sem, *, cap):
    ne = pl.num_programs(1)
    step = pl.program_id(0) * ne + pl.program_id(1)
    last = pl.num_programs(0) * ne - 1
    slot = step % 2
    nxt = jnp.minimum(step + 1, last)

    def start_row(c, of_step, into_slot):
        pltpu.make_async_copy(h_hbm.at[of_step // ne, _tile_at(rows_sref[of_step * cap + c]), :],
                              xg_ref.at[into_slot, _row_tile(c), :], sem.at[into_slot]).start()

    def wait_rows(of_slot):
        pltpu.make_async_copy(h_hbm.at[0, pl.ds(0, cap * ROW_CHUNKS), :], xg_ref.at[of_slot],
                              sem.at[of_slot]).wait()

    @pl.when(step == 0)
    def _first_rows():
        def body(c, carry):
            start_row(c, step, slot)
            return carry

        lax.fori_loop(0, cap, body, 0)

    wait_rows(slot)
    for m in range(cap // ROW_TILE):
        rows = slice(m * ROW_TILE, (m + 1) * ROW_TILE)
        for c in range(rows.start, rows.stop):
            start_row(c, nxt, 1 - slot)
        xr = _load_row_tiles(xg_ref.at[slot], rows.start, ROW_TILE).astype(BF16)
        hg = jnp.dot(xr, wg_ref[...], preferred_element_type=F32)
        hu = jnp.dot(xr, wu_ref[...], preferred_element_type=F32)
        hid = (hg * jax.nn.sigmoid(hg) * hu).astype(BF16)
        out = jnp.dot(hid, wd_ref[...], preferred_element_type=F32)
        _store_row_tiles(res_ref, rows.start, out * gate_ref[rows, :])

    @pl.when(step == last)
    def _drain():
        wait_rows(1 - slot)


def _experts(tile_rows, h, gate, wg, wu, wd, cap):
    bn = h.shape[0]
    ne = wg.shape[0]
    return pl.pallas_call(
        functools.partial(_experts_kernel, cap=cap),
        grid_spec=pltpu.PrefetchScalarGridSpec(
            num_scalar_prefetch=1,
            grid=(bn, ne),
            in_specs=[pl.BlockSpec(memory_space=pl.ANY),
                      pl.BlockSpec((None, None, cap, 1), lambda b, e, rows: (b, e, 0, 0)),
                      pl.BlockSpec((None, D_MODEL, D_FF), lambda b, e, rows: (e, 0, 0)),
                      pl.BlockSpec((None, D_MODEL, D_FF), lambda b, e, rows: (e, 0, 0)),
                      pl.BlockSpec((None, D_FF, D_MODEL), lambda b, e, rows: (e, 0, 0))],
            out_specs=pl.BlockSpec((None, None, cap * ROW_CHUNKS, LANES), lambda b, e, rows: (b, e, 0, 0)),
            scratch_shapes=[pltpu.VMEM((2, cap * ROW_CHUNKS, LANES), F32),
                            pltpu.SemaphoreType.DMA((2,))]),
        out_shape=jax.ShapeDtypeStruct((bn, ne, cap * ROW_CHUNKS, LANES), F32),
        compiler_params=_params(("arbitrary", "arbitrary"), VMEM_LIMIT_BYTES),
        name="experts",
    )(tile_rows, h, gate, wg, wu, wd)


SCATTER_UNROLL = 16


def _combine_kernel(rows_sref, res_ref, x1_ref, mod_ref, ln_ref, o_ref, acc_ref, *, cap, ne):
    b = pl.program_id(0)
    step = pl.program_id(1)

    @pl.when(step == 0)
    def _init():
        acc_ref[...] = jnp.zeros_like(acc_ref)

    @pl.when(step < ne)
    def _scatter():
        base = (b * ne + step) * cap

        def body(i, carry):
            first = i * SCATTER_UNROLL
            toks = [_tile_at(rows_sref[base + first + k]) for k in range(SCATTER_UNROLL)]
            sums = [acc_ref[toks[k], :] + res_ref[_row_tile(first + k), :] for k in range(SCATTER_UNROLL)]
            for k in range(SCATTER_UNROLL):
                acc_ref[toks[k], :] = sums[k]
            return carry

        lax.fori_loop(0, cap // SCATTER_UNROLL, body, 0)

    @pl.when(step >= ne)
    def _finish():
        moe = _load_row_tiles(acc_ref, (step - ne) * FIN_TILE, FIN_TILE)
        y = ALPHA * x1_ref[...] + mod_ref[5:6, :] * moe
        o_ref[...] = _layer_norm(y) * ln_ref[0:1, :] + ln_ref[1:2, :]


def _combine(tile_rows, res, x1, mod, ln2, cap):
    bn, length, _ = x1.shape
    ne = res.shape[1]
    assert cap % SCATTER_UNROLL == 0
    tok = lambda b, s, idx: (b, jnp.maximum(s - ne, 0), 0)
    return pl.pallas_call(
        functools.partial(_combine_kernel, cap=cap, ne=ne),
        grid_spec=pltpu.PrefetchScalarGridSpec(
            num_scalar_prefetch=1,
            grid=(bn, ne + length // FIN_TILE),
            in_specs=[pl.BlockSpec((None, None, cap * ROW_CHUNKS, LANES),
                                   lambda b, s, idx: (b, jnp.minimum(s, ne - 1), 0, 0)),
                      pl.BlockSpec((None, FIN_TILE, D_MODEL), tok),
                      pl.BlockSpec((None, 6, D_MODEL), lambda b, s, idx: (b, 0, 0)),
                      pl.BlockSpec((2, D_MODEL), lambda b, s, idx: (0, 0))],
            out_specs=pl.BlockSpec((None, FIN_TILE, D_MODEL), tok),
            scratch_shapes=[pltpu.VMEM((length * ROW_CHUNKS, LANES), F32)]),
        out_shape=jax.ShapeDtypeStruct((bn, length, D_MODEL), F32),
        compiler_params=_params(("arbitrary", "arbitrary"), VMEM_LIMIT_BYTES),
        name="combine",
    )(tile_rows, res, x1, mod, ln2)


def kernel(x, c, ctx, c_ctx, w_ada, b_ada, w_in, gm_ws, gm_bs, s5_a_re, s5_a_im, s5_log_step, s5_b_re, s5_b_im,
           s5_c_re, s5_c_im, s5_d, s5_w_glu, s5_b_glu, w_out, ln1_g, ln1_b, w_router, moe_w_gate, moe_w_up,
           moe_w_down, ln2_g, ln2_b):
    assert w_ada.shape[0] == DEPTH == 1
    bn, length, _ = x.shape
    assert bn == SCAN_HALF and ctx.shape[1] == SCAN_T and length % TOK_TILE == 0
    cap = CAPACITY_FACTOR * length // N_EXPERTS

    cc = jnp.zeros((SUBLANES, D_MODEL), F32).at[:bn].set(c).at[bn].set(c_ctx)
    mod = _adaln(cc, w_ada[0], b_ada[0][None, :])
    mod_lat = mod[:bn].reshape(bn, 6, D_MODEL)
    mod_ctx = mod[bn:bn + 1].reshape(1, 6, D_MODEL)

    w_in_b = w_in[0].astype(BF16)
    zs, gm, wg_b = _inproj(x, mod_lat, w_in_b, gm_ws[0].astype(BF16), gm_bs[0][:, :, None], moe_w_gate[0])
    zc = _ctxproj(ctx, mod_ctx, w_in_b)

    bmat, cmat, tab = _s5_tables(s5_a_re[0], s5_a_im[0], s5_log_step[0], s5_b_re[0], s5_b_im[0],
                                 s5_c_re[0], s5_c_im[0])
    yf, yb, wd_b = _s5scan(zs, zc, bmat, cmat, tab, moe_w_down[0])

    w_rt = w_router[0].T
    w_rt_hi = w_rt.astype(BF16)
    w_rt_parts = jnp.stack([w_rt_hi, (w_rt - w_rt_hi.astype(F32)).astype(BF16)])
    x1, hp, aff = _outproj(x, zs, yf, yb, gm, mod_lat, s5_d[0].reshape(1, S5_W), s5_w_glu[0].astype(BF16),
                           s5_b_glu[0][None, :], w_out[0].astype(BF16), jnp.stack([ln1_g[0], ln1_b[0]]),
                           w_rt_parts)

    aff4 = aff.reshape(bn, N_EXPERTS, length // LANES, LANES)
    key4, offs4 = _route(aff4, cap)
    row_offs = offs4[..., 0].astype(jnp.int32)
    rows5 = (bn, N_EXPERTS, length // LANES, 1, LANES)
    idx, gate, wu_b = _compact(row_offs.reshape(-1), key4.reshape(rows5), aff4.reshape(rows5), cap, moe_w_up[0])
    tile_rows = idx.reshape(-1) * ROW_CHUNKS

    res = _experts(tile_rows, hp, gate, wg_b, wu_b, wd_b, cap)

    return _combine(tile_rows, res, x1, mod_lat, jnp.stack([ln2_g[0], ln2_b[0]]), cap)
```

```python
import functools

import jax
import jax.numpy as jnp
from jax import lax
from jax.experimental import pallas as pl
from jax.experimental.pallas import tpu as pltpu

D_MODEL = 1024
CHUNK = 128
S5_W = D_MODEL // 4
GM_W = D_MODEL - S5_W
GM_HEAD_DIM = 128
GM_HEADS = GM_W // GM_HEAD_DIM
S5_GROUP = 16
S5_GROUPS = S5_W // S5_GROUP
S5_STATE = 64
S5_STATES = S5_GROUPS * S5_STATE
IN_COLS = S5_W + 2 * GM_W
N_EXPERTS = 16
CAPACITY_FACTOR = 2
D_FF = 2816
DEPTH = 1
ALPHA = (2.0 * DEPTH) ** 0.25
LN_EPS = 1e-6

F32 = jnp.float32
BF16 = jnp.bfloat16
HIGHEST = lax.Precision.HIGHEST

LANES = 128
SUBLANES = 8
ROW_CHUNKS = D_MODEL // LANES
VMEM_LIMIT_BYTES = 58 * 1024 * 1024

TOK_TILE = 1024
FIN_TILE = 512
SCAN_T = 256
SCAN_LANES = 1024
SCAN_HALF = SUBLANES // 2
ROW_TILE = 256
COMPACT_WIN = CHUNK + SUBLANES
F32_TINY = 2.0 ** -126
BRACKET_GEO_STEPS = 8
BRACKET_LIN_STEPS = 26


def _layer_norm(x):
    mu = jnp.mean(x, axis=-1, keepdims=True)
    xc = x - mu
    var = jnp.mean(xc * xc, axis=-1, keepdims=True)
    return xc * lax.rsqrt(var + LN_EPS)


def _params(semantics, vmem=None):
    return pltpu.CompilerParams(dimension_semantics=semantics, vmem_limit_bytes=vmem)


def _store_row_tiles(ref, first_token, x):
    n = x.shape[0]
    for s in range(ROW_CHUNKS):
        ref[pl.ds(first_token * ROW_CHUNKS + s, n, stride=ROW_CHUNKS), :] = x[:, s * LANES:(s + 1) * LANES]


def _load_row_tiles(ref, first_token, n):
    return jnp.concatenate(
        [ref[pl.ds(first_token * ROW_CHUNKS + s, n, stride=ROW_CHUNKS), :] for s in range(ROW_CHUNKS)], axis=1)


def _row_tile(token):
    return pl.ds(pl.multiple_of(token * ROW_CHUNKS, ROW_CHUNKS), ROW_CHUNKS)


def _tile_at(first_row):
    return pl.ds(pl.multiple_of(first_row, ROW_CHUNKS), ROW_CHUNKS)


def _cast_spec(w, n_steps, step_of):
    ne, rows, cols = w.shape
    per_expert = n_steps // ne
    assert per_expert * ne == n_steps and rows % (per_expert * 2 * SUBLANES) == 0
    return pl.BlockSpec((None, rows // per_expert, cols),
                        lambda *ids: (step_of(*ids) // per_expert, step_of(*ids) % per_expert, 0))


def _cast_block(w_ref, o_ref):
    o_ref[...] = w_ref[...].astype(BF16)


def _adaln_kernel(c_ref, w_ref, b_ref, o_ref):
    c = c_ref[...]
    a = c * jax.nn.sigmoid(c)
    o_ref[...] = jnp.dot(a, w_ref[...], precision=HIGHEST, preferred_element_type=F32) + b_ref[...]


def _adaln(cc, w_ada, b_ada):
    n = w_ada.shape[1] // D_MODEL
    return pl.pallas_call(
        _adaln_kernel,
        grid=(n,),
        in_specs=[pl.BlockSpec((SUBLANES, D_MODEL), lambda j: (0, 0)),
                  pl.BlockSpec((D_MODEL, D_MODEL), lambda j: (0, j)),
                  pl.BlockSpec((1, D_MODEL), lambda j: (0, j))],
        out_specs=pl.BlockSpec((SUBLANES, D_MODEL), lambda j: (0, j)),
        out_shape=jax.ShapeDtypeStruct((SUBLANES, w_ada.shape[1]), F32),
        compiler_params=_params(("arbitrary",)),
        name="adaln",
    )(cc, w_ada, b_ada)


def _inproj_kernel(x_ref, mod_ref, w_ref, ws_ref, bs_ref, wcast_ref, zs_ref, gm_ref, wcast_out_ref):
    _cast_block(wcast_ref, wcast_out_ref)
    h = _layer_norm(x_ref[...]) * (1.0 + mod_ref[1:2, :]) + mod_ref[0:1, :]
    z = jnp.dot(h.astype(BF16), w_ref[...], preferred_element_type=F32)
    zs_ref[...] = z[:, :S5_W]
    for g in range(GM_HEADS):
        lo = S5_W + g * GM_HEAD_DIM
        u = jax.nn.gelu(z[:, lo:lo + GM_HEAD_DIM])
        v = jax.nn.gelu(z[:, lo + GM_W:lo + GM_W + GM_HEAD_DIM])
        vn = _layer_norm(v).astype(BF16)
        wsg = ws_ref[g]
        bsg = bs_ref[g]
        for c in range(TOK_TILE // CHUNK):
            rows = slice(c * CHUNK, (c + 1) * CHUNK)
            mixed = jnp.dot(wsg, vn[rows, :], preferred_element_type=F32) + bsg
            gm_ref[rows, g * GM_HEAD_DIM:(g + 1) * GM_HEAD_DIM] = (u[rows, :] * mixed).astype(BF16)


def _inproj(x, mod, w_in, gm_ws, gm_bs, w_cast):
    bn, length, _ = x.shape
    nt = length // TOK_TILE
    cast = _cast_spec(w_cast, bn * nt, lambda b, i: b * nt + i)
    return pl.pallas_call(
        _inproj_kernel,
        grid=(bn, nt),
        in_specs=[pl.BlockSpec((None, TOK_TILE, D_MODEL), lambda b, i: (b, i, 0)),
                  pl.BlockSpec((None, 6, D_MODEL), lambda b, i: (b, 0, 0)),
                  pl.BlockSpec((D_MODEL, IN_COLS), lambda b, i: (0, 0)),
                  pl.BlockSpec((GM_HEADS, CHUNK, CHUNK), lambda b, i: (0, 0, 0)),
                  pl.BlockSpec((GM_HEADS, CHUNK, 1), lambda b, i: (0, 0, 0)),
                  cast],
        out_specs=[pl.BlockSpec((TOK_TILE, S5_W), lambda b, i: (i, b)),
                   pl.BlockSpec((None, TOK_TILE, GM_W), lambda b, i: (b, i, 0)),
                   cast],
        out_shape=[jax.ShapeDtypeStruct((length, bn * S5_W), F32),
                   jax.ShapeDtypeStruct((bn, length, GM_W), BF16),
                   jax.ShapeDtypeStruct(w_cast.shape, BF16)],
        compiler_params=_params(("arbitrary", "arbitrary"), VMEM_LIMIT_BYTES),
        name="inproj",
    )(x, mod, w_in, gm_ws, gm_bs, w_cast)


def _ctxproj_kernel(x_ref, mod_ref, w_ref, o_ref):
    h = _layer_norm(x_ref[...]) * (1.0 + mod_ref[1:2, :]) + mod_ref[0:1, :]
    o_ref[...] = jnp.dot(h.astype(BF16), w_ref[...], preferred_element_type=F32)


def _ctxproj(ctx, mod_ctx, w_in):
    bn, clen, _ = ctx.shape
    return pl.pallas_call(
        _ctxproj_kernel,
        grid=(bn,),
        in_specs=[pl.BlockSpec((None, clen, D_MODEL), lambda b: (b, 0, 0)),
                  pl.BlockSpec((None, 6, D_MODEL), lambda b: (0, 0, 0)),
                  pl.BlockSpec((D_MODEL, S5_W), lambda b: (0, 0))],
        out_specs=pl.BlockSpec((clen, S5_W), lambda b: (0, b)),
        out_shape=jax.ShapeDtypeStruct((clen, bn * S5_W), F32),
        compiler_params=_params(("arbitrary",)),
        name="ctxproj",
    )(ctx, mod_ctx, w_in)


def _s5_tables(a_re, a_im, log_step, b_re, b_im, c_re, c_im):
    step = jnp.exp(log_step)[..., None]
    mag = jnp.exp(a_re * step)
    lb_re = mag * jnp.cos(a_im * step)
    lb_im = mag * jnp.sin(a_im * step)
    den = a_re * a_re + a_im * a_im
    q_re = ((lb_re - 1.0) * a_re + lb_im * a_im) / den
    q_im = (lb_im * a_re - (lb_re - 1.0) * a_im) / den
    bb_re = q_re[..., None] * b_re - q_im[..., None] * b_im
    bb_im = q_re[..., None] * b_im + q_im[..., None] * b_re
    eye = jnp.eye(S5_GROUPS, dtype=F32)

    def in_block(m):
        return jnp.einsum('dgph,gk->dghkp', m, eye).reshape(2, S5_W, S5_STATES)

    def out_block(m):
        return jnp.einsum('dghp,gk->dgpkh', m, eye).reshape(2, S5_STATES, S5_W)

    bmat = jnp.concatenate([in_block(bb_re), in_block(bb_im)], axis=2).astype(BF16)
    cmat = jnp.concatenate([out_block(c_re), out_block(-c_im)], axis=1).astype(BF16)
    lam = jnp.concatenate([lb_re.reshape(2, 1, S5_STATES), lb_im.reshape(2, 1, S5_STATES)], axis=2)
    tab = jnp.concatenate([jnp.repeat(lam[0], SCAN_HALF, axis=0), jnp.repeat(lam[1], SCAN_HALF, axis=0)], axis=0)
    return bmat, cmat, tab


def _scan_tiles(df_ref, db_ref, s_ref, tab_ref):
    n_tiles = df_ref.shape[1] // SUBLANES
    slabs_per_block = SCAN_LANES // LANES
    top = lax.broadcasted_iota(jnp.int32, (SUBLANES, SCAN_LANES), 0) < SCAN_HALF

    def advance(l_re, l_im, s_re, s_im, x_re, x_im):
        return x_re + l_re * s_re - l_im * s_im, x_im + l_re * s_im + l_im * s_re

    def swap(x):
        return pltpu.roll(x, SCAN_HALF, 0)

    for lb in range(S5_STATES // SCAN_LANES):
        re_cols = pl.ds(lb * SCAN_LANES, SCAN_LANES)
        im_cols = pl.ds(S5_STATES + lb * SCAN_LANES, SCAN_LANES)
        re_slabs = [lb * slabs_per_block + k for k in range(slabs_per_block)]
        im_slabs = [S5_STATES // LANES + s for s in re_slabs]
        l_re, l_im = tab_ref[:, re_cols], tab_ref[:, im_cols]

        def body(i, s, re_slabs=re_slabs, im_slabs=im_slabs, l_re=l_re, l_im=l_im):
            rows_f = pl.ds(pl.multiple_of(i * SUBLANES, SUBLANES), SUBLANES)
            rows_b = pl.ds(pl.multiple_of((n_tiles - 1 - i) * SUBLANES, SUBLANES), SUBLANES)
            out = []
            for slabs, s_part in ((re_slabs, 0), (im_slabs, 1)):
                f = jnp.concatenate([df_ref[k, rows_f, :] for k in slabs], axis=1)
                b = jnp.concatenate([db_ref[k, rows_b, :] for k in slabs], axis=1)
                out.append((jnp.where(top, f, b), swap(jnp.where(top, b, f))))
            (x0_re, x1_re), (x0_im, x1_im) = out
            y0_re, y0_im = advance(l_re, l_im, s[0], s[1], x0_re, x0_im)
            y1_re, y1_im = advance(l_re, l_im, y0_re, y0_im, x1_re, x1_im)
            for slabs, y0, y1 in ((re_slabs, y0_re, y1_re), (im_slabs, y0_im, y1_im)):
                r1 = swap(y1)
                f_new = jnp.where(top, y0, r1)
                b_new = jnp.where(top, r1, y0)
                for n, k in enumerate(slabs):
                    df_ref[k, rows_f, :] = f_new[:, n * LANES:(n + 1) * LANES]
                    db_ref[k, rows_b, :] = b_new[:, n * LANES:(n + 1) * LANES]
            return y1_re, y1_im

        s_re, s_im = lax.fori_loop(0, n_tiles, body, (s_ref[:, re_cols], s_ref[:, im_cols]))
        s_ref[:, re_cols] = s_re
        s_ref[:, im_cols] = s_im


def _s5scan_kernel(uf_ref, ub_ref, uc_ref, bmat_ref, cmat_ref, tab_ref, wcast_ref, yf_ref, yb_ref, wcast_out_ref,
                   df_ref, db_ref, s_ref):
    _cast_block(wcast_ref, wcast_out_ref)
    j = pl.program_id(0)
    n_slabs = 2 * S5_STATES // LANES

    def batch_rows(b):
        return pl.ds(b, SCAN_T, stride=SCAN_HALF)

    def drive(u_ref, d_ref, direction):
        u = jnp.concatenate([u_ref[:, b * S5_W:(b + 1) * S5_W] for b in range(SCAN_HALF)], axis=0)
        d = jnp.dot(u.astype(BF16), bmat_ref[direction], preferred_element_type=F32)
        for b in range(SCAN_HALF):
            for k in range(n_slabs):
                d_ref[k, batch_rows(b), :] = d[b * SCAN_T:(b + 1) * SCAN_T, k * LANES:(k + 1) * LANES]

    def readout(d_ref, y_ref, direction):
        s = jnp.concatenate(
            [jnp.concatenate([d_ref[k, batch_rows(b), :] for k in range(n_slabs)], axis=1)
             for b in range(SCAN_HALF)], axis=0)
        y = jnp.dot(s.astype(BF16), cmat_ref[direction], preferred_element_type=F32)
        for b in range(SCAN_HALF):
            y_ref[:, b * S5_W:(b + 1) * S5_W] = y[b * SCAN_T:(b + 1) * SCAN_T, :]

    @pl.when(j == 0)
    def _context():
        s_ref[...] = jnp.zeros_like(s_ref)
        drive(uc_ref, df_ref, 0)
        drive(uc_ref, db_ref, 1)
        _scan_tiles(df_ref, db_ref, s_ref, tab_ref)

    drive(uf_ref, df_ref, 0)
    drive(ub_ref, db_ref, 1)
    _scan_tiles(df_ref, db_ref, s_ref, tab_ref)
    readout(df_ref, yf_ref, 0)
    readout(db_ref, yb_ref, 1)


def _s5scan(u_lat, u_ctx, bmat, cmat, tab, w_cast):
    length, cols = u_lat.shape
    n = length // SCAN_T
    assert u_ctx.shape == (SCAN_T, cols) and cols == SCAN_HALF * S5_W
    slab = pltpu.VMEM((2 * S5_STATES // LANES, SCAN_T * SCAN_HALF, LANES), F32)
    cast = _cast_spec(w_cast, n, lambda j: j)
    return pl.pallas_call(
        _s5scan_kernel,
        grid=(n,),
        in_specs=[pl.BlockSpec((SCAN_T, cols), lambda j: (j, 0)),
                  pl.BlockSpec((SCAN_T, cols), lambda j: (n - 1 - j, 0)),
                  pl.BlockSpec((SCAN_T, cols), lambda j: (0, 0)),
                  pl.BlockSpec((2, S5_W, 2 * S5_STATES), lambda j: (0, 0, 0)),
                  pl.BlockSpec((2, 2 * S5_STATES, S5_W), lambda j: (0, 0, 0)),
                  pl.BlockSpec((SUBLANES, 2 * S5_STATES), lambda j: (0, 0)),
                  cast],
        out_specs=[pl.BlockSpec((SCAN_T, cols), lambda j: (j, 0)),
                   pl.BlockSpec((SCAN_T, cols), lambda j: (n - 1 - j, 0)),
                   cast],
        out_shape=[jax.ShapeDtypeStruct((length, cols), F32)] * 2 + [jax.ShapeDtypeStruct(w_cast.shape, BF16)],
        scratch_shapes=[slab, slab, pltpu.VMEM((SUBLANES, 2 * S5_STATES), F32)],
        compiler_params=_params(("arbitrary",), VMEM_LIMIT_BYTES),
        name="s5scan",
    )(u_lat, u_lat, u_ctx, bmat, cmat, tab, w_cast)


def _outproj_kernel(x_ref, zs_ref, yf_ref, yb_ref, gm_ref, mod_ref, d_ref, wglu_ref, bglu_ref, wout_ref,
                    ln_ref, wr_ref, x1_ref, hp_ref, aff_ref):
    y = d_ref[...] * zs_ref[...] + yf_ref[...] + yb_ref[...]
    g = jax.nn.gelu(y)
    s5 = g * jax.nn.sigmoid(jnp.dot(g.astype(BF16), wglu_ref[...], preferred_element_type=F32) + bglu_ref[...])
    mix = (jnp.dot(s5.astype(BF16), wout_ref[0:S5_W, :], preferred_element_type=F32)
           + jnp.dot(gm_ref[...], wout_ref[S5_W:, :], preferred_element_type=F32))
    x1 = _layer_norm(ALPHA * x_ref[...] + mod_ref[2:3, :] * mix) * ln_ref[0:1, :] + ln_ref[1:2, :]
    x1_ref[...] = x1
    h = _layer_norm(x1) * (1.0 + mod_ref[4:5, :]) + mod_ref[3:4, :]
    h_hi = h.astype(BF16)
    h_lo = (h - h_hi.astype(F32)).astype(BF16)
    contract_last = (((1,), (1,)), ((), ()))
    logits = (lax.dot_general(wr_ref[0], h_hi, contract_last, preferred_element_type=F32)
              + lax.dot_general(wr_ref[0], h_lo, contract_last, preferred_element_type=F32)
              + lax.dot_general(wr_ref[1], h_hi, contract_last, preferred_element_type=F32))
    ex = jnp.exp(logits - jnp.max(logits, axis=0, keepdims=True))
    aff_ref[...] = ex / jnp.sum(ex, axis=0, keepdims=True)
    _store_row_tiles(hp_ref, 0, h)


def _outproj(x, zs, yf, yb, gm, mod, d, w_glu, b_glu, w_out, ln1, w_rt):
    bn, length, _ = x.shape
    tok = lambda b, i: (b, i, 0)
    s5 = lambda b, i: (i, b)
    const2 = lambda b, i: (0, 0)
    return pl.pallas_call(
        _outproj_kernel,
        grid=(bn, length // TOK_TILE),
        in_specs=[pl.BlockSpec((None, TOK_TILE, D_MODEL), tok),
                  pl.BlockSpec((TOK_TILE, S5_W), s5),
                  pl.BlockSpec((TOK_TILE, S5_W), s5),
                  pl.BlockSpec((TOK_TILE, S5_W), s5),
                  pl.BlockSpec((None, TOK_TILE, GM_W), tok),
                  pl.BlockSpec((None, 6, D_MODEL), lambda b, i: (b, 0, 0)),
                  pl.BlockSpec((1, S5_W), const2),
                  pl.BlockSpec((S5_W, S5_W), const2),
                  pl.BlockSpec((1, S5_W), const2),
                  pl.BlockSpec((D_MODEL, D_MODEL), const2),
                  pl.BlockSpec((2, D_MODEL), const2),
                  pl.BlockSpec((2, N_EXPERTS, D_MODEL), lambda b, i: (0, 0, 0))],
        out_specs=[pl.BlockSpec((None, TOK_TILE, D_MODEL), tok),
                   pl.BlockSpec((None, TOK_TILE * ROW_CHUNKS, LANES), tok),
                   pl.BlockSpec((None, N_EXPERTS, TOK_TILE), lambda b, i: (b, 0, i))],
        out_shape=[jax.ShapeDtypeStruct((bn, length, D_MODEL), F32),
                   jax.ShapeDtypeStruct((bn, length * ROW_CHUNKS, LANES), F32),
                   jax.ShapeDtypeStruct((bn, N_EXPERTS, length), F32)],
        compiler_params=_params(("arbitrary", "arbitrary"), VMEM_LIMIT_BYTES),
        name="outproj",
    )(x, zs, yf, yb, gm, mod, d, w_glu, b_glu, w_out, ln1, w_rt)


def _route_kernel(aff_ref, key_ref, offs_ref, *, cap):
    aff = aff_ref[...]
    ne, nr, _ = aff.shape

    def count(mask):
        return jnp.sum(jnp.sum(mask.astype(F32), axis=2, keepdims=True), axis=1, keepdims=True)

    normal = count(aff >= F32_TINY) >= cap
    lo = jnp.where(normal, F32_TINY, 0.0)
    hi = jnp.where(normal, 2.0, F32_TINY)
    for it in range(BRACKET_GEO_STEPS + BRACKET_LIN_STEPS):
        mid = 0.5 * (lo + hi)
        if it < BRACKET_GEO_STEPS:
            mid = jnp.where(normal, jnp.sqrt(lo * hi), mid)
        ok = count(aff >= mid) >= cap
        lo = jnp.where(ok, mid, lo)
        hi = jnp.where(ok, hi, mid)
    thr = jnp.min(jnp.min(jnp.where(aff >= lo, aff, 2.0), axis=2, keepdims=True), axis=1, keepdims=True)
    above = aff > thr
    tied = aff == thr
    need = cap - count(above)

    kk = lax.broadcasted_iota(jnp.int32, (LANES, LANES), 0)
    ll = lax.broadcasted_iota(jnp.int32, (LANES, LANES), 1)
    upper = (kk <= ll).astype(BF16)
    ones = jnp.ones((LANES, LANES), BF16)
    ri = lax.broadcasted_iota(jnp.int32, (nr, nr), 0)
    rj = lax.broadcasted_iota(jnp.int32, (nr, nr), 1)
    lower = (rj < ri).astype(BF16)

    def exclusive_prefix(mask):
        m = mask.astype(BF16).reshape(ne * nr, LANES)
        in_row = jnp.dot(m, upper, preferred_element_type=F32).reshape(ne, nr, LANES)
        row_tot = jnp.dot(m, ones, preferred_element_type=F32).reshape(ne, nr, LANES)
        row_off = jnp.stack([jnp.dot(lower, row_tot[e].astype(BF16), preferred_element_type=F32)
                             for e in range(ne)])
        return in_row - mask.astype(F32) + row_off, row_off

    tie_rank, _ = exclusive_prefix(tied)
    sel = above | (tied & (tie_rank < need))
    pos, row_off = exclusive_prefix(sel)
    key_ref[...] = jnp.where(sel, pos, -1.0)
    offs_ref[...] = row_off


def _route(aff4, cap):
    bn, ne, nr, _ = aff4.shape
    spec = pl.BlockSpec((None, ne, nr, LANES), lambda b: (b, 0, 0, 0))
    return pl.pallas_call(
        functools.partial(_route_kernel, cap=cap),
        grid=(bn,),
        in_specs=[spec],
        out_specs=[spec, spec],
        out_shape=[jax.ShapeDtypeStruct(aff4.shape, F32)] * 2,
        compiler_params=_params(("arbitrary",), VMEM_LIMIT_BYTES),
        name="route",
    )(aff4)


def _compact_kernel(offs_sref, key_ref, aff_ref, wcast_ref, idx_ref, gate_ref, wcast_out_ref, acc_i, acc_g, *, cap):
    _cast_block(wcast_ref, wcast_out_ref)
    be = pl.program_id(0) * pl.num_programs(1) + pl.program_id(1)
    nr = key_ref.shape[0]
    acc_i[...] = jnp.zeros_like(acc_i)
    acc_g[...] = jnp.zeros_like(acc_g)
    slot = lax.broadcasted_iota(jnp.int32, (COMPACT_WIN, LANES), 0)
    lane = lax.broadcasted_iota(jnp.int32, (1, LANES), 1)

    def body(r, carry):
        off = offs_sref[be * nr + r]
        base = pl.multiple_of((off // SUBLANES) * SUBLANES, SUBLANES)
        win = pl.ds(base, COMPACT_WIN)
        hit = (slot + base) == key_ref[r].astype(jnp.int32)
        tok = (lane + r * LANES).astype(F32)
        acc_i[win, :] += jnp.where(hit, tok, 0.0)
        acc_g[win, :] += jnp.where(hit, aff_ref[r], 0.0)
        return carry

    lax.fori_loop(0, nr, body, 0)
    ones = jnp.ones((SUBLANES, LANES), F32)
    idx_rows = lax.dot_general(ones, acc_i[0:cap, :], (((1,), (1,)), ((), ())), precision=HIGHEST,
                               preferred_element_type=F32)
    idx_ref[...] = idx_rows[0:1, :].astype(jnp.int32)
    gate_ref[...] = jnp.sum(acc_g[0:cap, :], axis=1, keepdims=True)


def _compact(row_offs, key5, aff5, cap, w_cast):
    bn, ne, nr, _, _ = key5.shape
    spec_in = pl.BlockSpec((None, None, nr, 1, LANES), lambda b, e, offs: (b, e, 0, 0, 0))
    spec_out = pl.BlockSpec((None, None, cap, 1), lambda b, e, offs: (b, e, 0, 0))
    cast = _cast_spec(w_cast, bn * ne, lambda b, e, offs: b * ne + e)
    return pl.pallas_call(
        functools.partial(_compact_kernel, cap=cap),
        grid_spec=pltpu.PrefetchScalarGridSpec(
            num_scalar_prefetch=1,
            grid=(bn, ne),
            in_specs=[spec_in, spec_in, cast],
            out_specs=[pl.BlockSpec((None, None, 1, cap), lambda b, e, offs: (b, e, 0, 0)), spec_out, cast],
            scratch_shapes=[pltpu.VMEM((cap + COMPACT_WIN, LANES), F32),
                            pltpu.VMEM((cap + COMPACT_WIN, LANES), F32)]),
        out_shape=[jax.ShapeDtypeStruct((bn, ne, 1, cap), jnp.int32),
                   jax.ShapeDtypeStruct((bn, ne, cap, 1), F32),
                   jax.ShapeDtypeStruct(w_cast.shape, BF16)],
        compiler_params=_params(("arbitrary", "arbitrary"), VMEM_LIMIT_BYTES),
        name="compact",
    )(row_offs, key5, aff5, w_cast)


def _experts_kernel(rows_sref, h_hbm, gate_ref, wg_ref, wu_ref, wd_ref, res_ref, xg_ref, sem, *, cap):
    ne = pl.num_programs(1)
    step = pl.program_id(0) * ne + pl.program_id(1)
    last = pl.num_programs(0) * ne - 1
    slot = step % 2
    nxt = jnp.minimum(step + 1, last)

    def start_row(c, of_step, into_slot, priority=0):
        pltpu.make_async_copy(h_hbm.at[of_step // ne, _tile_at(rows_sref[of_step * cap + c]), :],
                              xg_ref.at[into_slot, _row_tile(c), :], sem.at[into_slot]).start(priority=priority)

    def wait_rows(of_slot):
        pltpu.make_async_copy(h_hbm.at[0, pl.ds(0, cap * ROW_CHUNKS), :], xg_ref.at[of_slot],
                              sem.at[of_slot]).wait()

    @pl.when(step == 0)
    def _first_rows():
        def body(c, carry):
            start_row(c, step, slot)
            return carry

        lax.fori_loop(0, cap, body, 0)

    wait_rows(slot)
    for m in range(cap // ROW_TILE):
        rows = slice(m * ROW_TILE, (m + 1) * ROW_TILE)
        for c in range(rows.start, rows.stop):
            start_row(c, nxt, 1 - slot, priority=c % 2)
        xr = _load_row_tiles(xg_ref.at[slot], rows.start, ROW_TILE).astype(BF16)
        hg = jnp.dot(xr, wg_ref[...], preferred_element_type=F32)
        hu = jnp.dot(xr, wu_ref[...], preferred_element_type=F32)
        hid = (hg * jax.nn.sigmoid(hg) * hu).astype(BF16)
        out = jnp.dot(hid, wd_ref[...], preferred_element_type=F32)
        _store_row_tiles(res_ref, rows.start, out * gate_ref[rows, :])

    @pl.when(step == last)
    def _drain():
        wait_rows(1 - slot)


def _experts(tile_rows, h, gate, wg, wu, wd, cap):
    bn = h.shape[0]
    ne = wg.shape[0]
    return pl.pallas_call(
        functools.partial(_experts_kernel, cap=cap),
        grid_spec=pltpu.PrefetchScalarGridSpec(
            num_scalar_prefetch=1,
            grid=(bn, ne),
            in_specs=[pl.BlockSpec(memory_space=pl.ANY),
                      pl.BlockSpec((None, None, cap, 1), lambda b, e, rows: (b, e, 0, 0)),
                      pl.BlockSpec((None, D_MODEL, D_FF), lambda b, e, rows: (e, 0, 0)),
                      pl.BlockSpec((None, D_MODEL, D_FF), lambda b, e, rows: (e, 0, 0)),
                      pl.BlockSpec((None, D_FF, D_MODEL), lambda b, e, rows: (e, 0, 0))],
            out_specs=pl.BlockSpec((None, None, cap * ROW_CHUNKS, LANES), lambda b, e, rows: (b, e, 0, 0)),
            scratch_shapes=[pltpu.VMEM((2, cap * ROW_CHUNKS, LANES), F32),
                            pltpu.SemaphoreType.DMA((2,))]),
        out_shape=jax.ShapeDtypeStruct((bn, ne, cap * ROW_CHUNKS, LANES), F32),
        compiler_params=_params(("arbitrary", "arbitrary"), VMEM_LIMIT_BYTES),
        name="experts",
    )(tile_rows, h, gate, wg, wu, wd)


SCATTER_UNROLL = 16


def _combine_kernel(rows_sref, res_ref, x1_ref, mod_ref, ln_ref, o_ref, acc_ref, *, cap, ne):
    b = pl.program_id(0)
    step = pl.program_id(1)

    @pl.when(step == 0)
    def _init():
        acc_ref[...] = jnp.zeros_like(acc_ref)

    @pl.when(step < ne)
    def _scatter():
        base = (b * ne + step) * cap

        def body(i, carry):
            first = i * SCATTER_UNROLL
            toks = [_tile_at(rows_sref[base + first + k]) for k in range(SCATTER_UNROLL)]
            sums = [acc_ref[toks[k], :] + res_ref[_row_tile(first + k), :] for k in range(SCATTER_UNROLL)]
            for k in range(SCATTER_UNROLL):
                acc_ref[toks[k], :] = sums[k]
            return carry

        lax.fori_loop(0, cap // SCATTER_UNROLL, body, 0)

    @pl.when(step >= ne)
    def _finish():
        moe = _load_row_tiles(acc_ref, (step - ne) * FIN_TILE, FIN_TILE)
        y = ALPHA * x1_ref[...] + mod_ref[5:6, :] * moe
        o_ref[...] = _layer_norm(y) * ln_ref[0:1, :] + ln_ref[1:2, :]


def _combine(tile_rows, res, x1, mod, ln2, cap):
    bn, length, _ = x1.shape
    ne = res.shape[1]
    assert cap % SCATTER_UNROLL == 0
    tok = lambda b, s, idx: (b, jnp.maximum(s - ne, 0), 0)
    return pl.pallas_call(
        functools.partial(_combine_kernel, cap=cap, ne=ne),
        grid_spec=pltpu.PrefetchScalarGridSpec(
            num_scalar_prefetch=1,
            grid=(bn, ne + length // FIN_TILE),
            in_specs=[pl.BlockSpec((None, None, cap * ROW_CHUNKS, LANES),
                                   lambda b, s, idx: (b, jnp.minimum(s, ne - 1), 0, 0)),
                      pl.BlockSpec((None, FIN_TILE, D_MODEL), tok),
                      pl.BlockSpec((None, 6, D_MODEL), lambda b, s, idx: (b, 0, 0)),
                      pl.BlockSpec((2, D_MODEL), lambda b, s, idx: (0, 0))],
            out_specs=pl.BlockSpec((None, FIN_TILE, D_MODEL), tok),
            scratch_shapes=[pltpu.VMEM((length * ROW_CHUNKS, LANES), F32)]),
        out_shape=jax.ShapeDtypeStruct((bn, length, D_MODEL), F32),
        compiler_params=_params(("arbitrary", "arbitrary"), VMEM_LIMIT_BYTES),
        name="combine",
    )(tile_rows, res, x1, mod, ln2)


def kernel(x, c, ctx, c_ctx, w_ada, b_ada, w_in, gm_ws, gm_bs, s5_a_re, s5_a_im, s5_log_step, s5_b_re, s5_b_im,
           s5_c_re, s5_c_im, s5_d, s5_w_glu, s5_b_glu, w_out, ln1_g, ln1_b, w_router, moe_w_gate, moe_w_up,
           moe_w_down, ln2_g, ln2_b):
    assert w_ada.shape[0] == DEPTH == 1
    bn, length, _ = x.shape
    assert bn == SCAN_HALF and ctx.shape[1] == SCAN_T and length % TOK_TILE == 0
    cap = CAPACITY_FACTOR * length // N_EXPERTS

    cc = jnp.zeros((SUBLANES, D_MODEL), F32).at[:bn].set(c).at[bn].set(c_ctx)
    mod = _adaln(cc, w_ada[0], b_ada[0][None, :])
    mod_lat = mod[:bn].reshape(bn, 6, D_MODEL)
    mod_ctx = mod[bn:bn + 1].reshape(1, 6, D_MODEL)

    w_in_b = w_in[0].astype(BF16)
    zs, gm, wg_b = _inproj(x, mod_lat, w_in_b, gm_ws[0].astype(BF16), gm_bs[0][:, :, None], moe_w_gate[0])
    zc = _ctxproj(ctx, mod_ctx, w_in_b)

    bmat, cmat, tab = _s5_tables(s5_a_re[0], s5_a_im[0], s5_log_step[0], s5_b_re[0], s5_b_im[0],
                                 s5_c_re[0], s5_c_im[0])
    yf, yb, wd_b = _s5scan(zs, zc, bmat, cmat, tab, moe_w_down[0])

    w_rt = w_router[0].T
    w_rt_hi = w_rt.astype(BF16)
    w_rt_parts = jnp.stack([w_rt_hi, (w_rt - w_rt_hi.astype(F32)).astype(BF16)])
    x1, hp, aff = _outproj(x, zs, yf, yb, gm, mod_lat, s5_d[0].reshape(1, S5_W), s5_w_glu[0].astype(BF16),
                           s5_b_glu[0][None, :], w_out[0].astype(BF16), jnp.stack([ln1_g[0], ln1_b[0]]),
                           w_rt_parts)

    aff4 = aff.reshape(bn, N_EXPERTS, length // LANES, LANES)
    key4, offs4 = _route(aff4, cap)
    row_offs = offs4[..., 0].astype(jnp.int32)
    rows5 = (bn, N_EXPERTS, length // LANES, 1, LANES)
    idx, gate, wu_b = _compact(row_offs.reshape(-1), key4.reshape(rows5), aff4.reshape(rows5), cap, moe_w_up[0])
    tile_rows = idx.reshape(-1) * ROW_CHUNKS

    res = _experts(tile_rows, hp, gate, wg_b, wu_b, wd_b, cap)

    return _combine(tile_rows, res, x1, mod_lat, jnp.stack([ln2_g[0], ln2_b[0]]), cap)
```
